```python
import jax, jax.numpy as jnp
from jax import lax
import numpy as np

D_MODEL = 1024
BATCH = 2
SEQ = 16384
DEPTH = 4
DEC_BATCH = 8
DEC_SEQ = 16
PAST_LEN = 2048

CHUNK = 64
MLA_HEADS = 8
QK_NOPE = 64
QK_ROPE = 32
V_DIM = 64
Q_LORA = 384
KV_LORA = 256
ROPE_THETA = 10000.0
Q_BLOCK = 128
RWKV_HEADS = 8
RWKV_HEAD = 64
RWKV_DIM = RWKV_HEADS * RWKV_HEAD
W_LORA = 64
A_LORA = 64
G_LORA = 128
RWKV_GN_EPS = 64e-5
SSM_HEADS = 8
SSM_HEADDIM = 64
SSM_DIM = SSM_HEADS * SSM_HEADDIM
SSM_STATE = 64
SSM_GROUPS = 2
CONV_W = 4
CONV_DIM = SSM_DIM + 2 * SSM_GROUPS * SSM_STATE
N_BRANCH = 3
BRANCH_DIM = 512
N_EXPERTS = 32
TOP_K = 4
D_FF = 1024
SWIGLU_LIMIT = 7.0
SWIGLU_ALPHA = 1.702
MOE_BLOCK = 128
DEEPNORM_ALPHA = (2.0 * DEPTH) ** 0.25
DEEPNORM_BETA = (8.0 * DEPTH) ** -0.25
LN_EPS = 1e-5
RMS_EPS = 1e-6
MLA_COLS = Q_LORA + KV_LORA + QK_ROPE
RWKV_COLS = 3 * RWKV_DIM + W_LORA + A_LORA + G_LORA
SSM_COLS = SSM_DIM + CONV_DIM + SSM_HEADS
GATE_COLS = N_BRANCH * D_MODEL
IN_COLS = MLA_COLS + RWKV_COLS + SSM_COLS + GATE_COLS

kernel_name = 'hybrid_mla_rwkv7_mamba2_moe_stream_step'


def layer_norm(x, g, b):
    xf = x.astype(jnp.float32)
    mu = xf.mean(-1, keepdims=True)
    var = jnp.square(xf - mu).mean(-1, keepdims=True)
    return ((xf - mu) * lax.rsqrt(var + LN_EPS) * g + b).astype(x.dtype)


def rms_norm(x, g):
    xf = x.astype(jnp.float32)
    return (xf * lax.rsqrt(jnp.mean(xf * xf, -1, keepdims=True) + RMS_EPS) * g).astype(x.dtype)


def rope(x, pos):
    half = x.shape[-1] // 2
    inv_freq = ROPE_THETA ** (-jnp.arange(half, dtype=jnp.float32) / half)
    ang = pos.astype(jnp.float32)[:, None] * inv_freq[None, :]
    shape = (pos.shape[0],) + (1,) * (x.ndim - 3) + (half,)
    cos, sin = jnp.cos(ang).reshape(shape), jnp.sin(ang).reshape(shape)
    x1, x2 = x[..., :half].astype(jnp.float32), x[..., half:].astype(jnp.float32)
    return jnp.concatenate([x1 * cos - x2 * sin, x2 * cos + x1 * sin], axis=-1).astype(x.dtype)


def block_causal_attention(q, k, v, q_chunk, k_chunk):
    b, Sq, H, Dqk = q.shape
    blk = min(Q_BLOCK, Sq)
    nb = Sq // blk
    scale = Dqk ** -0.5
    qb = q.reshape(b, nb, blk, H, Dqk).swapaxes(0, 1)
    qc = q_chunk.reshape(nb, blk)

    def one_block(args):
        qblk, qcb = args
        s = jnp.einsum('bqhd,bkhd->bhqk', qblk, k, preferred_element_type=jnp.float32) * scale
        s = jnp.where((k_chunk[None, :] <= qcb[:, None])[None, None], s, -jnp.inf)
        p = jax.nn.softmax(s, axis=-1).astype(v.dtype)
        return jnp.einsum('bhqk,bkhd->bqhd', p, v)

    o = lax.map(one_block, (qb, qc))
    return o.swapaxes(0, 1).reshape(b, Sq, H, v.shape[-1])


def mla_branch(u, pos, past_latent, past_krope, q_norm, kv_norm, w_uq, w_ukv):
    b, S, _ = u.shape
    c_q, c_kv, k_r = jnp.split(u, [Q_LORA, Q_LORA + KV_LORA], axis=-1)
    q = (rms_norm(c_q, q_norm) @ w_uq).reshape(b, S, MLA_HEADS, QK_NOPE + QK_ROPE)
    q = jnp.concatenate([q[..., :QK_NOPE], rope(q[..., QK_NOPE:], pos)], axis=-1)
    latent = rms_norm(c_kv, kv_norm)
    k_rope = rope(k_r, pos)
    if past_latent is None:
        lat_all, kr_all, kpos = latent, k_rope, pos
    else:
        lat_all = jnp.concatenate([past_latent.astype(latent.dtype), latent], axis=1)
        kr_all = jnp.concatenate([past_krope.astype(k_rope.dtype), k_rope], axis=1)
        kpos = jnp.concatenate([jnp.arange(past_latent.shape[1], dtype=pos.dtype), pos])
    Sk = lat_all.shape[1]
    kv = (lat_all @ w_ukv).reshape(b, Sk, MLA_HEADS, QK_NOPE + V_DIM)
    k = jnp.concatenate([kv[..., :QK_NOPE], jnp.broadcast_to(kr_all[:, :, None, :], (b, Sk, MLA_HEADS, QK_ROPE))], axis=-1)
    v = kv[..., QK_NOPE:]
    o = block_causal_attention(q, k, v, pos // CHUNK, kpos // CHUNK)
    return o.reshape(b, S, MLA_HEADS * V_DIM).astype(u.dtype), latent, k_rope


def rwkv7_scan(r, w, k, v, kk, a, s0):
    def step(st, inp):
        r_t, w_t, k_t, v_t, kk_t, a_t = inp
        sk = jnp.einsum('bhvk,bhk->bhv', st, kk_t)
        st = st * w_t[:, :, None, :] - sk[..., None] * (kk_t * a_t)[:, :, None, :] + v_t[..., None] * k_t[:, :, None, :]
        return st, jnp.einsum('bhvk,bhk->bhv', st, r_t)
    xs = (r.swapaxes(0, 1), w.swapaxes(0, 1), k.swapaxes(0, 1), v.swapaxes(0, 1), kk.swapaxes(0, 1), a.swapaxes(0, 1))
    s_last, ys = lax.scan(step, s0.astype(jnp.float32), xs)
    return ys.swapaxes(0, 1), s_last


def rwkv7_branch(u, s0, shift_prev, mu, w0, w_w2, a0, w_a2, w_g2, k_k, k_a, r_k, gn_g, gn_b):
    f32 = jnp.float32
    b, S, _ = u.shape
    shifted = jnp.concatenate([shift_prev.astype(u.dtype), u[:, :-1]], axis=1)
    m = u + (shifted - u) * mu
    r, k, v, wl, al, gl = jnp.split(m, [RWKV_DIM, 2 * RWKV_DIM, 3 * RWKV_DIM, 3 * RWKV_DIM + W_LORA, 3 * RWKV_DIM + W_LORA + A_LORA], axis=-1)
    decay = jnp.exp(-jnp.exp(-jax.nn.softplus(-(w0 + jnp.tanh(wl) @ w_w2).astype(f32)) - 0.5))
    a = jax.nn.sigmoid((a0 + al @ w_a2).astype(f32))
    g = jax.nn.sigmoid(gl) @ w_g2

    def heads(t):
        return t.astype(f32).reshape(b, S, RWKV_HEADS, RWKV_HEAD)

    kk = heads(k * k_k)
    kk = kk * lax.rsqrt(jnp.sum(kk * kk, axis=-1, keepdims=True) + 1e-12)
    k_h = heads(k.astype(f32) * (1.0 + (a - 1.0) * k_a.astype(f32)))
    r_h, v_h, a_h, w_h = heads(r), heads(v), heads(a), heads(decay)
    o, s_new = rwkv7_scan(r_h, w_h, k_h, v_h, kk, a_h, s0)
    mean = o.mean(-1, keepdims=True)
    var = jnp.square(o - mean).mean(-1, keepdims=True)
    o = ((o - mean) * lax.rsqrt(var + RWKV_GN_EPS)).reshape(b, S, RWKV_DIM) * gn_g + gn_b
    bonus = jnp.sum(r_h * k_h * r_k.astype(f32).reshape(RWKV_HEADS, RWKV_HEAD), -1, keepdims=True) * v_h
    y = (o + bonus.reshape(b, S, RWKV_DIM)) * g
    return y.astype(u.dtype), s_new, u[:, -1:]


def ssd_scan(x, dt, A, Bm, Cm, h0):
    f32 = jnp.float32
    b, S, H, P = x.shape
    G, N = Bm.shape[2], Bm.shape[3]
    L = min(CHUNK, S)
    nc = S // L
    rep = H // G
    xc = x.astype(f32).reshape(b, nc, L, H, P)
    Bc = jnp.repeat(Bm.astype(f32), rep, axis=2).reshape(b, nc, L, H, N)
    Cc = jnp.repeat(Cm.astype(f32), rep, axis=2).reshape(b, nc, L, H, N)
    dtc = dt.reshape(b, nc, L, H)
    cum = jnp.cumsum(dtc * A, axis=2)
    seg = cum[:, :, :, None, :] - cum[:, :, None, :, :]
    causal = jnp.tril(jnp.ones((L, L), dtype=bool))
    decay_ts = jnp.exp(jnp.where(causal[None, None, :, :, None], seg, -jnp.inf))
    scores = jnp.einsum('bcthn,bcshn->bctsh', Cc, Bc) * decay_ts * dtc[:, :, None, :, :]
    y_diag = jnp.einsum('bctsh,bcshp->bcthp', scores, xc)
    decay_end = jnp.exp(cum[:, :, -1:, :] - cum)
    chunk_states = jnp.einsum('bcsh,bcshn,bcshp->bchpn', decay_end * dtc, Bc, xc)
    chunk_decay = jnp.exp(cum[:, :, -1, :])

    def step(h, inp):
        cs, cd = inp
        return h * cd[:, :, None, None] + cs, h

    h_last, h_starts = lax.scan(step, h0.astype(f32), (chunk_states.swapaxes(0, 1), chunk_decay.swapaxes(0, 1)))
    h_starts = h_starts.swapaxes(0, 1)
    y_off = jnp.einsum('bcthn,bchpn->bcthp', Cc, h_starts) * jnp.exp(cum)[..., None]
    return (y_diag + y_off).reshape(b, S, H, P), h_last


def mamba2_branch(u, s0, conv_prev, conv_w, conv_b, dt_bias, a_log, d_skip, norm_g):
    f32 = jnp.float32
    b, S, _ = u.shape
    z, xbc, dt = jnp.split(u, [SSM_DIM, SSM_DIM + CONV_DIM], axis=-1)
    xpad = jnp.concatenate([conv_prev.astype(u.dtype), xbc], axis=1)
    conv = conv_b + sum(xpad[:, j:j + S] * conv_w[j] for j in range(CONV_W))
    xbc_act = jax.nn.silu(conv)
    xs, Bm, Cm = jnp.split(xbc_act, [SSM_DIM, SSM_DIM + SSM_GROUPS * SSM_STATE], axis=-1)
    xs = xs.reshape(b, S, SSM_HEADS, SSM_HEADDIM)
    Bm = Bm.reshape(b, S, SSM_GROUPS, SSM_STATE)
    Cm = Cm.reshape(b, S, SSM_GROUPS, SSM_STATE)
    dt = jax.nn.softplus(dt.astype(f32) + dt_bias.astype(f32))
    A = -jnp.exp(a_log.astype(f32))
    y, s_new = ssd_scan(xs, dt, A, Bm, Cm, s0)
    y = y + d_skip.astype(f32)[:, None] * xs.astype(f32)
    y = y.reshape(b, S, SSM_DIM) * jax.nn.silu(z.astype(f32))
    yg = y.reshape(b, S, SSM_GROUPS, SSM_DIM // SSM_GROUPS)
    yg = yg * lax.rsqrt(jnp.mean(yg * yg, -1, keepdims=True) + RMS_EPS)
    y = yg.reshape(b, S, SSM_DIM) * norm_g
    return y.astype(u.dtype), s_new, xpad[:, -(CONV_W - 1):]


def moe_ffn(x, router_w, router_b, w_gu, b_gu, w_down, b_down):
    b, S, D = x.shape
    xt = x.reshape(-1, D)
    T = xt.shape[0]
    logits = (xt @ router_w).astype(jnp.float32) + router_b.astype(jnp.float32)
    top_val, top_idx = lax.top_k(logits, TOP_K)
    gates = jax.nn.softmax(top_val, axis=-1)
    n_assign = T * TOP_K
    flat_e = top_idx.reshape(-1)
    order = jnp.argsort(flat_e)
    sorted_e = flat_e[order]
    sorted_tok = (order // TOP_K).astype(jnp.int32)
    counts = jnp.zeros((N_EXPERTS,), jnp.int32).at[flat_e].add(1)
    padded = (counts + MOE_BLOCK - 1) // MOE_BLOCK * MOE_BLOCK
    pad_end = jnp.cumsum(padded)
    pad_start = pad_end - padded
    start = jnp.cumsum(counts) - counts
    dest = pad_start[sorted_e] + (jnp.arange(n_assign, dtype=jnp.int32) - start[sorted_e])
    n_blocks = -(-(n_assign + N_EXPERTS * (MOE_BLOCK - 1)) // MOE_BLOCK)
    n_rows = n_blocks * MOE_BLOCK
    row_tok = jnp.full((n_rows,), T, jnp.int32).at[dest].set(sorted_tok)
    block_e = jnp.minimum(jnp.searchsorted(pad_end, jnp.arange(n_blocks, dtype=jnp.int32) * MOE_BLOCK, side='right'), N_EXPERTS - 1)
    x_pad = jnp.concatenate([xt, jnp.zeros((1, D), xt.dtype)], axis=0)
    xb = x_pad[row_tok].reshape(n_blocks, MOE_BLOCK, D)

    def expert_block(args):
        xblk, e = args
        hgu = xblk @ w_gu[e] + b_gu[e]
        gate = jnp.minimum(hgu[:, :D_FF], SWIGLU_LIMIT)
        up = jnp.clip(hgu[:, D_FF:], -SWIGLU_LIMIT, SWIGLU_LIMIT)
        hid = gate * jax.nn.sigmoid(SWIGLU_ALPHA * gate) * (up + 1.0)
        return hid @ w_down[e] + b_down[e]

    yb = lax.map(expert_block, (xb, block_e)).reshape(n_rows, D)
    y_assign = jnp.zeros((n_assign, D), yb.dtype).at[order].set(yb[dest])
    y = jnp.einsum('tkd,tk->td', y_assign.reshape(T, TOP_K, D), gates.astype(yb.dtype))
    return y.reshape(b, S, D).astype(x.dtype)


def trunk_layer(x, pos, past_latent, past_krope, rwkv_s0, shift_prev, ssm_s0, conv_prev, lp):
    b, S, _ = x.shape
    u = x @ lp['w_in']
    u_mla, u_rwkv, u_ssm, u_gate = jnp.split(u, [MLA_COLS, MLA_COLS + RWKV_COLS, MLA_COLS + RWKV_COLS + SSM_COLS], axis=-1)
    y_mla, new_latent, new_krope = mla_branch(u_mla, pos, past_latent, past_krope, lp['q_norm'], lp['kv_norm'], lp['w_uq'], lp['w_ukv'])
    y_rwkv, rwkv_s, shift_new = rwkv7_branch(u_rwkv, rwkv_s0, shift_prev, lp['mu'], lp['w0'], lp['w_w2'], lp['a0'], lp['w_a2'], lp['w_g2'], lp['k_k'], lp['k_a'], lp['r_k'], lp['gn_g'], lp['gn_b'])
    y_ssm, ssm_s, conv_new = mamba2_branch(u_ssm, ssm_s0, conv_prev, lp['conv_w'], lp['conv_b'], lp['dt_bias'], lp['a_log'], lp['d_skip'], lp['ssm_norm_g'])
    y_br = jnp.einsum('bsgc,gcd->bsgd', jnp.stack([y_mla, y_rwkv, y_ssm], axis=2), lp['w_branch'])
    gates = jax.nn.sigmoid(u_gate.reshape(b, S, N_BRANCH, D_MODEL))
    mix = jnp.einsum('bsgd,bsgd->bsd', gates, y_br) @ lp['w_out']
    h = layer_norm(DEEPNORM_ALPHA * x + mix, lp['ln1_g'], lp['ln1_b'])
    f = moe_ffn(h, lp['router_w'], lp['router_b'], lp['w_gu'], lp['b_gu'], lp['w_down'], lp['b_down'])
    out = layer_norm(DEEPNORM_ALPHA * h + f, lp['ln2_g'], lp['ln2_b'])
    return out, (new_latent, new_krope, rwkv_s, shift_new, ssm_s, conv_new)


def setup_inputs(seed: int = 0) -> dict:
    key = jax.random.key(seed)
    keys = iter(jax.random.split(key, 48))
    f32 = jnp.float32
    L = DEPTH

    def nrm(shape, scale):
        return jax.random.normal(next(keys), shape, f32) * scale

    def unif(shape, lo, hi):
        return jax.random.uniform(next(keys), shape, f32, lo, hi)

    def gain(shape):
        return 1.0 + nrm(shape, 0.02)

    inp = {}
    inp['x_prompt'] = nrm((BATCH, SEQ, D_MODEL), 1.0)
    inp['x_sample'] = nrm((DEC_BATCH, DEC_SEQ, D_MODEL), 1.0)
    inp['cache_mla_latent'] = nrm((L, DEC_BATCH, PAST_LEN, KV_LORA), 1.0)
    inp['cache_mla_krope'] = nrm((L, DEC_BATCH, PAST_LEN, QK_ROPE), 1.0)
    inp['state_rwkv'] = nrm((L, DEC_BATCH, RWKV_HEADS, RWKV_HEAD, RWKV_HEAD), 0.1)
    inp['state_rwkv_shift'] = nrm((L, DEC_BATCH, 1, RWKV_COLS), 1.0)
    inp['state_ssm'] = nrm((L, DEC_BATCH, SSM_HEADS, SSM_HEADDIM, SSM_STATE), 0.1)
    inp['state_ssm_conv'] = nrm((L, DEC_BATCH, CONV_W - 1, CONV_DIM), 1.0)
    inp['w_in'] = nrm((L, D_MODEL, IN_COLS), D_MODEL ** -0.5)
    inp['mla_q_norm'] = gain((L, Q_LORA))
    inp['mla_kv_norm'] = gain((L, KV_LORA))
    inp['mla_w_uq'] = nrm((L, Q_LORA, MLA_HEADS * (QK_NOPE + QK_ROPE)), Q_LORA ** -0.5)
    inp['mla_w_ukv'] = nrm((L, KV_LORA, MLA_HEADS * (QK_NOPE + V_DIM)), KV_LORA ** -0.5)
    inp['rwkv_mu'] = unif((L, RWKV_COLS), 0.0, 1.0)
    inp['rwkv_w0'] = unif((L, RWKV_DIM), -6.0, 0.0)
    inp['rwkv_w_w2'] = nrm((L, W_LORA, RWKV_DIM), 0.1 * W_LORA ** -0.5)
    inp['rwkv_a0'] = nrm((L, RWKV_DIM), 0.1)
    inp['rwkv_w_a2'] = nrm((L, A_LORA, RWKV_DIM), A_LORA ** -0.5)
    inp['rwkv_w_g2'] = nrm((L, G_LORA, RWKV_DIM), G_LORA ** -0.5)
    inp['rwkv_k_k'] = 0.85 + nrm((L, RWKV_DIM), 0.02)
    inp['rwkv_k_a'] = gain((L, RWKV_DIM))
    inp['rwkv_r_k'] = nrm((L, RWKV_DIM), 0.1)
    inp['rwkv_gn_g'] = gain((L, RWKV_DIM))
    inp['rwkv_gn_b'] = nrm((L, RWKV_DIM), 0.02)
    inp['ssm_conv_w'] = nrm((L, CONV_W, CONV_DIM), 0.5)
    inp['ssm_conv_b'] = nrm((L, CONV_DIM), 0.02)
    dt0 = jnp.exp(unif((L, SSM_HEADS), float(np.log(1e-3)), float(np.log(1e-1))))
    inp['ssm_dt_bias'] = jnp.log(jnp.expm1(dt0))
    inp['ssm_a_log'] = jnp.log(unif((L, SSM_HEADS), 1.0, 16.0))
    inp['ssm_d'] = gain((L, SSM_HEADS))
    inp['ssm_norm_g'] = gain((L, SSM_DIM))
    inp['w_branch'] = nrm((L, N_BRANCH, BRANCH_DIM, D_MODEL), BRANCH_DIM ** -0.5)
    inp['w_out'] = nrm((L, D_MODEL, D_MODEL), DEEPNORM_BETA * D_MODEL ** -0.5)
    inp['ln1_g'] = gain((L, D_MODEL))
    inp['ln1_b'] = nrm((L, D_MODEL), 0.02)
    inp['router_w'] = nrm((L, D_MODEL, N_EXPERTS), D_MODEL ** -0.5)
    inp['router_b'] = nrm((L, N_EXPERTS), 0.01)
    inp['expert_w_gu'] = nrm((L, N_EXPERTS, D_MODEL, 2 * D_FF), D_MODEL ** -0.5)
    inp['expert_b_gu'] = nrm((L, N_EXPERTS, 2 * D_FF), 0.02)
    inp['expert_w_down'] = nrm((L, N_EXPERTS, D_FF, D_MODEL), DEEPNORM_BETA * D_FF ** -0.5)
    inp['expert_b_down'] = nrm((L, N_EXPERTS, D_MODEL), 0.02)
    inp['ln2_g'] = gain((L, D_MODEL))
    inp['ln2_b'] = nrm((L, D_MODEL), 0.02)
    return inp


def reference(x_prompt, x_sample, cache_mla_latent, cache_mla_krope, state_rwkv, state_rwkv_shift, state_ssm, state_ssm_conv,
              w_in, mla_q_norm, mla_kv_norm, mla_w_uq, mla_w_ukv,
              rwkv_mu, rwkv_w0, rwkv_w_w2, rwkv_a0, rwkv_w_a2, rwkv_w_g2, rwkv_k_k, rwkv_k_a, rwkv_r_k, rwkv_gn_g, rwkv_gn_b,
              ssm_conv_w, ssm_conv_b, ssm_dt_bias, ssm_a_log, ssm_d, ssm_norm_g,
              w_branch, w_out, ln1_g, ln1_b,
              router_w, router_b, expert_w_gu, expert_b_gu, expert_w_down, expert_b_down, ln2_g, ln2_b):
    bp, sp = x_prompt.shape[0], x_prompt.shape[1]
    past_len = cache_mla_latent.shape[2]
    pos_p = jnp.arange(sp, dtype=jnp.int32)
    pos_s = past_len + jnp.arange(x_sample.shape[1], dtype=jnp.int32)
    zero_rwkv = jnp.zeros((bp, RWKV_HEADS, RWKV_HEAD, RWKV_HEAD), jnp.float32)
    zero_shift = jnp.zeros((bp, 1, RWKV_COLS), x_prompt.dtype)
    zero_ssm = jnp.zeros((bp, SSM_HEADS, SSM_HEADDIM, SSM_STATE), jnp.float32)
    zero_conv = jnp.zeros((bp, CONV_W - 1, CONV_DIM), x_prompt.dtype)
    yp, ys = x_prompt, x_sample
    st_p = [[] for _ in range(6)]
    st_s = [[] for _ in range(6)]
    for l in range(DEPTH):
        lp = dict(w_in=w_in[l], q_norm=mla_q_norm[l], kv_norm=mla_kv_norm[l], w_uq=mla_w_uq[l], w_ukv=mla_w_ukv[l],
                  mu=rwkv_mu[l], w0=rwkv_w0[l], w_w2=rwkv_w_w2[l], a0=rwkv_a0[l], w_a2=rwkv_w_a2[l], w_g2=rwkv_w_g2[l],
                  k_k=rwkv_k_k[l], k_a=rwkv_k_a[l], r_k=rwkv_r_k[l], gn_g=rwkv_gn_g[l], gn_b=rwkv_gn_b[l],
                  conv_w=ssm_conv_w[l], conv_b=ssm_conv_b[l], dt_bias=ssm_dt_bias[l], a_log=ssm_a_log[l], d_skip=ssm_d[l],
                  ssm_norm_g=ssm_norm_g[l], w_branch=w_branch[l], w_out=w_out[l], ln1_g=ln1_g[l], ln1_b=ln1_b[l],
                  router_w=router_w[l], router_b=router_b[l], w_gu=expert_w_gu[l], b_gu=expert_b_gu[l],
                  w_down=expert_w_down[l], b_down=expert_b_down[l], ln2_g=ln2_g[l], ln2_b=ln2_b[l])
        yp, new_p = trunk_layer(yp, pos_p, None, None, zero_rwkv, zero_shift, zero_ssm, zero_conv, lp)
        ys, new_s = trunk_layer(ys, pos_s, cache_mla_latent[l], cache_mla_krope[l], state_rwkv[l], state_rwkv_shift[l],
                                state_ssm[l], state_ssm_conv[l], lp)
        for i in range(6):
            st_p[i].append(new_p[i])
            st_s[i].append(new_s[i])
    latent_p, krope_p, rwkv_p, shift_p, ssm_p, conv_p = [jnp.stack(t, axis=0) for t in st_p]
    latent_s, krope_s, rwkv_s, shift_s, ssm_s, conv_s = [jnp.stack(t, axis=0) for t in st_s]
    return (yp, ys, latent_p, krope_p, rwkv_p, shift_p, ssm_p, conv_p, latent_s, krope_s, rwkv_s, shift_s, ssm_s, conv_s)
```

```python
import functools
import math

import jax
import jax.numpy as jnp
from jax import lax
from jax.experimental import pallas as pl
from jax.experimental.pallas import tpu as pltpu

F32 = jnp.float32
BF16 = jnp.bfloat16

D_MODEL = 1024
DEPTH = 4
CHUNK = 64
MLA_HEADS = 8
QK_NOPE = 64
QK_ROPE = 32
QK_DIM = QK_NOPE + QK_ROPE
V_DIM = 64
Q_LORA = 384
KV_LORA = 256
ROPE_THETA = 10000.0
RWKV_HEADS = 8
RWKV_HEAD = 64
RWKV_DIM = RWKV_HEADS * RWKV_HEAD
W_LORA = 64
A_LORA = 64
G_LORA = 128
RWKV_GN_EPS = 64e-5
SSM_HEADS = 8
SSM_HEADDIM = 64
SSM_DIM = SSM_HEADS * SSM_HEADDIM
SSM_STATE = 64
SSM_GROUPS = 2
CONV_W = 4
CONV_DIM = SSM_DIM + 2 * SSM_GROUPS * SSM_STATE
N_BRANCH = 3
BRANCH_DIM = 512
N_EXPERTS = 32
TOP_K = 4
D_FF = 1024
SWIGLU_LIMIT = 7.0
SWIGLU_ALPHA = 1.702
DEEPNORM_ALPHA = (2.0 * DEPTH) ** 0.25
LN_EPS = 1e-5
RMS_EPS = 1e-6
MLA_COLS = Q_LORA + KV_LORA + QK_ROPE
RWKV_COLS = 3 * RWKV_DIM + W_LORA + A_LORA + G_LORA
SSM_COLS = SSM_DIM + CONV_DIM + SSM_HEADS

LANES = 128
MLA_PROJ_COLS = 768
SSM_PROJ_COLS = SSM_DIM + CONV_DIM + LANES
MOE_ROWS = 256
NEG_BIG = -1e30
VMEM_LIMIT = 56 * 1024 * 1024


def _cparams(*sem):
    return pltpu.CompilerParams(dimension_semantics=sem, vmem_limit_bytes=VMEM_LIMIT)


def _dot(a, b):
    return jnp.dot(a.astype(BF16), b.astype(BF16), preferred_element_type=F32)


def _dot_nt(a, b):
    return lax.dot_general(a.astype(BF16), b.astype(BF16), (((1,), (1,)), ((), ())), preferred_element_type=F32)


def _dot_tn(a, b):
    return lax.dot_general(a.astype(BF16), b.astype(BF16), (((0,), (0,)), ((), ())), preferred_element_type=F32)


def _split3(x):
    hi = x.astype(BF16)
    r1 = x - hi.astype(F32)
    mid = r1.astype(BF16)
    lo = (r1 - mid.astype(F32)).astype(BF16)
    return hi, mid, lo


def _dot_exact_lhs(m, x):
    hi, mid, lo = _split3(x)
    return (jnp.dot(m, hi, preferred_element_type=F32) + jnp.dot(m, mid, preferred_element_type=F32)
            + jnp.dot(m, lo, preferred_element_type=F32))


def _dot_nt_exact_lhs(m, x):
    dn = (((1,), (1,)), ((), ()))
    hi, mid, lo = _split3(x)
    return (lax.dot_general(m, hi, dn, preferred_element_type=F32) + lax.dot_general(m, mid, dn, preferred_element_type=F32)
            + lax.dot_general(m, lo, dn, preferred_element_type=F32))


def _sigmoid(x):
    return 1.0 / (1.0 + jnp.exp(-x))


def _softplus(x):
    return jnp.maximum(x, 0.0) + jnp.log(1.0 + jnp.exp(-jnp.abs(x)))


def _rms(x, g):
    return x * lax.rsqrt(jnp.mean(x * x, axis=-1, keepdims=True) + RMS_EPS) * g


def _layer_norm(x, g, b):
    mu = jnp.mean(x, axis=-1, keepdims=True)
    xc = x - mu
    var = jnp.mean(xc * xc, axis=-1, keepdims=True)
    return xc * lax.rsqrt(var + LN_EPS) * g + b


def _full(shape):
    return pl.BlockSpec(shape, lambda *_: (0,) * len(shape))


def _mla_prep_kernel(x_ref, w_ref, qn_ref, kvn_ref, wq_ref, cq_ref, sq_ref, ck_ref, sk_ref, q_ref, lat_ref, kr_ref):
    u = _dot(x_ref[0], w_ref[...])
    c_q = u[:, :Q_LORA]
    c_kv = u[:, Q_LORA:Q_LORA + KV_LORA]
    kr = u[:, MLA_COLS - QK_ROPE:MLA_COLS]
    kr_rot = u[:, MLA_COLS:MLA_COLS + QK_ROPE]
    qall = _dot(_rms(c_q, qn_ref[...]), wq_ref[...])
    nope_w = MLA_HEADS * QK_NOPE
    rope_w = MLA_HEADS * QK_ROPE
    q_rope = qall[:, nope_w:nope_w + rope_w] * cq_ref[0] + qall[:, nope_w + rope_w:] * sq_ref[0]
    scale = QK_DIM ** -0.5
    for h in range(MLA_HEADS):
        qh = jnp.concatenate([qall[:, h * QK_NOPE:(h + 1) * QK_NOPE], q_rope[:, h * QK_ROPE:(h + 1) * QK_ROPE]], axis=1)
        q_ref[0, h] = (qh * scale).astype(BF16)
    lat_ref[0] = _rms(c_kv, kvn_ref[...])
    kr_ref[0] = kr * ck_ref[0] + kr_rot * sk_ref[0]


def _kv_up_kernel(lat_ref, kr_ref, w_ref, k_ref, v_ref):
    kv = _dot(lat_ref[0], w_ref[...])
    kr = kr_ref[0]
    hw = QK_NOPE + V_DIM
    for h in range(MLA_HEADS):
        k_ref[0, h] = jnp.concatenate([kv[:, h * hw:h * hw + QK_NOPE], kr], axis=1).astype(BF16)
        v_ref[0, h] = kv[:, h * hw + QK_NOPE:(h + 1) * hw].astype(BF16)


def _last_kv_block(qi, tq, tk, q_off, nk):
    q_hi = q_off + qi * tq + tq - 1
    return jnp.minimum(nk - 1, ((q_hi // CHUNK) * CHUNK + CHUNK - 1) // tk)


def _flash_kernel(q_ref, k_ref, v_ref, o_ref, m_sc, l_sc, acc_sc, *, tq, tk, q_off, sk_valid, nk):
    qi = pl.program_id(1)
    kj = pl.program_id(2)

    @pl.when(kj == 0)
    def _():
        m_sc[...] = jnp.full(m_sc.shape, NEG_BIG, F32)
        l_sc[...] = jnp.zeros(l_sc.shape, F32)
        acc_sc[...] = jnp.zeros(acc_sc.shape, F32)

    q_lo = q_off + qi * tq
    q_hi = q_lo + tq - 1
    k_lo = kj * tk
    k_hi = k_lo + tk - 1
    needed = (k_lo // CHUNK) <= (q_hi // CHUNK)
    full = jnp.logical_and((k_hi // CHUNK) <= (q_lo // CHUNK), k_hi < sk_valid)

    def body(masked):
        if masked:
            row = q_lo + lax.broadcasted_iota(jnp.int32, (tq, tk), 0)
            col = k_lo + lax.broadcasted_iota(jnp.int32, (tq, tk), 1)
            mask = jnp.logical_and((col // CHUNK) <= (row // CHUNK), col < sk_valid)
        for h in range(MLA_HEADS):
            s = _dot_nt(q_ref[0, h], k_ref[0, h])
            if masked:
                s = jnp.where(mask, s, NEG_BIG)
            m_prev = m_sc[h]
            m_new = jnp.maximum(m_prev, jnp.max(s, axis=-1, keepdims=True))
            p = jnp.exp(s - m_new)
            if masked:
                p = jnp.where(mask, p, 0.0)
            alpha = jnp.exp(m_prev - m_new)
            l_sc[h] = alpha * l_sc[h] + jnp.sum(p, axis=-1, keepdims=True)
            acc_sc[h] = alpha * acc_sc[h] + _dot(p, v_ref[0, h])
            m_sc[h] = m_new

    @pl.when(jnp.logical_and(needed, full))
    def _():
        body(False)

    @pl.when(jnp.logical_and(needed, jnp.logical_not(full)))
    def _():
        body(True)

    @pl.when(kj == _last_kv_block(qi, tq, tk, q_off, nk))
    def _():
        o_ref[0] = jnp.concatenate([acc_sc[h] / l_sc[h] for h in range(MLA_HEADS)], axis=1).astype(o_ref.dtype)


def _rot_half(w):
    half = w.shape[-1] // 2
    return jnp.concatenate([-w[..., half:], w[..., :half]], axis=-1)


def _rope_tables(pos0, S):
    half = QK_ROPE // 2
    inv_freq = ROPE_THETA ** (-jnp.arange(half, dtype=F32) / half)
    ang = (pos0 + jnp.arange(S, dtype=jnp.int32)).astype(F32)[:, None] * inv_freq[None, :]
    cos = jnp.concatenate([jnp.cos(ang), jnp.cos(ang)], axis=-1)[None]
    sin = jnp.concatenate([jnp.sin(ang), jnp.sin(ang)], axis=-1)[None]
    return cos, sin, jnp.tile(cos, (1, 1, MLA_HEADS)), jnp.tile(sin, (1, 1, MLA_HEADS))


def _mla_branch(x, pos0, past_lat, past_kr, p):
    b, S, _ = x.shape
    ts = min(512, S)
    cos_k, sin_k, cos_q, sin_q = _rope_tables(pos0, S)
    q, lat, kr = pl.pallas_call(
        _mla_prep_kernel,
        grid=(b, S // ts),
        in_specs=[
            pl.BlockSpec((1, ts, D_MODEL), lambda bi, si: (bi, si, 0)),
            _full((D_MODEL, MLA_PROJ_COLS)), _full((1, Q_LORA)), _full((1, KV_LORA)),
            _full((Q_LORA, MLA_HEADS * (QK_NOPE + 2 * QK_ROPE))),
            pl.BlockSpec((1, ts, MLA_HEADS * QK_ROPE), lambda bi, si: (0, si, 0)),
            pl.BlockSpec((1, ts, MLA_HEADS * QK_ROPE), lambda bi, si: (0, si, 0)),
            pl.BlockSpec((1, ts, QK_ROPE), lambda bi, si: (0, si, 0)),
            pl.BlockSpec((1, ts, QK_ROPE), lambda bi, si: (0, si, 0)),
        ],
        out_specs=[
            pl.BlockSpec((1, MLA_HEADS, ts, QK_DIM), lambda bi, si: (bi, 0, si, 0)),
            pl.BlockSpec((1, ts, KV_LORA), lambda bi, si: (bi, si, 0)),
            pl.BlockSpec((1, ts, QK_ROPE), lambda bi, si: (bi, si, 0)),
        ],
        out_shape=[
            jax.ShapeDtypeStruct((b, MLA_HEADS, S, QK_DIM), BF16),
            jax.ShapeDtypeStruct((b, S, KV_LORA), F32),
            jax.ShapeDtypeStruct((b, S, QK_ROPE), F32),
        ],
        compiler_params=_cparams("parallel", "parallel"),
        name="mla_prep",
    )(x, p["w_mla"], p["q_norm"], p["kv_norm"], p["w_q"], cos_q, sin_q, cos_k, sin_k)

    if past_lat is None:
        lat_all, kr_all, q_off, sk_valid = lat, kr, 0, S
        tk = min(512, S)
    else:
        past_len = past_lat.shape[1]
        sk_valid = past_len + S
        tk = -(-sk_valid // LANES) * LANES
        pad = tk - sk_valid
        lat_all = jnp.concatenate([past_lat, lat, jnp.zeros((b, pad, KV_LORA), F32)], axis=1)
        kr_all = jnp.concatenate([past_kr, kr, jnp.zeros((b, pad, QK_ROPE), F32)], axis=1)
        q_off = past_len
    Sk = lat_all.shape[1]
    tku = min(512, Sk) if Sk % min(512, Sk) == 0 else Sk
    k, v = pl.pallas_call(
        _kv_up_kernel,
        grid=(b, Sk // tku),
        in_specs=[
            pl.BlockSpec((1, tku, KV_LORA), lambda bi, si: (bi, si, 0)),
            pl.BlockSpec((1, tku, QK_ROPE), lambda bi, si: (bi, si, 0)),
            _full((KV_LORA, MLA_HEADS * (QK_NOPE + V_DIM))),
        ],
        out_specs=[
            pl.BlockSpec((1, MLA_HEADS, tku, QK_DIM), lambda bi, si: (bi, 0, si, 0)),
            pl.BlockSpec((1, MLA_HEADS, tku, V_DIM), lambda bi, si: (bi, 0, si, 0)),
        ],
        out_shape=[
            jax.ShapeDtypeStruct((b, MLA_HEADS, Sk, QK_DIM), BF16),
            jax.ShapeDtypeStruct((b, MLA_HEADS, Sk, V_DIM), BF16),
        ],
        compiler_params=_cparams("parallel", "parallel"),
        name="mla_kv_up",
    )(lat_all, kr_all, p["w_ukv"])

    tq = min(512, S)
    nq, nk = S // tq, Sk // tk
    kv_map = lambda bi, qi, kj: (bi, 0, jnp.minimum(kj, _last_kv_block(qi, tq, tk, q_off, nk)), 0)
    y = pl.pallas_call(
        functools.partial(_flash_kernel, tq=tq, tk=tk, q_off=q_off, sk_valid=sk_valid, nk=nk),
        grid=(b, nq, nk),
        in_specs=[
            pl.BlockSpec((1, MLA_HEADS, tq, QK_DIM), lambda bi, qi, kj: (bi, 0, qi, 0)),
            pl.BlockSpec((1, MLA_HEADS, tk, QK_DIM), kv_map),
            pl.BlockSpec((1, MLA_HEADS, tk, V_DIM), kv_map),
        ],
        out_specs=pl.BlockSpec((1, tq, MLA_HEADS * V_DIM), lambda bi, qi, kj: (bi, qi, 0)),
        out_shape=jax.ShapeDtypeStruct((b, S, MLA_HEADS * V_DIM), BF16),
        scratch_shapes=[
            pltpu.VMEM((MLA_HEADS, tq, 1), F32),
            pltpu.VMEM((MLA_HEADS, tq, 1), F32),
            pltpu.VMEM((MLA_HEADS, tq, V_DIM), F32),
        ],
        compiler_params=_cparams("parallel", "parallel", "arbitrary"),
        name="mla_flash",
    )(q, k, v)
    return y, lat, kr


def _rwkv_prep_kernel(x_ref, sp_ref, w_ref, mu_ref, w0_ref, a0_ref, kk_ref, ka_ref, rk_ref, ww2_ref, wa2_ref, wg2_ref,
                      hsum_ref, r_out, lw_out, k_out, v_out, kk_out, a_out, g_out, bon_out, sh_out, prev_sc):
    si = pl.program_id(1)

    @pl.when(si == 0)
    def _():
        prev_sc[...] = sp_ref[0]

    u = _dot(x_ref[0], w_ref[...])
    ts = u.shape[0]
    row = lax.broadcasted_iota(jnp.int32, u.shape, 0)
    shifted = jnp.where(row == 0, prev_sc[...], pltpu.roll(u, 1, axis=0))
    prev_sc[...] = u[ts - 1:ts, :]
    sh_out[0] = u[ts - 1:ts, :]
    m = u + (shifted - u) * mu_ref[...]
    r = m[:, :RWKV_DIM]
    k = m[:, RWKV_DIM:2 * RWKV_DIM]
    v = m[:, 2 * RWKV_DIM:3 * RWKV_DIM]
    o = 3 * RWKV_DIM
    wl = m[:, o:o + W_LORA]
    al = m[:, o + W_LORA:o + W_LORA + A_LORA]
    gl = m[:, o + W_LORA + A_LORA:]
    d = w0_ref[...] + _dot(jnp.tanh(wl), ww2_ref[...])
    lw_out[0] = -jnp.exp(-_softplus(-d) - 0.5)
    a = _sigmoid(a0_ref[...] + _dot(al, wa2_ref[...]))
    g_out[0] = _dot(_sigmoid(gl), wg2_ref[...])
    hsum = hsum_ref[...]
    kk = k * kk_ref[...]
    kk2 = kk * kk
    kk2_hi = kk2.astype(BF16)
    kk2_lo = (kk2 - kk2_hi.astype(F32)).astype(BF16)
    nrm = jnp.dot(kk2_hi, hsum, preferred_element_type=F32) + jnp.dot(kk2_lo, hsum, preferred_element_type=F32)
    kk_out[0] = kk * lax.rsqrt(nrm + 1e-12)
    kh = k * (1.0 + (a - 1.0) * ka_ref[...])
    rkr = r * kh * rk_ref[...]
    rkr_hi = rkr.astype(BF16)
    rkr_lo = (rkr - rkr_hi.astype(F32)).astype(BF16)
    bsum = jnp.dot(rkr_hi, hsum, preferred_element_type=F32) + jnp.dot(rkr_lo, hsum, preferred_element_type=F32)
    bon_out[0] = bsum * v
    r_out[0] = r
    k_out[0] = kh
    v_out[0] = v
    a_out[0] = a


def _rwkv_pair_chunk(r, lw, k, v, kk, a, z, c):
    L = r.shape[0]
    g = _dot_exact_lhs(c["tri"], lw)
    gp = g - lw
    gl = g[L - 1:L, :]
    e_neg = jnp.exp(-g)
    at = -kk * jnp.exp(gp)
    rt = r * jnp.exp(g)
    beta = kk * a
    e_end = jnp.exp(gl - g)
    lo = c["lane_lo"]

    def stack(xv):
        return jnp.concatenate([jnp.where(lo, xv, 0.0), jnp.where(lo, 0.0, xv)], axis=0)

    bt = beta * e_neg
    kt = k * e_neg
    at_s = stack(at)
    rt_s = stack(rt)
    mm = _dot_nt(jnp.concatenate([at_s, rt_s], axis=0), jnp.concatenate([bt, bt, kt, kt], axis=0))
    L2 = 2 * L
    n = jnp.where(c["strict"], mm[:L2, :L2], 0.0)
    mak = jnp.where(c["strict"], mm[:L2, L2:], 0.0)
    mrb = jnp.where(c["incl"], mm[L2:, :L2], 0.0)
    mrk = jnp.where(c["incl"], mm[L2:, L2:], 0.0)
    xinv = c["eye2"] + n
    pw = n
    for _ in range(int(math.log2(L)) - 1):
        pw = _dot(pw, pw)
        xinv = xinv + _dot(xinv, pw)
    vs = stack(v)
    au = _dot(xinv, jnp.concatenate([at_s, _dot(mak, vs)], axis=1))
    ry = _dot(mrb, au)
    rh = rt_s + ry[:, :LANES]
    yh = ry[:, LANES:] + _dot(mrk, vs)
    bs = stack(beta * e_end)
    ks = stack(k * e_end)
    pt_lr = _dot_tn(bs, au[:, :LANES])
    qt = _dot_tn(jnp.concatenate([bs, ks], axis=0), jnp.concatenate([au[:, LANES:], vs], axis=0))
    ys = _dot(rh, z) + yh
    y = ys[:L] + ys[L:]
    gcol = jnp.sum(jnp.where(c["diag"], jnp.broadcast_to(jnp.exp(gl), (LANES, LANES)), 0.0), axis=1, keepdims=True)
    z_new = gcol * z + _dot(pt_lr, z) + qt
    return y, z_new


def _rwkv_scan_kernel(r_ref, lw_ref, k_ref, v_ref, kk_ref, a_ref, g_ref, bon_ref, gng_ref, gnb_ref, z0_ref,
                      y_ref, zf_ref, z_sc, *, L, n_chunks):
    si = pl.program_id(1)

    @pl.when(si == 0)
    def _():
        z_sc[...] = z0_ref[0]

    L2 = 2 * L
    ri = lax.broadcasted_iota(jnp.int32, (L2, L2), 0)
    ci = lax.broadcasted_iota(jnp.int32, (L2, L2), 1)
    same = (ri // L) == (ci // L)
    rl = lax.broadcasted_iota(jnp.int32, (L, L), 0)
    cl = lax.broadcasted_iota(jnp.int32, (L, L), 1)
    r128 = lax.broadcasted_iota(jnp.int32, (LANES, LANES), 0)
    c128 = lax.broadcasted_iota(jnp.int32, (LANES, LANES), 1)
    consts = dict(
        tri=(cl <= rl).astype(BF16),
        strict=jnp.logical_and(same, (ci % L) < (ri % L)),
        incl=jnp.logical_and(same, (ci % L) <= (ri % L)),
        eye2=(ri == ci).astype(F32),
        lane_lo=lax.broadcasted_iota(jnp.int32, (L, LANES), 1) < RWKV_HEAD,
        diag=r128 == c128,
    )
    gmean = jnp.where((r128 // RWKV_HEAD) == (c128 // RWKV_HEAD), 1.0 / RWKV_HEAD, 0.0).astype(BF16)

    def head_mean(xv):
        hi = xv.astype(BF16)
        lo = (xv - hi.astype(F32)).astype(BF16)
        return jnp.dot(hi, gmean, preferred_element_type=F32) + jnp.dot(lo, gmean, preferred_element_type=F32)

    def chunk(ci_, carry):
        rows = pl.ds(pl.multiple_of(ci_ * L, L), L)
        for p in range(RWKV_HEADS // 2):
            ln = slice(p * LANES, (p + 1) * LANES)
            y, z_new = _rwkv_pair_chunk(r_ref[0, rows, ln], lw_ref[0, rows, ln], k_ref[0, rows, ln], v_ref[0, rows, ln],
                                        kk_ref[0, rows, ln], a_ref[0, rows, ln], z_sc[p], consts)
            z_sc[p] = z_new
            mean = head_mean(y)
            yc = y - mean
            var = head_mean(yc * yc)
            o = yc * lax.rsqrt(var + RWKV_GN_EPS) * gng_ref[:, ln] + gnb_ref[:, ln]
            y_ref[0, rows, ln] = ((o + bon_ref[0, rows, ln]) * g_ref[0, rows, ln]).astype(y_ref.dtype)
        return carry

    lax.fori_loop(0, n_chunks, chunk, 0)

    @pl.when(si == pl.num_programs(1) - 1)
    def _():
        zf_ref[0] = z_sc[...]


def _rwkv_branch(x, s0, shift_prev, p):
    b, S, _ = x.shape
    ts = min(512, S)
    tok = lambda w: pl.BlockSpec((1, ts, w), lambda bi, si: (bi, si, 0))
    row = lambda w: pl.BlockSpec((1, 1, w), lambda bi, si: (bi, 0, 0))
    outs = pl.pallas_call(
        _rwkv_prep_kernel,
        grid=(b, S // ts),
        in_specs=[tok(D_MODEL), row(RWKV_COLS), _full((D_MODEL, RWKV_COLS)), _full((1, RWKV_COLS))]
        + [_full((1, RWKV_DIM))] * 5
        + [_full((W_LORA, RWKV_DIM)), _full((A_LORA, RWKV_DIM)), _full((G_LORA, RWKV_DIM)), _full((RWKV_DIM, RWKV_DIM))],
        out_specs=[tok(RWKV_DIM)] * 8 + [row(RWKV_COLS)],
        out_shape=[jax.ShapeDtypeStruct((b, S, RWKV_DIM), F32)] * 8 + [jax.ShapeDtypeStruct((b, 1, RWKV_COLS), F32)],
        scratch_shapes=[pltpu.VMEM((1, RWKV_COLS), F32)],
        compiler_params=_cparams("parallel", "arbitrary"),
        name="rwkv_prep",
    )(x, shift_prev, p["w_rwkv"], p["mu"], p["w0"], p["a0"], p["k_k"], p["k_a"], p["r_k"], p["w_w2"], p["w_a2"], p["w_g2"],
      p["head_sum"])
    r, lw, kh, v, kk, a, g, bonus, shift_new = outs

    L = min(CHUNK, S)
    tb = min(4 * L, S)
    npair = RWKV_HEADS // 2
    zt = jnp.swapaxes(s0.astype(F32), 2, 3).reshape(b, npair, 2, RWKV_HEAD, RWKV_HEAD)
    z0 = jnp.einsum("bpikv,ij->bpikjv", zt, jnp.eye(2, dtype=F32)).reshape(b, npair, LANES, LANES)
    tokb = lambda: pl.BlockSpec((1, tb, RWKV_DIM), lambda bi, si: (bi, si, 0))
    zspec = pl.BlockSpec((1, npair, LANES, LANES), lambda bi, si: (bi, 0, 0, 0))
    y, zf = pl.pallas_call(
        functools.partial(_rwkv_scan_kernel, L=L, n_chunks=tb // L),
        grid=(b, S // tb),
        in_specs=[tokb() for _ in range(8)] + [_full((1, RWKV_DIM)), _full((1, RWKV_DIM)), zspec],
        out_specs=[tokb(), zspec],
        out_shape=[jax.ShapeDtypeStruct((b, S, RWKV_DIM), BF16), jax.ShapeDtypeStruct((b, npair, LANES, LANES), F32)],
        scratch_shapes=[pltpu.VMEM((npair, LANES, LANES), F32)],
        compiler_params=_cparams("parallel", "arbitrary"),
        name="rwkv_scan",
    )(r, lw, kh, v, kk, a, g, bonus, p["gn_g"], p["gn_b"], z0)
    zd = jnp.einsum("bpikiv->bpikv", zf.reshape(b, npair, 2, RWKV_HEAD, 2, RWKV_HEAD))
    s_new = jnp.swapaxes(zd.reshape(b, RWKV_HEADS, RWKV_HEAD, RWKV_HEAD), 2, 3)
    return y, s_new, shift_new


def _ssd_kernel(x_ref, w_ref, s0_ref, cp_ref, cw_ref, cb_ref, dtb_ref, a_ref, dsk_ref, ng_ref,
                y_ref, sf_ref, ct_ref, st_sc, tail_sc, *, L):
    si = pl.program_id(1)

    @pl.when(si == 0)
    def _():
        st_sc[...] = s0_ref[0]
        tail_sc[...] = cp_ref[0]

    u = _dot(x_ref[0], w_ref[...])
    z = u[:, :SSM_DIM]
    xbc = u[:, SSM_DIM:SSM_DIM + CONV_DIM]
    dtr = u[:, SSM_DIM + CONV_DIM:]
    tail = tail_sc[...]
    row = lax.broadcasted_iota(jnp.int32, xbc.shape, 0)
    sh1 = jnp.where(row == 0, tail[7:8], pltpu.roll(xbc, 1, axis=0))
    sh2 = jnp.where(row == 0, tail[6:7], pltpu.roll(sh1, 1, axis=0))
    sh3 = jnp.where(row == 0, tail[5:6], pltpu.roll(sh2, 1, axis=0))
    tail_sc[...] = xbc[L - 8:L]
    ct_ref[0] = xbc[L - 8:L]
    conv = cb_ref[...] + cw_ref[3:4] * xbc + cw_ref[2:3] * sh1 + cw_ref[1:2] * sh2 + cw_ref[0:1] * sh3
    act = conv * _sigmoid(conv)
    xs = act[:, :SSM_DIM]
    gw = SSM_STATE
    dt = _softplus(dtr + dtb_ref[...])
    a = dt * a_ref[...]
    rl = lax.broadcasted_iota(jnp.int32, (L, L), 0)
    cl = lax.broadcasted_iota(jnp.int32, (L, L), 1)
    causal = cl <= rl
    cum = _dot_exact_lhs(causal.astype(BF16), a)
    sel = (lax.broadcasted_iota(jnp.int32, (16, LANES), 0) == lax.broadcasted_iota(jnp.int32, (16, LANES), 1)).astype(BF16)
    cum_t = _dot_nt_exact_lhs(sel, cum)
    dt_t = _dot_nt_exact_lhs(sel, dt)
    ys = []
    hpg = SSM_HEADS // SSM_GROUPS
    for gi in range(SSM_GROUPS):
        bg = act[:, SSM_DIM + gi * gw:SSM_DIM + (gi + 1) * gw]
        cg = act[:, SSM_DIM + SSM_GROUPS * gw + gi * gw:SSM_DIM + SSM_GROUPS * gw + (gi + 1) * gw]
        cb = _dot_nt(cg, bg)
        for h in range(gi * hpg, (gi + 1) * hpg):
            xh = xs[:, h * SSM_HEADDIM:(h + 1) * SSM_HEADDIM]
            cc = cum[:, h:h + 1]
            seg = cc - cum_t[h:h + 1, :]
            dec = jnp.where(causal, jnp.exp(jnp.minimum(seg, 0.0)), 0.0)
            sc = cb * dec * dt_t[h:h + 1, :]
            st = st_sc[h]
            yh = _dot(sc, xh) + _dot_nt(cg, st) * jnp.exp(cc)
            clast = cum[L - 1:L, h:h + 1]
            wcol = jnp.exp(clast - cc) * dt[:, h:h + 1]
            st_sc[h] = st * jnp.exp(clast) + _dot_tn(xh * wcol, bg)
            ys.append(yh)
    y = jnp.concatenate(ys, axis=1) + dsk_ref[...] * xs
    y = y * (z * _sigmoid(z))
    gdim = SSM_DIM // SSM_GROUPS
    outs = []
    for gi in range(SSM_GROUPS):
        yg = y[:, gi * gdim:(gi + 1) * gdim]
        outs.append(yg * lax.rsqrt(jnp.mean(yg * yg, axis=-1, keepdims=True) + RMS_EPS))
    y_ref[0] = (jnp.concatenate(outs, axis=1) * ng_ref[...]).astype(y_ref.dtype)

    @pl.when(si == pl.num_programs(1) - 1)
    def _():
        sf_ref[0] = st_sc[...]


def _ssm_branch(x, s0, conv_prev, p):
    b, S, _ = x.shape
    L = min(256, S)
    cp = jnp.concatenate([jnp.zeros((b, 8 - (CONV_W - 1), CONV_DIM), F32), conv_prev.astype(F32)], axis=1)
    sspec = pl.BlockSpec((1, SSM_HEADS, SSM_HEADDIM, SSM_STATE), lambda bi, si: (bi, 0, 0, 0))
    cspec = pl.BlockSpec((1, 8, CONV_DIM), lambda bi, si: (bi, 0, 0))
    y, s_new, ctail = pl.pallas_call(
        functools.partial(_ssd_kernel, L=L),
        grid=(b, S // L),
        in_specs=[
            pl.BlockSpec((1, L, D_MODEL), lambda bi, si: (bi, si, 0)),
            _full((D_MODEL, SSM_PROJ_COLS)), sspec, cspec,
            _full((CONV_W, CONV_DIM)), _full((1, CONV_DIM)), _full((1, LANES)), _full((1, LANES)),
            _full((1, SSM_DIM)), _full((1, SSM_DIM)),
        ],
        out_specs=[pl.BlockSpec((1, L, SSM_DIM), lambda bi, si: (bi, si, 0)), sspec, cspec],
        out_shape=[
            jax.ShapeDtypeStruct((b, S, SSM_DIM), BF16),
            jax.ShapeDtypeStruct((b, SSM_HEADS, SSM_HEADDIM, SSM_STATE), F32),
            jax.ShapeDtypeStruct((b, 8, CONV_DIM), F32),
        ],
        scratch_shapes=[pltpu.VMEM((SSM_HEADS, SSM_HEADDIM, SSM_STATE), F32), pltpu.VMEM((8, CONV_DIM), F32)],
        compiler_params=_cparams("parallel", "arbitrary"),
        name="ssd",
    )(x, p["w_ssm"], s0.astype(F32), cp, p["conv_w"], p["conv_b"], p["dt_bias"], p["a_neg"], p["d_skip"], p["ssm_norm_g"])
    return y, s_new, ctail[:, 8 - (CONV_W - 1):]


def _merge_kernel(x_ref, ym_ref, yr_ref, ys_ref, wg_ref, wb_ref, wo_ref, g_ref, b_ref, rwh_ref, rwl_ref, rb_ref,
                  h_ref, ti_ref, tg_ref):
    x = x_ref[...]
    gates = _sigmoid(_dot(x, wg_ref[...]))
    mix = gates[:, :D_MODEL] * _dot(ym_ref[...], wb_ref[0])
    mix = mix + gates[:, D_MODEL:2 * D_MODEL] * _dot(yr_ref[...], wb_ref[1])
    mix = mix + gates[:, 2 * D_MODEL:] * _dot(ys_ref[...], wb_ref[2])
    h = _layer_norm(DEEPNORM_ALPHA * x + _dot(mix, wo_ref[...]), g_ref[...], b_ref[...])
    h_ref[...] = h
    h_hi = h.astype(BF16)
    h_lo = (h - h_hi.astype(F32)).astype(BF16)
    logits = (jnp.dot(h_hi, rwh_ref[...], preferred_element_type=F32) + jnp.dot(h_lo, rwh_ref[...], preferred_element_type=F32)
              + jnp.dot(h_hi, rwl_ref[...], preferred_element_type=F32) + rb_ref[...])
    lane = lax.broadcasted_iota(jnp.int32, logits.shape, 1)
    idx_out = jnp.zeros(logits.shape, jnp.int32)
    val_out = jnp.zeros(logits.shape, F32)
    top = None
    den = None
    for kth in range(TOP_K):
        mval = jnp.max(logits, axis=-1, keepdims=True)
        midx = jnp.min(jnp.where(logits == mval, lane, LANES), axis=-1, keepdims=True)
        if kth == 0:
            top = mval
            e = jnp.ones_like(mval)
            den = e
        else:
            e = jnp.exp(mval - top)
            den = den + e
        idx_out = jnp.where(lane == kth, midx, idx_out)
        val_out = jnp.where(lane == kth, e, val_out)
        logits = jnp.where(lane == midx, NEG_BIG * 2.0, logits)
    ti_ref[...] = idx_out
    tg_ref[...] = val_out / den


def _merge(x2, ym, yr, ys, p):
    T = x2.shape[0]
    tm = min(512, T)
    tok = lambda w: pl.BlockSpec((tm, w), lambda i: (i, 0))
    return pl.pallas_call(
        _merge_kernel,
        grid=(T // tm,),
        in_specs=[tok(D_MODEL), tok(BRANCH_DIM), tok(BRANCH_DIM), tok(BRANCH_DIM),
                  _full((D_MODEL, N_BRANCH * D_MODEL)), _full((N_BRANCH, BRANCH_DIM, D_MODEL)), _full((D_MODEL, D_MODEL)),
                  _full((1, D_MODEL)), _full((1, D_MODEL)), _full((D_MODEL, LANES)), _full((D_MODEL, LANES)), _full((1, LANES))],
        out_specs=[tok(D_MODEL), tok(LANES), tok(LANES)],
        out_shape=[jax.ShapeDtypeStruct((T, D_MODEL), F32), jax.ShapeDtypeStruct((T, LANES), jnp.int32),
                   jax.ShapeDtypeStruct((T, LANES), F32)],
        compiler_params=_cparams("parallel"),
        name="merge_router",
    )(x2, ym, yr, ys, p["w_gate"], p["w_branch"], p["w_out"], p["ln1_g"], p["ln1_b"], p["router_hi"], p["router_lo"], p["router_b"])


def _expert_kernel(be_ref, nu_ref, x_ref, wgu_ref, bgu_ref, wd_ref, bd_ref, o_ref):
    i = pl.program_id(0)

    @pl.when(i < nu_ref[0])
    def _():
        hgu = _dot(x_ref[...], wgu_ref[0]) + bgu_ref[0]
        gate = jnp.minimum(hgu[:, :D_FF], SWIGLU_LIMIT)
        up = jnp.clip(hgu[:, D_FF:], -SWIGLU_LIMIT, SWIGLU_LIMIT)
        hid = gate * _sigmoid(SWIGLU_ALPHA * gate) * (up + 1.0)
        o_ref[...] = _dot(hid, wd_ref[0]) + bd_ref[0]

    @pl.when(i >= nu_ref[0])
    def _():
        o_ref[...] = jnp.zeros(o_ref.shape, o_ref.dtype)


def _combine_kernel(yg_ref, tg_ref, h_ref, g_ref, b_ref, o_ref):
    tg = tg_ref[...]
    f = tg[:, 0:1] * yg_ref[:, :D_MODEL]
    for kth in range(1, TOP_K):
        f = f + tg[:, kth:kth + 1] * yg_ref[:, kth * D_MODEL:(kth + 1) * D_MODEL]
    o_ref[...] = _layer_norm(DEEPNORM_ALPHA * h_ref[...] + f, g_ref[...], b_ref[...])


def _moe(h, top_i, top_g, p):
    T = h.shape[0]
    n_assign = T * TOP_K
    flat_e = top_i[:, :TOP_K].reshape(-1)
    order = jnp.argsort(flat_e)
    sorted_e = flat_e[order]
    sorted_tok = (order // TOP_K).astype(jnp.int32)
    counts = jnp.zeros((N_EXPERTS,), jnp.int32).at[flat_e].add(1)
    padded = (counts + MOE_ROWS - 1) // MOE_ROWS * MOE_ROWS
    pad_end = jnp.cumsum(padded)
    pad_start = pad_end - padded
    start = jnp.cumsum(counts) - counts
    dest = pad_start[sorted_e] + (jnp.arange(n_assign, dtype=jnp.int32) - start[sorted_e])
    n_blocks = -(-(n_assign + N_EXPERTS * (MOE_ROWS - 1)) // MOE_ROWS)
    n_rows = n_blocks * MOE_ROWS
    row_tok = jnp.full((n_rows,), T, jnp.int32).at[dest].set(sorted_tok)
    block_e = jnp.minimum(jnp.searchsorted(pad_end, jnp.arange(n_blocks, dtype=jnp.int32) * MOE_ROWS, side="right"),
                          N_EXPERTS - 1).astype(jnp.int32)
    n_used = (pad_end[-1:] // MOE_ROWS).astype(jnp.int32)
    x_pad = jnp.concatenate([h.astype(BF16), jnp.zeros((1, D_MODEL), BF16)], axis=0)
    xb = x_pad[row_tok]

    yb = pl.pallas_call(
        _expert_kernel,
        grid_spec=pltpu.PrefetchScalarGridSpec(
            num_scalar_prefetch=2,
            grid=(n_blocks,),
            in_specs=[
                pl.BlockSpec((MOE_ROWS, D_MODEL), lambda i, be, nu: (i, 0)),
                pl.BlockSpec((1, D_MODEL, 2 * D_FF), lambda i, be, nu: (be[i], 0, 0)),
                pl.BlockSpec((1, 1, 2 * D_FF), lambda i, be, nu: (be[i], 0, 0)),
                pl.BlockSpec((1, D_FF, D_MODEL), lambda i, be, nu: (be[i], 0, 0)),
                pl.BlockSpec((1, 1, D_MODEL), lambda i, be, nu: (be[i], 0, 0)),
            ],
            out_specs=pl.BlockSpec((MOE_ROWS, D_MODEL), lambda i, be, nu: (i, 0)),
        ),
        out_shape=jax.ShapeDtypeStruct((n_rows, D_MODEL), F32),
        compiler_params=_cparams("arbitrary"),
        name="moe_experts",
    )(block_e, n_used, xb, p["w_gu"], p["b_gu"], p["w_down"], p["b_down"])

    inv = jnp.zeros((n_assign,), jnp.int32).at[order].set(dest)
    yg = yb[inv].reshape(T, TOP_K * D_MODEL)
    tm = min(256, T)
    tok = lambda w: pl.BlockSpec((tm, w), lambda i: (i, 0))
    return pl.pallas_call(
        _combine_kernel,
        grid=(T // tm,),
        in_specs=[tok(TOP_K * D_MODEL), tok(LANES), tok(D_MODEL), _full((1, D_MODEL)), _full((1, D_MODEL))],
        out_specs=tok(D_MODEL),
        out_shape=jax.ShapeDtypeStruct((T, D_MODEL), F32),
        compiler_params=_cparams("parallel"),
        name="moe_combine_ln",
    )(yg, top_g, h, p["ln2_g"], p["ln2_b"])


def _prep_layer(l, w):
    w_in = w["w_in"][l]
    kr_cols = w_in[:, MLA_COLS - QK_ROPE:MLA_COLS]
    w_mla = jnp.concatenate([w_in[:, :MLA_COLS], _rot_half(kr_cols),
                             jnp.zeros((D_MODEL, MLA_PROJ_COLS - MLA_COLS - QK_ROPE), F32)], axis=1)
    wq = w["mla_w_uq"][l].reshape(Q_LORA, MLA_HEADS, QK_DIM)
    wq_rope = wq[:, :, QK_NOPE:]
    w_q = jnp.concatenate([wq[:, :, :QK_NOPE].reshape(Q_LORA, -1), wq_rope.reshape(Q_LORA, -1),
                           _rot_half(wq_rope).reshape(Q_LORA, -1)], axis=1)
    o_r = MLA_COLS
    o_s = o_r + RWKV_COLS
    o_g = o_s + SSM_COLS
    w_ssm = jnp.concatenate([w_in[:, o_s:o_s + SSM_DIM + CONV_DIM], w_in[:, o_s + SSM_DIM + CONV_DIM:o_g],
                             jnp.zeros((D_MODEL, LANES - SSM_HEADS), F32)], axis=1)
    pad8 = lambda v: jnp.concatenate([v.astype(F32), jnp.zeros((LANES - SSM_HEADS,), F32)])[None]
    row = lambda v: v.astype(F32)[None]
    hid = jnp.arange(RWKV_DIM) // RWKV_HEAD
    rw = jnp.concatenate([w["router_w"][l], jnp.zeros((D_MODEL, LANES - N_EXPERTS), F32)], axis=1)
    rw_hi = rw.astype(BF16)
    return dict(
        w_mla=w_mla.astype(BF16), q_norm=row(w["mla_q_norm"][l]), kv_norm=row(w["mla_kv_norm"][l]), w_q=w_q.astype(BF16),
        w_ukv=w["mla_w_ukv"][l].astype(BF16),
        w_rwkv=w_in[:, o_r:o_s].astype(BF16), mu=row(w["rwkv_mu"][l]), w0=row(w["rwkv_w0"][l]), a0=row(w["rwkv_a0"][l]),
        k_k=row(w["rwkv_k_k"][l]), k_a=row(w["rwkv_k_a"][l]), r_k=row(w["rwkv_r_k"][l]),
        w_w2=w["rwkv_w_w2"][l].astype(BF16), w_a2=w["rwkv_w_a2"][l].astype(BF16), w_g2=w["rwkv_w_g2"][l].astype(BF16),
        head_sum=(hid[:, None] == hid[None, :]).astype(BF16),
        gn_g=row(w["rwkv_gn_g"][l]), gn_b=row(w["rwkv_gn_b"][l]),
        w_ssm=w_ssm.astype(BF16), conv_w=w["ssm_conv_w"][l].astype(F32), conv_b=row(w["ssm_conv_b"][l]),
        dt_bias=pad8(w["ssm_dt_bias"][l]), a_neg=pad8(-jnp.exp(w["ssm_a_log"][l].astype(F32))),
        d_skip=row(jnp.repeat(w["ssm_d"][l], SSM_HEADDIM)), ssm_norm_g=row(w["ssm_norm_g"][l]),
        w_gate=w_in[:, o_g:].astype(BF16), w_branch=w["w_branch"][l].astype(BF16), w_out=w["w_out"][l].astype(BF16),
        ln1_g=row(w["ln1_g"][l]), ln1_b=row(w["ln1_b"][l]),
        router_hi=rw_hi, router_lo=(rw - rw_hi.astype(F32)).astype(BF16),
        router_b=jnp.concatenate([w["router_b"][l].astype(F32), jnp.full((LANES - N_EXPERTS,), NEG_BIG, F32)])[None],
        w_gu=w["expert_w_gu"][l].astype(BF16), b_gu=w["expert_b_gu"][l].astype(F32)[:, None, :],
        w_down=w["expert_w_down"][l].astype(BF16), b_down=w["expert_b_down"][l].astype(F32)[:, None, :],
        ln2_g=row(w["ln2_g"][l]), ln2_b=row(w["ln2_b"][l]),
    )


def _trunk_layer(x, pos0, past_lat, past_kr, rwkv_s0, shift_prev, ssm_s0, conv_prev, p):
    b, S, _ = x.shape
    y_mla, lat, kr = _mla_branch(x, pos0, past_lat, past_kr, p)
    y_rwkv, rwkv_s, shift_new = _rwkv_branch(x, rwkv_s0, shift_prev, p)
    y_ssm, ssm_s, conv_new = _ssm_branch(x, ssm_s0, conv_prev, p)
    T = b * S
    flat = lambda t: t.reshape(T, t.shape[-1])
    h, top_i, top_g = _merge(flat(x), flat(y_mla), flat(y_rwkv), flat(y_ssm), p)
    out = _moe(h, top_i, top_g, p)
    return out.reshape(b, S, D_MODEL), (lat, kr, rwkv_s, shift_new, ssm_s, conv_new)


def kernel(x_prompt, x_sample, cache_mla_latent, cache_mla_krope, state_rwkv, state_rwkv_shift, state_ssm, state_ssm_conv, w_in, mla_q_norm, mla_kv_norm, mla_w_uq, mla_w_ukv, rwkv_mu, rwkv_w0, rwkv_w_w2, rwkv_a0, rwkv_w_a2, rwkv_w_g2, rwkv_k_k, rwkv_k_a, rwkv_r_k, rwkv_gn_g, rwkv_gn_b, ssm_conv_w, ssm_conv_b, ssm_dt_bias, ssm_a_log, ssm_d, ssm_norm_g, w_branch, w_out, ln1_g, ln1_b, router_w, router_b, expert_w_gu, expert_b_gu, expert_w_down, expert_b_down, ln2_g, ln2_b):
    w = dict(w_in=w_in, mla_q_norm=mla_q_norm, mla_kv_norm=mla_kv_norm, mla_w_uq=mla_w_uq, mla_w_ukv=mla_w_ukv,
             rwkv_mu=rwkv_mu, rwkv_w0=rwkv_w0, rwkv_w_w2=rwkv_w_w2, rwkv_a0=rwkv_a0, rwkv_w_a2=rwkv_w_a2, rwkv_w_g2=rwkv_w_g2,
             rwkv_k_k=rwkv_k_k, rwkv_k_a=rwkv_k_a, rwkv_r_k=rwkv_r_k, rwkv_gn_g=rwkv_gn_g, rwkv_gn_b=rwkv_gn_b,
             ssm_conv_w=ssm_conv_w, ssm_conv_b=ssm_conv_b, ssm_dt_bias=ssm_dt_bias, ssm_a_log=ssm_a_log, ssm_d=ssm_d,
             ssm_norm_g=ssm_norm_g, w_branch=w_branch, w_out=w_out, ln1_g=ln1_g, ln1_b=ln1_b, router_w=router_w,
             router_b=router_b, expert_w_gu=expert_w_gu, expert_b_gu=expert_b_gu, expert_w_down=expert_w_down,
             expert_b_down=expert_b_down, ln2_g=ln2_g, ln2_b=ln2_b)
    bp = x_prompt.shape[0]
    past_len = cache_mla_latent.shape[2]
    zero_rwkv = jnp.zeros((bp, RWKV_HEADS, RWKV_HEAD, RWKV_HEAD), F32)
    zero_shift = jnp.zeros((bp, 1, RWKV_COLS), F32)
    zero_ssm = jnp.zeros((bp, SSM_HEADS, SSM_HEADDIM, SSM_STATE), F32)
    zero_conv = jnp.zeros((bp, CONV_W - 1, CONV_DIM), F32)
    yp, ys = x_prompt, x_sample
    st_p = [[] for _ in range(6)]
    st_s = [[] for _ in range(6)]
    for l in range(DEPTH):
        p = _prep_layer(l, w)
        yp, new_p = _trunk_layer(yp, 0, None, None, zero_rwkv, zero_shift, zero_ssm, zero_conv, p)
        ys, new_s = _trunk_layer(ys, past_len, cache_mla_latent[l], cache_mla_krope[l], state_rwkv[l], state_rwkv_shift[l],
                                 state_ssm[l], state_ssm_conv[l], p)
        for i in range(6):
            st_p[i].append(new_p[i])
            st_s[i].append(new_s[i])
    outs_p = [jnp.stack(t, axis=0) for t in st_p]
    outs_s = [jnp.stack(t, axis=0) for t in st_s]
    return (yp, ys, *outs_p, *outs_s)
```

```python
import functools
import math

import jax
import jax.numpy as jnp
from jax import lax
from jax.experimental import pallas as pl
from jax.experimental.pallas import tpu as pltpu

F32 = jnp.float32
BF16 = jnp.bfloat16

D_MODEL = 1024
DEPTH = 4
CHUNK = 64
MLA_HEADS = 8
QK_NOPE = 64
QK_ROPE = 32
QK_DIM = QK_NOPE + QK_ROPE
V_DIM = 64
VT_ROWS = V_DIM + 16
Q_LORA = 384
KV_LORA = 256
ROPE_THETA = 10000.0
RWKV_HEADS = 8
RWKV_HEAD = 64
RWKV_DIM = RWKV_HEADS * RWKV_HEAD
W_LORA = 64
A_LORA = 64
G_LORA = 128
RWKV_GN_EPS = 64e-5
SSM_HEADS = 8
SSM_HEADDIM = 64
SSM_DIM = SSM_HEADS * SSM_HEADDIM
SSM_STATE = 64
SSM_GROUPS = 2
CONV_W = 4
CONV_DIM = SSM_DIM + 2 * SSM_GROUPS * SSM_STATE
N_BRANCH = 3
BRANCH_DIM = 512
N_EXPERTS = 32
TOP_K = 4
D_FF = 1024
SWIGLU_LIMIT = 7.0
SWIGLU_ALPHA = 1.702
DEEPNORM_ALPHA = (2.0 * DEPTH) ** 0.25
LN_EPS = 1e-5
RMS_EPS = 1e-6
MLA_COLS = Q_LORA + KV_LORA + QK_ROPE
RWKV_COLS = 3 * RWKV_DIM + W_LORA + A_LORA + G_LORA
SSM_COLS = SSM_DIM + CONV_DIM + SSM_HEADS

LANES = 128
MLA_PROJ_COLS = 768
SSM_PROJ_COLS = SSM_DIM + CONV_DIM + LANES
MOE_ROWS = 256
NEG_BIG = -1e30
VMEM_LIMIT = 56 * 1024 * 1024


def _cparams(*sem):
    return pltpu.CompilerParams(dimension_semantics=sem, vmem_limit_bytes=VMEM_LIMIT)


def _dot(a, b):
    return jnp.dot(a.astype(BF16), b.astype(BF16), preferred_element_type=F32)


def _dot_nt(a, b):
    return lax.dot_general(a.astype(BF16), b.astype(BF16), (((1,), (1,)), ((), ())), preferred_element_type=F32)


def _dot_tn(a, b):
    return lax.dot_general(a.astype(BF16), b.astype(BF16), (((0,), (0,)), ((), ())), preferred_element_type=F32)


def _dot_tn_split(a, b):
    a_hi = a.astype(BF16)
    a_lo = (a - a_hi.astype(F32)).astype(BF16)
    b_hi = b.astype(BF16)
    b_lo = (b - b_hi.astype(F32)).astype(BF16)
    dn = (((0,), (0,)), ((), ()))
    return (lax.dot_general(a_hi, b_hi, dn, preferred_element_type=F32) + lax.dot_general(a_lo, b_hi, dn, preferred_element_type=F32)
            + lax.dot_general(a_hi, b_lo, dn, preferred_element_type=F32))


def _split3(x):
    hi = x.astype(BF16)
    r1 = x - hi.astype(F32)
    mid = r1.astype(BF16)
    lo = (r1 - mid.astype(F32)).astype(BF16)
    return hi, mid, lo


def _dot_exact_lhs(m, x):
    hi, mid, lo = _split3(x)
    return (jnp.dot(m, hi, preferred_element_type=F32) + jnp.dot(m, mid, preferred_element_type=F32)
            + jnp.dot(m, lo, preferred_element_type=F32))


def _dot_nt_exact_lhs(m, x):
    dn = (((1,), (1,)), ((), ()))
    hi, mid, lo = _split3(x)
    return (lax.dot_general(m, hi, dn, preferred_element_type=F32) + lax.dot_general(m, mid, dn, preferred_element_type=F32)
            + lax.dot_general(m, lo, dn, preferred_element_type=F32))


def _sigmoid(x):
    return 1.0 / (1.0 + jnp.exp(-x))


def _softplus(x):
    return jnp.maximum(x, 0.0) + jnp.log(1.0 + jnp.exp(-jnp.abs(x)))


def _rms(x, g):
    return x * lax.rsqrt(jnp.mean(x * x, axis=-1, keepdims=True) + RMS_EPS) * g


def _layer_norm(x, g, b):
    mu = jnp.mean(x, axis=-1, keepdims=True)
    xc = x - mu
    var = jnp.mean(xc * xc, axis=-1, keepdims=True)
    return xc * lax.rsqrt(var + LN_EPS) * g + b


def _full(shape):
    return pl.BlockSpec(shape, lambda *_: (0,) * len(shape))


def _mla_prep_kernel(x_ref, w_ref, qn_ref, kvn_ref, wq_ref, cq_ref, sq_ref, ck_ref, sk_ref, q_ref, lat_ref, kr_ref):
    u = _dot(x_ref[0], w_ref[...])
    c_q = u[:, :Q_LORA]
    c_kv = u[:, Q_LORA:Q_LORA + KV_LORA]
    kr = u[:, MLA_COLS - QK_ROPE:MLA_COLS]
    kr_rot = u[:, MLA_COLS:MLA_COLS + QK_ROPE]
    qall = _dot(_rms(c_q, qn_ref[...]), wq_ref[...])
    nope_w = MLA_HEADS * QK_NOPE
    rope_w = MLA_HEADS * QK_ROPE
    q_rope = qall[:, nope_w:nope_w + rope_w] * cq_ref[0] + qall[:, nope_w + rope_w:] * sq_ref[0]
    scale = QK_DIM ** -0.5 * math.log2(math.e)
    for h in range(MLA_HEADS):
        qh = jnp.concatenate([qall[:, h * QK_NOPE:(h + 1) * QK_NOPE], q_rope[:, h * QK_ROPE:(h + 1) * QK_ROPE]], axis=1)
        q_ref[0, h] = (qh * scale).astype(BF16)
    lat_ref[0] = _rms(c_kv, kvn_ref[...])
    kr_ref[0] = kr * ck_ref[0] + kr_rot * sk_ref[0]


def _kv_up_kernel(lat_ref, kr_ref, w_ref, k_ref, vt_ref):
    kv = _dot(lat_ref[0], w_ref[...])
    kr = kr_ref[0]
    hw = QK_NOPE + V_DIM
    ones_rows = (lax.broadcasted_iota(jnp.int32, (VT_ROWS - V_DIM, kv.shape[0]), 0) == 0).astype(F32)
    for h in range(MLA_HEADS):
        k_ref[0, h] = jnp.concatenate([kv[:, h * hw:h * hw + QK_NOPE], kr], axis=1).astype(BF16)
        vt_ref[0, h] = jnp.concatenate([kv[:, h * hw + QK_NOPE:(h + 1) * hw].T, ones_rows], axis=0).astype(BF16)


def _last_kv_block(qi, tq, tk, q_off, nk):
    q_hi = q_off + qi * tq + tq - 1
    return jnp.minimum(nk - 1, ((q_hi // CHUNK) * CHUNK + CHUNK - 1) // tk)


def _flash_kernel(q_ref, k_ref, vt_ref, o_ref, m_sc, acc_sc, *, tq, tk, q_off, sk_valid, nk):
    qi = pl.program_id(1)
    kj = pl.program_id(2)

    @pl.when(kj == 0)
    def _():
        m_sc[...] = jnp.full(m_sc.shape, NEG_BIG, F32)
        acc_sc[...] = jnp.zeros(acc_sc.shape, F32)

    q_lo = q_off + qi * tq
    q_hi = q_lo + tq - 1
    k_lo = kj * tk
    k_hi = k_lo + tk - 1
    needed = (k_lo // CHUNK) <= (q_hi // CHUNK)
    full = jnp.logical_and((k_hi // CHUNK) <= (q_lo // CHUNK), k_hi < sk_valid)

    def body(masked):
        if masked:
            key = k_lo + lax.broadcasted_iota(jnp.int32, (tk, tq), 0)
            qry = q_lo + lax.broadcasted_iota(jnp.int32, (tk, tq), 1)
            mask = jnp.logical_and((key // CHUNK) <= (qry // CHUNK), key < sk_valid)
        s_next = _dot_nt(k_ref[0, 0], q_ref[0, 0])
        for h in range(MLA_HEADS):
            s = s_next
            if h + 1 < MLA_HEADS:
                s_next = _dot_nt(k_ref[0, h + 1], q_ref[0, h + 1])
            if masked:
                s = jnp.where(mask, s, NEG_BIG)
            m_prev = m_sc[h]
            m_new = jnp.maximum(m_prev, jnp.max(s, axis=0, keepdims=True))
            p = jnp.exp2((s - m_new).astype(BF16))
            if masked:
                p = jnp.where(mask, p, jnp.zeros_like(p))
            alpha = jnp.exp2(m_prev - m_new)
            acc_sc[h] = alpha * acc_sc[h] + jnp.dot(vt_ref[0, h], p, preferred_element_type=F32)
            m_sc[h] = m_new

    @pl.when(jnp.logical_and(needed, full))
    def _():
        body(False)

    @pl.when(jnp.logical_and(needed, jnp.logical_not(full)))
    def _():
        body(True)

    @pl.when(kj == _last_kv_block(qi, tq, tk, q_off, nk))
    def _():
        o_ref[0] = jnp.concatenate([(acc_sc[h, :V_DIM] / acc_sc[h, V_DIM:V_DIM + 1]).T for h in range(MLA_HEADS)],
                                   axis=1).astype(o_ref.dtype)


def _rot_half(w):
    half = w.shape[-1] // 2
    return jnp.concatenate([-w[..., half:], w[..., :half]], axis=-1)


def _rope_tables(pos0, S):
    half = QK_ROPE // 2
    inv_freq = ROPE_THETA ** (-jnp.arange(half, dtype=F32) / half)
    ang = (pos0 + jnp.arange(S, dtype=jnp.int32)).astype(F32)[:, None] * inv_freq[None, :]
    cos = jnp.concatenate([jnp.cos(ang), jnp.cos(ang)], axis=-1)[None]
    sin = jnp.concatenate([jnp.sin(ang), jnp.sin(ang)], axis=-1)[None]
    return cos, sin, jnp.tile(cos, (1, 1, MLA_HEADS)), jnp.tile(sin, (1, 1, MLA_HEADS))


def _mla_branch(x, pos0, past_lat, past_kr, p):
    b, S, _ = x.shape
    ts = min(512, S)
    cos_k, sin_k, cos_q, sin_q = _rope_tables(pos0, S)
    q, lat, kr = pl.pallas_call(
        _mla_prep_kernel,
        grid=(b, S // ts),
        in_specs=[
            pl.BlockSpec((1, ts, D_MODEL), lambda bi, si: (bi, si, 0)),
            _full((D_MODEL, MLA_PROJ_COLS)), _full((1, Q_LORA)), _full((1, KV_LORA)),
            _full((Q_LORA, MLA_HEADS * (QK_NOPE + 2 * QK_ROPE))),
            pl.BlockSpec((1, ts, MLA_HEADS * QK_ROPE), lambda bi, si: (0, si, 0)),
            pl.BlockSpec((1, ts, MLA_HEADS * QK_ROPE), lambda bi, si: (0, si, 0)),
            pl.BlockSpec((1, ts, QK_ROPE), lambda bi, si: (0, si, 0)),
            pl.BlockSpec((1, ts, QK_ROPE), lambda bi, si: (0, si, 0)),
        ],
        out_specs=[
            pl.BlockSpec((1, MLA_HEADS, ts, QK_DIM), lambda bi, si: (bi, 0, si, 0)),
            pl.BlockSpec((1, ts, KV_LORA), lambda bi, si: (bi, si, 0)),
            pl.BlockSpec((1, ts, QK_ROPE), lambda bi, si: (bi, si, 0)),
        ],
        out_shape=[
            jax.ShapeDtypeStruct((b, MLA_HEADS, S, QK_DIM), BF16),
            jax.ShapeDtypeStruct((b, S, KV_LORA), F32),
            jax.ShapeDtypeStruct((b, S, QK_ROPE), F32),
        ],
        compiler_params=_cparams("parallel", "parallel"),
        name="mla_prep",
    )(x, p["w_mla"], p["q_norm"], p["kv_norm"], p["w_q"], cos_q, sin_q, cos_k, sin_k)

    if past_lat is None:
        lat_all, kr_all, q_off, sk_valid = lat, kr, 0, S
        tk = min(512, S)
    else:
        past_len = past_lat.shape[1]
        sk_valid = past_len + S
        tk = -(-sk_valid // LANES) * LANES
        pad = tk - sk_valid
        lat_all = jnp.concatenate([past_lat, lat, jnp.zeros((b, pad, KV_LORA), F32)], axis=1)
        kr_all = jnp.concatenate([past_kr, kr, jnp.zeros((b, pad, QK_ROPE), F32)], axis=1)
        q_off = past_len
    Sk = lat_all.shape[1]
    tku = min(512, Sk) if Sk % min(512, Sk) == 0 else Sk
    k, vt = pl.pallas_call(
        _kv_up_kernel,
        grid=(b, Sk // tku),
        in_specs=[
            pl.BlockSpec((1, tku, KV_LORA), lambda bi, si: (bi, si, 0)),
            pl.BlockSpec((1, tku, QK_ROPE), lambda bi, si: (bi, si, 0)),
            _full((KV_LORA, MLA_HEADS * (QK_NOPE + V_DIM))),
        ],
        out_specs=[
            pl.BlockSpec((1, MLA_HEADS, tku, QK_DIM), lambda bi, si: (bi, 0, si, 0)),
            pl.BlockSpec((1, MLA_HEADS, VT_ROWS, tku), lambda bi, si: (bi, 0, 0, si)),
        ],
        out_shape=[
            jax.ShapeDtypeStruct((b, MLA_HEADS, Sk, QK_DIM), BF16),
            jax.ShapeDtypeStruct((b, MLA_HEADS, VT_ROWS, Sk), BF16),
        ],
        compiler_params=_cparams("parallel", "parallel"),
        name="mla_kv_up",
    )(lat_all, kr_all, p["w_ukv"])

    tq = min(512, S)
    nq, nk = S // tq, Sk // tk
    kv_blk = lambda qi, kj: jnp.minimum(kj, _last_kv_block(qi, tq, tk, q_off, nk))
    y = pl.pallas_call(
        functools.partial(_flash_kernel, tq=tq, tk=tk, q_off=q_off, sk_valid=sk_valid, nk=nk),
        grid=(b, nq, nk),
        in_specs=[
            pl.BlockSpec((1, MLA_HEADS, tq, QK_DIM), lambda bi, qi, kj: (bi, 0, qi, 0)),
            pl.BlockSpec((1, MLA_HEADS, tk, QK_DIM), lambda bi, qi, kj: (bi, 0, kv_blk(qi, kj), 0)),
            pl.BlockSpec((1, MLA_HEADS, VT_ROWS, tk), lambda bi, qi, kj: (bi, 0, 0, kv_blk(qi, kj))),
        ],
        out_specs=pl.BlockSpec((1, tq, MLA_HEADS * V_DIM), lambda bi, qi, kj: (bi, qi, 0)),
        out_shape=jax.ShapeDtypeStruct((b, S, MLA_HEADS * V_DIM), BF16),
        scratch_shapes=[
            pltpu.VMEM((MLA_HEADS, 1, tq), F32),
            pltpu.VMEM((MLA_HEADS, VT_ROWS, tq), F32),
        ],
        compiler_params=_cparams("parallel", "parallel", "arbitrary"),
        name="mla_flash",
    )(q, k, vt)
    return y, lat, kr


def _rwkv_prep_kernel(x_ref, sp_ref, w_ref, mu_ref, w0_ref, a0_ref, kk_ref, ka_ref, rk_ref, ww2_ref, wa2_ref, wg2_ref,
                      hsum_ref, r_out, lw_out, k_out, v_out, kk_out, a_out, g_out, bon_out, sh_out, prev_sc):
    si = pl.program_id(1)

    @pl.when(si == 0)
    def _():
        prev_sc[...] = sp_ref[0]

    u = _dot(x_ref[0], w_ref[...])
    ts = u.shape[0]
    row = lax.broadcasted_iota(jnp.int32, u.shape, 0)
    shifted = jnp.where(row == 0, prev_sc[...], pltpu.roll(u, 1, axis=0))
    prev_sc[...] = u[ts - 1:ts, :]
    sh_out[0] = u[ts - 1:ts, :]
    m = u + (shifted - u) * mu_ref[...]
    r = m[:, :RWKV_DIM]
    k = m[:, RWKV_DIM:2 * RWKV_DIM]
    v = m[:, 2 * RWKV_DIM:3 * RWKV_DIM]
    o = 3 * RWKV_DIM
    wl = m[:, o:o + W_LORA]
    al = m[:, o + W_LORA:o + W_LORA + A_LORA]
    gl = m[:, o + W_LORA + A_LORA:]
    d = w0_ref[...] + _dot(jnp.tanh(wl), ww2_ref[...])
    lw_out[0] = -jnp.exp(-_softplus(-d) - 0.5)
    a = _sigmoid(a0_ref[...] + _dot(al, wa2_ref[...]))
    g_out[0] = _dot(_sigmoid(gl), wg2_ref[...])
    hsum = hsum_ref[...]
    kk = k * kk_ref[...]
    kk2 = kk * kk
    kk2_hi = kk2.astype(BF16)
    kk2_lo = (kk2 - kk2_hi.astype(F32)).astype(BF16)
    nrm = jnp.dot(kk2_hi, hsum, preferred_element_type=F32) + jnp.dot(kk2_lo, hsum, preferred_element_type=F32)
    kk_out[0] = kk * lax.rsqrt(nrm + 1e-12)
    kh = k * (1.0 + (a - 1.0) * ka_ref[...])
    rkr = r * kh * rk_ref[...]
    rkr_hi = rkr.astype(BF16)
    rkr_lo = (rkr - rkr_hi.astype(F32)).astype(BF16)
    bsum = jnp.dot(rkr_hi, hsum, preferred_element_type=F32) + jnp.dot(rkr_lo, hsum, preferred_element_type=F32)
    bon_out[0] = bsum * v
    r_out[0] = r
    k_out[0] = kh
    v_out[0] = v
    a_out[0] = a


def _rwkv_chunk_prepare(ins, c):
    r, lw, k, v, kk, a = (list(t) for t in zip(*ins))
    n = len(ins)
    L = r[0].shape[0]
    L2 = 2 * L
    lo = c["lane_lo"]

    def stack(xv):
        return jnp.concatenate([jnp.where(lo, xv, 0.0), jnp.where(lo, 0.0, xv)], axis=0)

    g = [_dot_exact_lhs(c["tri"], x) for x in lw]
    gl = [x[L - 1:L, :] for x in g]
    e_neg = [jnp.exp(-x) for x in g]
    at_s = [stack(-kk[i] * jnp.exp(g[i] - lw[i])) for i in range(n)]
    rt_s = [stack(r[i] * jnp.exp(g[i])) for i in range(n)]
    beta = [kk[i] * a[i] for i in range(n)]
    mm = []
    for i in range(n):
        bt = beta[i] * e_neg[i]
        kt = k[i] * e_neg[i]
        mm.append(_dot_nt(jnp.concatenate([at_s[i], rt_s[i]], axis=0), jnp.concatenate([bt, bt, kt, kt], axis=0)))
    nmat = [jnp.where(c["strict"], x[:L2, :L2], 0.0) for x in mm]
    mak = [jnp.where(c["strict"], x[:L2, L2:], 0.0) for x in mm]
    mrb = [jnp.where(c["incl"], x[L2:, :L2], 0.0) for x in mm]
    mrk = [jnp.where(c["incl"], x[L2:, L2:], 0.0) for x in mm]
    xinv = [c["eye2"] + x for x in nmat]
    pw = nmat
    for _ in range(int(math.log2(L)) - 1):
        pw = [_dot(x, x) for x in pw]
        xinv = [xinv[i] + _dot(xinv[i], pw[i]) for i in range(n)]
    vs = [stack(x) for x in v]
    w1 = [_dot(mak[i], vs[i]) for i in range(n)]
    au = [_dot(xinv[i], jnp.concatenate([at_s[i], w1[i]], axis=1)) for i in range(n)]
    ry = [_dot(mrb[i], au[i]) for i in range(n)]
    mv = [_dot(mrk[i], vs[i]) for i in range(n)]
    rh = [rt_s[i] + ry[i][:, :LANES] for i in range(n)]
    yh = [ry[i][:, LANES:] + mv[i] for i in range(n)]
    bs = [stack(beta[i] * jnp.exp(gl[i] - g[i])) for i in range(n)]
    ks = [stack(k[i] * jnp.exp(gl[i] - g[i])) for i in range(n)]
    pt_lr = [_dot_tn(bs[i], au[i][:, :LANES]) for i in range(n)]
    qt = [_dot_tn(jnp.concatenate([bs[i], ks[i]], axis=0), jnp.concatenate([au[i][:, LANES:], vs[i]], axis=0)) for i in range(n)]
    return [(rh[i], yh[i], pt_lr[i], qt[i], gl[i]) for i in range(n)]


def _rwkv_scan_kernel(r_ref, lw_ref, k_ref, v_ref, kk_ref, a_ref, g_ref, bon_ref, gng_ref, gnb_ref, z0_ref,
                      y_ref, zf_ref, z_sc, *, L, n_chunks):
    si = pl.program_id(1)

    @pl.when(si == 0)
    def _():
        z_sc[...] = z0_ref[0]

    L2 = 2 * L
    ri = lax.broadcasted_iota(jnp.int32, (L2, L2), 0)
    ci = lax.broadcasted_iota(jnp.int32, (L2, L2), 1)
    same = (ri // L) == (ci // L)
    rl = lax.broadcasted_iota(jnp.int32, (L, L), 0)
    cl = lax.broadcasted_iota(jnp.int32, (L, L), 1)
    r128 = lax.broadcasted_iota(jnp.int32, (LANES, LANES), 0)
    c128 = lax.broadcasted_iota(jnp.int32, (LANES, LANES), 1)
    consts = dict(
        tri=(cl <= rl).astype(BF16),
        strict=jnp.logical_and(same, (ci % L) < (ri % L)),
        incl=jnp.logical_and(same, (ci % L) <= (ri % L)),
        eye2=(ri == ci).astype(F32),
        lane_lo=lax.broadcasted_iota(jnp.int32, (L, LANES), 1) < RWKV_HEAD,
    )
    diag = r128 == c128
    gmean = jnp.where((r128 // RWKV_HEAD) == (c128 // RWKV_HEAD), 1.0 / RWKV_HEAD, 0.0).astype(BF16)
    npair = RWKV_HEADS // 2

    def head_mean(xv):
        hi = xv.astype(BF16)
        lo = (xv - hi.astype(F32)).astype(BF16)
        return jnp.dot(hi, gmean, preferred_element_type=F32) + jnp.dot(lo, gmean, preferred_element_type=F32)

    cpi = 2 if n_chunks % 2 == 0 else 1
    lanes = [slice(p * LANES, (p + 1) * LANES) for p in range(npair)]

    def chunk(ci_, carry):
        rows = [pl.ds(pl.multiple_of((ci_ * cpi + j) * L, L), L) for j in range(cpi)]
        ins = [(r_ref[0, rw, ln], lw_ref[0, rw, ln], k_ref[0, rw, ln], v_ref[0, rw, ln], kk_ref[0, rw, ln], a_ref[0, rw, ln])
               for rw in rows for ln in lanes]
        prep = _rwkv_chunk_prepare(ins, consts)
        z = [z_sc[p] for p in range(npair)]
        for j in range(cpi):
            pj = prep[j * npair:(j + 1) * npair]
            ys = [_dot(pj[p][0], z[p]) + pj[p][1] for p in range(npair)]
            zlr = [_dot(pj[p][2], z[p]) for p in range(npair)]
            y = [x[:L] + x[L:] for x in ys]
            mean = [head_mean(x) for x in y]
            yc = [y[p] - mean[p] for p in range(npair)]
            var = [head_mean(x * x) for x in yc]
            for p in range(npair):
                ln = lanes[p]
                gcol = jnp.sum(jnp.where(diag, jnp.broadcast_to(jnp.exp(pj[p][4]), (LANES, LANES)), 0.0), axis=1, keepdims=True)
                z[p] = gcol * z[p] + zlr[p] + pj[p][3]
                o = yc[p] * lax.rsqrt(var[p] + RWKV_GN_EPS) * gng_ref[:, ln] + gnb_ref[:, ln]
                y_ref[0, rows[j], ln] = ((o + bon_ref[0, rows[j], ln]) * g_ref[0, rows[j], ln]).astype(y_ref.dtype)
        for p in range(npair):
            z_sc[p] = z[p]
        return carry

    lax.fori_loop(0, n_chunks // cpi, chunk, 0)

    @pl.when(si == pl.num_programs(1) - 1)
    def _():
        zf_ref[0] = z_sc[...]


def _rwkv_branch(x, s0, shift_prev, p):
    b, S, _ = x.shape
    ts = min(512, S)
    tok = lambda w: pl.BlockSpec((1, ts, w), lambda bi, si: (bi, si, 0))
    row = lambda w: pl.BlockSpec((1, 1, w), lambda bi, si: (bi, 0, 0))
    outs = pl.pallas_call(
        _rwkv_prep_kernel,
        grid=(b, S // ts),
        in_specs=[tok(D_MODEL), row(RWKV_COLS), _full((D_MODEL, RWKV_COLS)), _full((1, RWKV_COLS))]
        + [_full((1, RWKV_DIM))] * 5
        + [_full((W_LORA, RWKV_DIM)), _full((A_LORA, RWKV_DIM)), _full((G_LORA, RWKV_DIM)), _full((RWKV_DIM, RWKV_DIM))],
        out_specs=[tok(RWKV_DIM)] * 8 + [row(RWKV_COLS)],
        out_shape=[jax.ShapeDtypeStruct((b, S, RWKV_DIM), F32)] * 8 + [jax.ShapeDtypeStruct((b, 1, RWKV_COLS), F32)],
        scratch_shapes=[pltpu.VMEM((1, RWKV_COLS), F32)],
        compiler_params=_cparams("parallel", "arbitrary"),
        name="rwkv_prep",
    )(x, shift_prev, p["w_rwkv"], p["mu"], p["w0"], p["a0"], p["k_k"], p["k_a"], p["r_k"], p["w_w2"], p["w_a2"], p["w_g2"],
      p["head_sum"])
    r, lw, kh, v, kk, a, g, bonus, shift_new = outs

    L = min(CHUNK, S)
    tb = min(4 * L, S)
    npair = RWKV_HEADS // 2
    zt = jnp.swapaxes(s0.astype(F32), 2, 3).reshape(b, npair, 2, RWKV_HEAD, RWKV_HEAD)
    z0 = jnp.einsum("bpikv,ij->bpikjv", zt, jnp.eye(2, dtype=F32)).reshape(b, npair, LANES, LANES)
    tokb = lambda: pl.BlockSpec((1, tb, RWKV_DIM), lambda bi, si: (bi, si, 0))
    zspec = pl.BlockSpec((1, npair, LANES, LANES), lambda bi, si: (bi, 0, 0, 0))
    y, zf = pl.pallas_call(
        functools.partial(_rwkv_scan_kernel, L=L, n_chunks=tb // L),
        grid=(b, S // tb),
        in_specs=[tokb() for _ in range(8)] + [_full((1, RWKV_DIM)), _full((1, RWKV_DIM)), zspec],
        out_specs=[tokb(), zspec],
        out_shape=[jax.ShapeDtypeStruct((b, S, RWKV_DIM), BF16), jax.ShapeDtypeStruct((b, npair, LANES, LANES), F32)],
        scratch_shapes=[pltpu.VMEM((npair, LANES, LANES), F32)],
        compiler_params=_cparams("parallel", "arbitrary"),
        name="rwkv_scan",
    )(r, lw, kh, v, kk, a, g, bonus, p["gn_g"], p["gn_b"], z0)
    zd = jnp.einsum("bpikiv->bpikv", zf.reshape(b, npair, 2, RWKV_HEAD, 2, RWKV_HEAD))
    s_new = jnp.swapaxes(zd.reshape(b, RWKV_HEADS, RWKV_HEAD, RWKV_HEAD), 2, 3)
    return y, s_new, shift_new


def _ssd_kernel(x_ref, w_ref, s0_ref, cp_ref, cw_ref, cb_ref, dtb_ref, a_ref, dsk_ref, ng_ref,
                y_ref, sf_ref, ct_ref, st_sc, tail_sc, *, L):
    si = pl.program_id(1)

    @pl.when(si == 0)
    def _():
        st_sc[...] = s0_ref[0]
        tail_sc[...] = cp_ref[0]

    u = _dot(x_ref[0], w_ref[...])
    z = u[:, :SSM_DIM]
    xbc = u[:, SSM_DIM:SSM_DIM + CONV_DIM]
    dtr = u[:, SSM_DIM + CONV_DIM:]
    tail = tail_sc[...]
    row = lax.broadcasted_iota(jnp.int32, xbc.shape, 0)
    sh1 = jnp.where(row == 0, tail[7:8], pltpu.roll(xbc, 1, axis=0))
    sh2 = jnp.where(row == 0, tail[6:7], pltpu.roll(sh1, 1, axis=0))
    sh3 = jnp.where(row == 0, tail[5:6], pltpu.roll(sh2, 1, axis=0))
    tail_sc[...] = xbc[L - 8:L]
    ct_ref[0] = xbc[L - 8:L]
    conv = cb_ref[...] + cw_ref[3:4] * xbc + cw_ref[2:3] * sh1 + cw_ref[1:2] * sh2 + cw_ref[0:1] * sh3
    act = conv * _sigmoid(conv)
    xs = act[:, :SSM_DIM]
    gw = SSM_STATE
    dt = _softplus(dtr + dtb_ref[...])
    a = dt * a_ref[...]
    rl = lax.broadcasted_iota(jnp.int32, (L, L), 0)
    cl = lax.broadcasted_iota(jnp.int32, (L, L), 1)
    causal = cl <= rl
    cum = _dot_exact_lhs(causal.astype(BF16), a)
    sel = (lax.broadcasted_iota(jnp.int32, (16, LANES), 0) == lax.broadcasted_iota(jnp.int32, (16, LANES), 1)).astype(BF16)
    cum_t = _dot_nt_exact_lhs(sel, cum)
    dt_t = _dot_nt_exact_lhs(sel, dt)
    ys = []
    hpg = SSM_HEADS // SSM_GROUPS
    for gi in range(SSM_GROUPS):
        bg = act[:, SSM_DIM + gi * gw:SSM_DIM + (gi + 1) * gw]
        cg = act[:, SSM_DIM + SSM_GROUPS * gw + gi * gw:SSM_DIM + SSM_GROUPS * gw + (gi + 1) * gw]
        cb = _dot_nt(cg, bg)
        for h in range(gi * hpg, (gi + 1) * hpg):
            xh = xs[:, h * SSM_HEADDIM:(h + 1) * SSM_HEADDIM]
            cc = cum[:, h:h + 1]
            seg = cc - cum_t[h:h + 1, :]
            dec = jnp.where(causal, jnp.exp(jnp.minimum(seg, 0.0)), 0.0)
            sc = cb * dec * dt_t[h:h + 1, :]
            st = st_sc[h]
            yh = _dot(sc, xh) + _dot_nt(cg, st) * jnp.exp(cc)
            clast = cum[L - 1:L, h:h + 1]
            wcol = jnp.exp(clast - cc) * dt[:, h:h + 1]
            st_sc[h] = st * jnp.exp(clast) + _dot_tn_split(xh * wcol, bg)
            ys.append(yh)
    y = jnp.concatenate(ys, axis=1) + dsk_ref[...] * xs
    y = y * (z * _sigmoid(z))
    gdim = SSM_DIM // SSM_GROUPS
    outs = []
    for gi in range(SSM_GROUPS):
        yg = y[:, gi * gdim:(gi + 1) * gdim]
        outs.append(yg * lax.rsqrt(jnp.mean(yg * yg, axis=-1, keepdims=True) + RMS_EPS))
    y_ref[0] = (jnp.concatenate(outs, axis=1) * ng_ref[...]).astype(y_ref.dtype)

    @pl.when(si == pl.num_programs(1) - 1)
    def _():
        sf_ref[0] = st_sc[...]


def _ssm_branch(x, s0, conv_prev, p):
    b, S, _ = x.shape
    L = min(256, S)
    cp = jnp.concatenate([jnp.zeros((b, 8 - (CONV_W - 1), CONV_DIM), F32), conv_prev.astype(F32)], axis=1)
    sspec = pl.BlockSpec((1, SSM_HEADS, SSM_HEADDIM, SSM_STATE), lambda bi, si: (bi, 0, 0, 0))
    cspec = pl.BlockSpec((1, 8, CONV_DIM), lambda bi, si: (bi, 0, 0))
    y, s_new, ctail = pl.pallas_call(
        functools.partial(_ssd_kernel, L=L),
        grid=(b, S // L),
        in_specs=[
            pl.BlockSpec((1, L, D_MODEL), lambda bi, si: (bi, si, 0)),
            _full((D_MODEL, SSM_PROJ_COLS)), sspec, cspec,
            _full((CONV_W, CONV_DIM)), _full((1, CONV_DIM)), _full((1, LANES)), _full((1, LANES)),
            _full((1, SSM_DIM)), _full((1, SSM_DIM)),
        ],
        out_specs=[pl.BlockSpec((1, L, SSM_DIM), lambda bi, si: (bi, si, 0)), sspec, cspec],
        out_shape=[
            jax.ShapeDtypeStruct((b, S, SSM_DIM), BF16),
            jax.ShapeDtypeStruct((b, SSM_HEADS, SSM_HEADDIM, SSM_STATE), F32),
            jax.ShapeDtypeStruct((b, 8, CONV_DIM), F32),
        ],
        scratch_shapes=[pltpu.VMEM((SSM_HEADS, SSM_HEADDIM, SSM_STATE), F32), pltpu.VMEM((8, CONV_DIM), F32)],
        compiler_params=_cparams("parallel", "arbitrary"),
        name="ssd",
    )(x, p["w_ssm"], s0.astype(F32), cp, p["conv_w"], p["conv_b"], p["dt_bias"], p["a_neg"], p["d_skip"], p["ssm_norm_g"])
    return y, s_new, ctail[:, 8 - (CONV_W - 1):]


def _merge_kernel(x_ref, ym_ref, yr_ref, ys_ref, wg_ref, wb_ref, wo_ref, g_ref, b_ref, rwh_ref, rwl_ref, rb_ref,
                  h_ref, hb_ref, ti_ref, tg_ref):
    x = x_ref[...]
    gates = _sigmoid(_dot(x, wg_ref[...]))
    mix = gates[:, :D_MODEL] * _dot(ym_ref[...], wb_ref[0])
    mix = mix + gates[:, D_MODEL:2 * D_MODEL] * _dot(yr_ref[...], wb_ref[1])
    mix = mix + gates[:, 2 * D_MODEL:] * _dot(ys_ref[...], wb_ref[2])
    h = _layer_norm(DEEPNORM_ALPHA * x + _dot(mix, wo_ref[...]), g_ref[...], b_ref[...])
    h_ref[...] = h
    hb_ref[...] = h.astype(BF16)
    h_hi = h.astype(BF16)
    h_lo = (h - h_hi.astype(F32)).astype(BF16)
    logits = (jnp.dot(h_hi, rwh_ref[...], preferred_element_type=F32) + jnp.dot(h_lo, rwh_ref[...], preferred_element_type=F32)
              + jnp.dot(h_hi, rwl_ref[...], preferred_element_type=F32) + rb_ref[...])
    lane = lax.broadcasted_iota(jnp.int32, logits.shape, 1)
    idx_out = jnp.zeros(logits.shape, jnp.int32)
    val_out = jnp.zeros(logits.shape, F32)
    top = None
    den = None
    for kth in range(TOP_K):
        mval = jnp.max(logits, axis=-1, keepdims=True)
        midx = jnp.min(jnp.where(logits == mval, lane, LANES), axis=-1, keepdims=True)
        if kth == 0:
            top = mval
            e = jnp.ones_like(mval)
            den = e
        else:
            e = jnp.exp(mval - top)
            den = den + e
        idx_out = jnp.where(lane == kth, midx, idx_out)
        val_out = jnp.where(lane == kth, e, val_out)
        logits = jnp.where(lane == midx, NEG_BIG * 2.0, logits)
    ti_ref[...] = idx_out
    tg_ref[...] = val_out / den


def _merge(x2, ym, yr, ys, p):
    T = x2.shape[0]
    tm = min(512, T)
    tok = lambda w: pl.BlockSpec((tm, w), lambda i: (i, 0))
    return pl.pallas_call(
        _merge_kernel,
        grid=(T // tm,),
        in_specs=[tok(D_MODEL), tok(BRANCH_DIM), tok(BRANCH_DIM), tok(BRANCH_DIM),
                  _full((D_MODEL, N_BRANCH * D_MODEL)), _full((N_BRANCH, BRANCH_DIM, D_MODEL)), _full((D_MODEL, D_MODEL)),
                  _full((1, D_MODEL)), _full((1, D_MODEL)), _full((D_MODEL, LANES)), _full((D_MODEL, LANES)), _full((1, LANES))],
        out_specs=[tok(D_MODEL), tok(D_MODEL), tok(LANES), tok(LANES)],
        out_shape=[jax.ShapeDtypeStruct((T, D_MODEL), F32), jax.ShapeDtypeStruct((T, D_MODEL), BF16),
                   jax.ShapeDtypeStruct((T, LANES), jnp.int32), jax.ShapeDtypeStruct((T, LANES), F32)],
        compiler_params=_cparams("parallel"),
        name="merge_router",
    )(x2, ym, yr, ys, p["w_gate"], p["w_branch"], p["w_out"], p["ln1_g"], p["ln1_b"], p["router_hi"], p["router_lo"], p["router_b"])


def _expert_kernel(be_ref, nu_ref, x_ref, wgu_ref, bgu_ref, wd_ref, bd_ref, o_ref, wgu_sc, wd_sc):
    i = pl.program_id(0)
    used = i < nu_ref[0]
    new_expert = jnp.logical_or(i == 0, be_ref[i] != be_ref[jnp.maximum(i - 1, 0)])

    @pl.when(jnp.logical_and(used, new_expert))
    def _():
        wgu_sc[...] = wgu_ref[0].astype(BF16)
        wd_sc[...] = wd_ref[0].astype(BF16)

    @pl.when(used)
    def _():
        hgu = jnp.dot(x_ref[...], wgu_sc[...], preferred_element_type=F32) + bgu_ref[0]
        gate = jnp.minimum(hgu[:, :D_FF], SWIGLU_LIMIT)
        up = jnp.clip(hgu[:, D_FF:], -SWIGLU_LIMIT, SWIGLU_LIMIT)
        hid = gate * _sigmoid(SWIGLU_ALPHA * gate) * (up + 1.0)
        o_ref[...] = (jnp.dot(hid.astype(BF16), wd_sc[...], preferred_element_type=F32) + bd_ref[0]).astype(o_ref.dtype)

    @pl.when(jnp.logical_not(used))
    def _():
        o_ref[...] = jnp.zeros(o_ref.shape, o_ref.dtype)


def _combine_kernel(y0_ref, y1_ref, y2_ref, y3_ref, tg_ref, h_ref, g_ref, b_ref, o_ref):
    tg = tg_ref[...]
    f = tg[:, 0:1] * y0_ref[...].astype(F32)
    for kth, y_ref in ((1, y1_ref), (2, y2_ref), (3, y3_ref)):
        f = f + tg[:, kth:kth + 1] * y_ref[...].astype(F32)
    o_ref[...] = _layer_norm(DEEPNORM_ALPHA * h_ref[...] + f, g_ref[...], b_ref[...])


def _moe(h, h_bf, top_i, top_g, p):
    T = h.shape[0]
    n_assign = T * TOP_K
    flat_e = top_i[:, :TOP_K].reshape(-1)
    ids = jnp.arange(n_assign, dtype=jnp.int32)
    skey = jnp.sort(flat_e * n_assign + ids)
    sorted_e = skey // n_assign
    order = skey - sorted_e * n_assign
    experts = jnp.arange(N_EXPERTS, dtype=jnp.int32)
    counts = jnp.sum((flat_e[:, None] == experts[None, :]).astype(jnp.int32), axis=0)
    padded = (counts + MOE_ROWS - 1) // MOE_ROWS * MOE_ROWS
    pad_end = jnp.cumsum(padded)
    pad_start = pad_end - padded
    start = jnp.cumsum(counts) - counts
    n_blocks = -(-(n_assign + N_EXPERTS * (MOE_ROWS - 1)) // MOE_ROWS)
    n_rows = n_blocks * MOE_ROWS
    blk_row0 = jnp.arange(n_blocks, dtype=jnp.int32) * MOE_ROWS
    block_e = jnp.minimum(jnp.sum((pad_end[None, :] <= blk_row0[:, None]).astype(jnp.int32), axis=1), N_EXPERTS - 1)
    n_used = (pad_end[-1:] // MOE_ROWS).astype(jnp.int32)
    within = (blk_row0 - pad_start[block_e])[:, None] + jnp.arange(MOE_ROWS, dtype=jnp.int32)[None, :]
    valid = within < counts[block_e][:, None]
    src = jnp.clip(start[block_e][:, None] + within, 0, n_assign - 1)
    row_tok = jnp.where(valid, order[src.reshape(-1)].reshape(n_blocks, MOE_ROWS) // TOP_K, 0).reshape(-1)
    xb = h_bf[row_tok]

    yb = pl.pallas_call(
        _expert_kernel,
        grid_spec=pltpu.PrefetchScalarGridSpec(
            num_scalar_prefetch=2,
            grid=(n_blocks,),
            in_specs=[
                pl.BlockSpec((MOE_ROWS, D_MODEL), lambda i, be, nu: (i, 0)),
                pl.BlockSpec((1, D_MODEL, 2 * D_FF), lambda i, be, nu: (be[i], 0, 0)),
                pl.BlockSpec((1, 1, 2 * D_FF), lambda i, be, nu: (be[i], 0, 0)),
                pl.BlockSpec((1, D_FF, D_MODEL), lambda i, be, nu: (be[i], 0, 0)),
                pl.BlockSpec((1, 1, D_MODEL), lambda i, be, nu: (be[i], 0, 0)),
            ],
            out_specs=pl.BlockSpec((MOE_ROWS, D_MODEL), lambda i, be, nu: (i, 0)),
            scratch_shapes=[pltpu.VMEM((D_MODEL, 2 * D_FF), BF16), pltpu.VMEM((D_FF, D_MODEL), BF16)],
        ),
        out_shape=jax.ShapeDtypeStruct((n_rows, D_MODEL), BF16),
        compiler_params=_cparams("arbitrary"),
        name="moe_experts",
    )(block_e, n_used, xb, p["w_gu"], p["b_gu"], p["w_down"], p["b_down"])

    dest_sorted = pad_start[sorted_e] + (ids - start[sorted_e])
    _, dest = lax.sort((order, dest_sorted), num_keys=1)
    dest = dest.reshape(T, TOP_K)
    ygs = [yb[dest[:, kth]] for kth in range(TOP_K)]
    tm = min(512, T)
    tok = lambda w: pl.BlockSpec((tm, w), lambda i: (i, 0))
    return pl.pallas_call(
        _combine_kernel,
        grid=(T // tm,),
        in_specs=[tok(D_MODEL)] * TOP_K + [tok(LANES), tok(D_MODEL), _full((1, D_MODEL)), _full((1, D_MODEL))],
        out_specs=tok(D_MODEL),
        out_shape=jax.ShapeDtypeStruct((T, D_MODEL), F32),
        compiler_params=_cparams("parallel"),
        name="moe_combine_ln",
    )(*ygs, top_g, h, p["ln2_g"], p["ln2_b"])


def _prep_layer(l, w):
    w_in = w["w_in"][l]
    kr_cols = w_in[:, MLA_COLS - QK_ROPE:MLA_COLS]
    w_mla = jnp.concatenate([w_in[:, :MLA_COLS], _rot_half(kr_cols),
                             jnp.zeros((D_MODEL, MLA_PROJ_COLS - MLA_COLS - QK_ROPE), F32)], axis=1)
    wq = w["mla_w_uq"][l].reshape(Q_LORA, MLA_HEADS, QK_DIM)
    wq_rope = wq[:, :, QK_NOPE:]
    w_q = jnp.concatenate([wq[:, :, :QK_NOPE].reshape(Q_LORA, -1), wq_rope.reshape(Q_LORA, -1),
                           _rot_half(wq_rope).reshape(Q_LORA, -1)], axis=1)
    o_r = MLA_COLS
    o_s = o_r + RWKV_COLS
    o_g = o_s + SSM_COLS
    w_ssm = jnp.concatenate([w_in[:, o_s:o_s + SSM_DIM + CONV_DIM], w_in[:, o_s + SSM_DIM + CONV_DIM:o_g],
                             jnp.zeros((D_MODEL, LANES - SSM_HEADS), F32)], axis=1)
    pad8 = lambda v: jnp.concatenate([v.astype(F32), jnp.zeros((LANES - SSM_HEADS,), F32)])[None]
    row = lambda v: v.astype(F32)[None]
    hid = jnp.arange(RWKV_DIM) // RWKV_HEAD
    rw = jnp.concatenate([w["router_w"][l], jnp.zeros((D_MODEL, LANES - N_EXPERTS), F32)], axis=1)
    rw_hi = rw.astype(BF16)
    return dict(
        w_mla=w_mla.astype(BF16), q_norm=row(w["mla_q_norm"][l]), kv_norm=row(w["mla_kv_norm"][l]), w_q=w_q.astype(BF16),
        w_ukv=w["mla_w_ukv"][l].astype(BF16),
        w_rwkv=w_in[:, o_r:o_s].astype(BF16), mu=row(w["rwkv_mu"][l]), w0=row(w["rwkv_w0"][l]), a0=row(w["rwkv_a0"][l]),
        k_k=row(w["rwkv_k_k"][l]), k_a=row(w["rwkv_k_a"][l]), r_k=row(w["rwkv_r_k"][l]),
        w_w2=w["rwkv_w_w2"][l].astype(BF16), w_a2=w["rwkv_w_a2"][l].astype(BF16), w_g2=w["rwkv_w_g2"][l].astype(BF16),
        head_sum=(hid[:, None] == hid[None, :]).astype(BF16),
        gn_g=row(w["rwkv_gn_g"][l]), gn_b=row(w["rwkv_gn_b"][l]),
        w_ssm=w_ssm.astype(BF16), conv_w=w["ssm_conv_w"][l].astype(F32), conv_b=row(w["ssm_conv_b"][l]),
        dt_bias=pad8(w["ssm_dt_bias"][l]), a_neg=pad8(-jnp.exp(w["ssm_a_log"][l].astype(F32))),
        d_skip=row(jnp.repeat(w["ssm_d"][l], SSM_HEADDIM)), ssm_norm_g=row(w["ssm_norm_g"][l]),
        w_gate=w_in[:, o_g:].astype(BF16), w_branch=w["w_branch"][l].astype(BF16), w_out=w["w_out"][l].astype(BF16),
        ln1_g=row(w["ln1_g"][l]), ln1_b=row(w["ln1_b"][l]),
        router_hi=rw_hi, router_lo=(rw - rw_hi.astype(F32)).astype(BF16),
        router_b=jnp.concatenate([w["router_b"][l].astype(F32), jnp.full((LANES - N_EXPERTS,), NEG_BIG, F32)])[None],
        w_gu=w["expert_w_gu"][l], b_gu=w["expert_b_gu"][l].astype(F32)[:, None, :],
        w_down=w["expert_w_down"][l], b_down=w["expert_b_down"][l].astype(F32)[:, None, :],
        ln2_g=row(w["ln2_g"][l]), ln2_b=row(w["ln2_b"][l]),
    )


def _trunk_layer(x, pos0, past_lat, past_kr, rwkv_s0, shift_prev, ssm_s0, conv_prev, p):
    b, S, _ = x.shape
    y_mla, lat, kr = _mla_branch(x, pos0, past_lat, past_kr, p)
    y_rwkv, rwkv_s, shift_new = _rwkv_branch(x, rwkv_s0, shift_prev, p)
    y_ssm, ssm_s, conv_new = _ssm_branch(x, ssm_s0, conv_prev, p)
    T = b * S
    flat = lambda t: t.reshape(T, t.shape[-1])
    h, h_bf, top_i, top_g = _merge(flat(x), flat(y_mla), flat(y_rwkv), flat(y_ssm), p)
    out = _moe(h, h_bf, top_i, top_g, p)
    return out.reshape(b, S, D_MODEL), (lat, kr, rwkv_s, shift_new, ssm_s, conv_new)


def kernel(x_prompt, x_sample, cache_mla_latent, cache_mla_krope, state_rwkv, state_rwkv_shift, state_ssm, state_ssm_conv, w_in, mla_q_norm, mla_kv_norm, mla_w_uq, mla_w_ukv, rwkv_mu, rwkv_w0, rwkv_w_w2, rwkv_a0, rwkv_w_a2, rwkv_w_g2, rwkv_k_k, rwkv_k_a, rwkv_r_k, rwkv_gn_g, rwkv_gn_b, ssm_conv_w, ssm_conv_b, ssm_dt_bias, ssm_a_log, ssm_d, ssm_norm_g, w_branch, w_out, ln1_g, ln1_b, router_w, router_b, expert_w_gu, expert_b_gu, expert_w_down, expert_b_down, ln2_g, ln2_b):
    w = dict(w_in=w_in, mla_q_norm=mla_q_norm, mla_kv_norm=mla_kv_norm, mla_w_uq=mla_w_uq, mla_w_ukv=mla_w_ukv,
             rwkv_mu=rwkv_mu, rwkv_w0=rwkv_w0, rwkv_w_w2=rwkv_w_w2, rwkv_a0=rwkv_a0, rwkv_w_a2=rwkv_w_a2, rwkv_w_g2=rwkv_w_g2,
             rwkv_k_k=rwkv_k_k, rwkv_k_a=rwkv_k_a, rwkv_r_k=rwkv_r_k, rwkv_gn_g=rwkv_gn_g, rwkv_gn_b=rwkv_gn_b,
             ssm_conv_w=ssm_conv_w, ssm_conv_b=ssm_conv_b, ssm_dt_bias=ssm_dt_bias, ssm_a_log=ssm_a_log, ssm_d=ssm_d,
             ssm_norm_g=ssm_norm_g, w_branch=w_branch, w_out=w_out, ln1_g=ln1_g, ln1_b=ln1_b, router_w=router_w,
             router_b=router_b, expert_w_gu=expert_w_gu, expert_b_gu=expert_b_gu, expert_w_down=expert_w_down,
             expert_b_down=expert_b_down, ln2_g=ln2_g, ln2_b=ln2_b)
    bp = x_prompt.shape[0]
    past_len = cache_mla_latent.shape[2]
    zero_rwkv = jnp.zeros((bp, RWKV_HEADS, RWKV_HEAD, RWKV_HEAD), F32)
    zero_shift = jnp.zeros((bp, 1, RWKV_COLS), F32)
    zero_ssm = jnp.zeros((bp, SSM_HEADS, SSM_HEADDIM, SSM_STATE), F32)
    zero_conv = jnp.zeros((bp, CONV_W - 1, CONV_DIM), F32)
    yp, ys = x_prompt, x_sample
    st_p = [[] for _ in range(6)]
    st_s = [[] for _ in range(6)]
    for l in range(DEPTH):
        p = _prep_layer(l, w)
        yp, new_p = _trunk_layer(yp, 0, None, None, zero_rwkv, zero_shift, zero_ssm, zero_conv, p)
        ys, new_s = _trunk_layer(ys, past_len, cache_mla_latent[l], cache_mla_krope[l], state_rwkv[l], state_rwkv_shift[l],
                                 state_ssm[l], state_ssm_conv[l], p)
        for i in range(6):
            st_p[i].append(new_p[i])
            st_s[i].append(new_s[i])
    outs_p = [jnp.stack(t, axis=0) for t in st_p]
    outs_s = [jnp.stack(t, axis=0) for t in st_s]
    return (yp, ys, *outs_p, *outs_s)
```

```python
import functools
import math

import jax
import jax.numpy as jnp
from jax import lax
from jax.experimental import pallas as pl
from jax.experimental.pallas import tpu as pltpu

F32 = jnp.float32
BF16 = jnp.bfloat16

D_MODEL = 1024
DEPTH = 4
CHUNK = 64
MLA_HEADS = 8
QK_NOPE = 64
QK_ROPE = 32
QK_DIM = QK_NOPE + QK_ROPE
V_DIM = 64
VT_ROWS = V_DIM + 16
Q_LORA = 384
KV_LORA = 256
ROPE_THETA = 10000.0
RWKV_HEADS = 8
RWKV_HEAD = 64
RWKV_DIM = RWKV_HEADS * RWKV_HEAD
W_LORA = 64
A_LORA = 64
G_LORA = 128
RWKV_GN_EPS = 64e-5
SSM_HEADS = 8
SSM_HEADDIM = 64
SSM_DIM = SSM_HEADS * SSM_HEADDIM
SSM_STATE = 64
SSM_GROUPS = 2
CONV_W = 4
CONV_DIM = SSM_DIM + 2 * SSM_GROUPS * SSM_STATE
N_BRANCH = 3
BRANCH_DIM = 512
N_EXPERTS = 32
TOP_K = 4
D_FF = 1024
SWIGLU_LIMIT = 7.0
SWIGLU_ALPHA = 1.702
DEEPNORM_ALPHA = (2.0 * DEPTH) ** 0.25
LN_EPS = 1e-5
RMS_EPS = 1e-6
MLA_COLS = Q_LORA + KV_LORA + QK_ROPE
RWKV_COLS = 3 * RWKV_DIM + W_LORA + A_LORA + G_LORA
SSM_COLS = SSM_DIM + CONV_DIM + SSM_HEADS

LANES = 128
MLA_PROJ_COLS = 768
SSM_PROJ_COLS = SSM_DIM + CONV_DIM + LANES
MOE_ROWS = 512
NEG_BIG = -1e30
VMEM_LIMIT = 56 * 1024 * 1024


def _cparams(*sem):
    return pltpu.CompilerParams(dimension_semantics=sem, vmem_limit_bytes=VMEM_LIMIT)


def _dot(a, b):
    return jnp.dot(a.astype(BF16), b.astype(BF16), preferred_element_type=F32)


def _dot_nt(a, b):
    return lax.dot_general(a.astype(BF16), b.astype(BF16), (((1,), (1,)), ((), ())), preferred_element_type=F32)


def _dot_tn(a, b):
    return lax.dot_general(a.astype(BF16), b.astype(BF16), (((0,), (0,)), ((), ())), preferred_element_type=F32)


def _dot_tn_split(a, b):
    a_hi = a.astype(BF16)
    a_lo = (a - a_hi.astype(F32)).astype(BF16)
    b_hi = b.astype(BF16)
    b_lo = (b - b_hi.astype(F32)).astype(BF16)
    dn = (((0,), (0,)), ((), ()))
    return (lax.dot_general(a_hi, b_hi, dn, preferred_element_type=F32) + lax.dot_general(a_lo, b_hi, dn, preferred_element_type=F32)
            + lax.dot_general(a_hi, b_lo, dn, preferred_element_type=F32))


def _split3(x):
    hi = x.astype(BF16)
    r1 = x - hi.astype(F32)
    mid = r1.astype(BF16)
    lo = (r1 - mid.astype(F32)).astype(BF16)
    return hi, mid, lo


def _dot_exact_lhs(m, x):
    hi, mid, lo = _split3(x)
    return (jnp.dot(m, hi, preferred_element_type=F32) + jnp.dot(m, mid, preferred_element_type=F32)
            + jnp.dot(m, lo, preferred_element_type=F32))


def _dot_nt_exact_lhs(m, x):
    dn = (((1,), (1,)), ((), ()))
    hi, mid, lo = _split3(x)
    return (lax.dot_general(m, hi, dn, preferred_element_type=F32) + lax.dot_general(m, mid, dn, preferred_element_type=F32)
            + lax.dot_general(m, lo, dn, preferred_element_type=F32))


def _sigmoid(x):
    return 1.0 / (1.0 + jnp.exp(-x))


def _softplus(x):
    return jnp.maximum(x, 0.0) + jnp.log(1.0 + jnp.exp(-jnp.abs(x)))


def _rms(x, g):
    return x * lax.rsqrt(jnp.mean(x * x, axis=-1, keepdims=True) + RMS_EPS) * g


def _layer_norm(x, g, b):
    mu = jnp.mean(x, axis=-1, keepdims=True)
    xc = x - mu
    var = jnp.mean(xc * xc, axis=-1, keepdims=True)
    return xc * lax.rsqrt(var + LN_EPS) * g + b


def _full(shape):
    return pl.BlockSpec(shape, lambda *_: (0,) * len(shape))


def _mla_prep_kernel(x_ref, w_ref, qn_ref, kvn_ref, wq_ref, cq_ref, sq_ref, ck_ref, sk_ref, q_ref, lat_ref, kr_ref):
    u = _dot(x_ref[0], w_ref[...])
    c_q = u[:, :Q_LORA]
    c_kv = u[:, Q_LORA:Q_LORA + KV_LORA]
    kr = u[:, MLA_COLS - QK_ROPE:MLA_COLS]
    kr_rot = u[:, MLA_COLS:MLA_COLS + QK_ROPE]
    qall = _dot(_rms(c_q, qn_ref[...]), wq_ref[...])
    nope_w = MLA_HEADS * QK_NOPE
    rope_w = MLA_HEADS * QK_ROPE
    q_rope = qall[:, nope_w:nope_w + rope_w] * cq_ref[0] + qall[:, nope_w + rope_w:] * sq_ref[0]
    scale = QK_DIM ** -0.5 * math.log2(math.e)
    for h in range(MLA_HEADS):
        qh = jnp.concatenate([qall[:, h * QK_NOPE:(h + 1) * QK_NOPE], q_rope[:, h * QK_ROPE:(h + 1) * QK_ROPE]], axis=1)
        q_ref[0, h] = (qh * scale).astype(BF16)
    lat_ref[0] = _rms(c_kv, kvn_ref[...])
    kr_ref[0] = kr * ck_ref[0] + kr_rot * sk_ref[0]


def _kv_up_kernel(lat_ref, kr_ref, w_ref, k_ref, vt_ref):
    kv = _dot(lat_ref[0], w_ref[...])
    kr = kr_ref[0]
    hw = QK_NOPE + V_DIM
    ones_rows = (lax.broadcasted_iota(jnp.int32, (VT_ROWS - V_DIM, kv.shape[0]), 0) == 0).astype(F32)
    for h in range(MLA_HEADS):
        k_ref[0, h] = jnp.concatenate([kv[:, h * hw:h * hw + QK_NOPE], kr], axis=1).astype(BF16)
        vt_ref[0, h] = jnp.concatenate([kv[:, h * hw + QK_NOPE:(h + 1) * hw].T, ones_rows], axis=0).astype(BF16)


def _last_kv_block(qi, tq, tk, q_off, nk):
    q_hi = q_off + qi * tq + tq - 1
    return min(nk - 1, ((q_hi // CHUNK) * CHUNK + CHUNK - 1) // tk)


def _flash_kernel(qi_ref, kj_ref, last_ref, q_ref, k_ref, vt_ref, o_ref, m_sc, acc_sc, *, tq, tk, q_off, sk_valid):
    t = pl.program_id(1)
    qi = qi_ref[t]
    kj = kj_ref[t]

    @pl.when(kj == 0)
    def _():
        m_sc[...] = jnp.full(m_sc.shape, NEG_BIG, F32)
        acc_sc[...] = jnp.zeros(acc_sc.shape, F32)

    q_lo = q_off + qi * tq
    k_lo = kj * tk
    k_hi = k_lo + tk - 1
    full = jnp.logical_and((k_hi // CHUNK) <= (q_lo // CHUNK), k_hi < sk_valid)

    def body(masked):
        if masked:
            key = k_lo + lax.broadcasted_iota(jnp.int32, (tk, tq), 0)
            qry = q_lo + lax.broadcasted_iota(jnp.int32, (tk, tq), 1)
            mask = jnp.logical_and((key // CHUNK) <= (qry // CHUNK), key < sk_valid)
        s_next = _dot_nt(k_ref[0, 0], q_ref[0, 0])
        for h in range(MLA_HEADS):
            s = s_next
            if h + 1 < MLA_HEADS:
                s_next = _dot_nt(k_ref[0, h + 1], q_ref[0, h + 1])
            if masked:
                s = jnp.where(mask, s, NEG_BIG)
            m_prev = m_sc[h]
            m_new = jnp.maximum(m_prev, jnp.max(s, axis=0, keepdims=True))
            p = jnp.exp2((s - m_new).astype(BF16))
            if masked:
                p = jnp.where(mask, p, jnp.zeros_like(p))
            alpha = jnp.exp2(m_prev - m_new)
            acc_sc[h] = alpha * acc_sc[h] + jnp.dot(vt_ref[0, h], p, preferred_element_type=F32)
            m_sc[h] = m_new

    @pl.when(full)
    def _():
        body(False)

    @pl.when(jnp.logical_not(full))
    def _():
        body(True)

    @pl.when(last_ref[t] == 1)
    def _():
        o_ref[0] = jnp.concatenate([(acc_sc[h, :V_DIM] / acc_sc[h, V_DIM:V_DIM + 1]).T for h in range(MLA_HEADS)],
                                   axis=1).astype(o_ref.dtype)


def _rot_half(w):
    half = w.shape[-1] // 2
    return jnp.concatenate([-w[..., half:], w[..., :half]], axis=-1)


def _rope_tables(pos0, S):
    half = QK_ROPE // 2
    inv_freq = ROPE_THETA ** (-jnp.arange(half, dtype=F32) / half)
    ang = (pos0 + jnp.arange(S, dtype=jnp.int32)).astype(F32)[:, None] * inv_freq[None, :]
    cos = jnp.concatenate([jnp.cos(ang), jnp.cos(ang)], axis=-1)[None]
    sin = jnp.concatenate([jnp.sin(ang), jnp.sin(ang)], axis=-1)[None]
    return cos, sin, jnp.tile(cos, (1, 1, MLA_HEADS)), jnp.tile(sin, (1, 1, MLA_HEADS))


def _mla_branch(x, pos0, past_lat, past_kr, p):
    b, S, _ = x.shape
    ts = min(512, S)
    cos_k, sin_k, cos_q, sin_q = _rope_tables(pos0, S)
    q, lat, kr = pl.pallas_call(
        _mla_prep_kernel,
        grid=(b, S // ts),
        in_specs=[
            pl.BlockSpec((1, ts, D_MODEL), lambda bi, si: (bi, si, 0)),
            _full((D_MODEL, MLA_PROJ_COLS)), _full((1, Q_LORA)), _full((1, KV_LORA)),
            _full((Q_LORA, MLA_HEADS * (QK_NOPE + 2 * QK_ROPE))),
            pl.BlockSpec((1, ts, MLA_HEADS * QK_ROPE), lambda bi, si: (0, si, 0)),
            pl.BlockSpec((1, ts, MLA_HEADS * QK_ROPE), lambda bi, si: (0, si, 0)),
            pl.BlockSpec((1, ts, QK_ROPE), lambda bi, si: (0, si, 0)),
            pl.BlockSpec((1, ts, QK_ROPE), lambda bi, si: (0, si, 0)),
        ],
        out_specs=[
            pl.BlockSpec((1, MLA_HEADS, ts, QK_DIM), lambda bi, si: (bi, 0, si, 0)),
            pl.BlockSpec((1, ts, KV_LORA), lambda bi, si: (bi, si, 0)),
            pl.BlockSpec((1, ts, QK_ROPE), lambda bi, si: (bi, si, 0)),
        ],
        out_shape=[
            jax.ShapeDtypeStruct((b, MLA_HEADS, S, QK_DIM), BF16),
            jax.ShapeDtypeStruct((b, S, KV_LORA), F32),
            jax.ShapeDtypeStruct((b, S, QK_ROPE), F32),
        ],
        compiler_params=_cparams("parallel", "parallel"),
        name="mla_prep",
    )(x, p["w_mla"], p["q_norm"], p["kv_norm"], p["w_q"], cos_q, sin_q, cos_k, sin_k)

    if past_lat is None:
        lat_all, kr_all, q_off, sk_valid = lat, kr, 0, S
        tk = min(512, S)
    else:
        past_len = past_lat.shape[1]
        sk_valid = past_len + S
        tk = -(-sk_valid // LANES) * LANES
        pad = tk - sk_valid
        lat_all = jnp.concatenate([past_lat, lat, jnp.zeros((b, pad, KV_LORA), F32)], axis=1)
        kr_all = jnp.concatenate([past_kr, kr, jnp.zeros((b, pad, QK_ROPE), F32)], axis=1)
        q_off = past_len
    Sk = lat_all.shape[1]
    tku = min(512, Sk) if Sk % min(512, Sk) == 0 else Sk
    k, vt = pl.pallas_call(
        _kv_up_kernel,
        grid=(b, Sk // tku),
        in_specs=[
            pl.BlockSpec((1, tku, KV_LORA), lambda bi, si: (bi, si, 0)),
            pl.BlockSpec((1, tku, QK_ROPE), lambda bi, si: (bi, si, 0)),
            _full((KV_LORA, MLA_HEADS * (QK_NOPE + V_DIM))),
        ],
        out_specs=[
            pl.BlockSpec((1, MLA_HEADS, tku, QK_DIM), lambda bi, si: (bi, 0, si, 0)),
            pl.BlockSpec((1, MLA_HEADS, VT_ROWS, tku), lambda bi, si: (bi, 0, 0, si)),
        ],
        out_shape=[
            jax.ShapeDtypeStruct((b, MLA_HEADS, Sk, QK_DIM), BF16),
            jax.ShapeDtypeStruct((b, MLA_HEADS, VT_ROWS, Sk), BF16),
        ],
        compiler_params=_cparams("parallel", "parallel"),
        name="mla_kv_up",
    )(lat_all, kr_all, p["w_ukv"])

    tq = min(512, S)
    nq, nk = S // tq, Sk // tk
    pairs = [(qi, kj) for qi in range(nq) for kj in range(_last_kv_block(qi, tq, tk, q_off, nk) + 1)]
    qi_tab = jnp.asarray([pq for pq, _ in pairs], jnp.int32)
    kj_tab = jnp.asarray([pk for _, pk in pairs], jnp.int32)
    last_tab = jnp.asarray([int(pk == _last_kv_block(pq, tq, tk, q_off, nk)) for pq, pk in pairs], jnp.int32)
    y = pl.pallas_call(
        functools.partial(_flash_kernel, tq=tq, tk=tk, q_off=q_off, sk_valid=sk_valid),
        grid_spec=pltpu.PrefetchScalarGridSpec(
            num_scalar_prefetch=3,
            grid=(b, len(pairs)),
            in_specs=[
                pl.BlockSpec((1, MLA_HEADS, tq, QK_DIM), lambda bi, t, qt, kt, lt: (bi, 0, qt[t], 0)),
                pl.BlockSpec((1, MLA_HEADS, tk, QK_DIM), lambda bi, t, qt, kt, lt: (bi, 0, kt[t], 0)),
                pl.BlockSpec((1, MLA_HEADS, VT_ROWS, tk), lambda bi, t, qt, kt, lt: (bi, 0, 0, kt[t])),
            ],
            out_specs=pl.BlockSpec((1, tq, MLA_HEADS * V_DIM), lambda bi, t, qt, kt, lt: (bi, qt[t], 0)),
            scratch_shapes=[
                pltpu.VMEM((MLA_HEADS, 1, tq), F32),
                pltpu.VMEM((MLA_HEADS, VT_ROWS, tq), F32),
            ],
        ),
        out_shape=jax.ShapeDtypeStruct((b, S, MLA_HEADS * V_DIM), BF16),
        compiler_params=_cparams("parallel", "arbitrary"),
        name="mla_flash",
    )(qi_tab, kj_tab, last_tab, q, k, vt)
    return y, lat, kr


def _rwkv_prep_kernel(x_ref, sp_ref, w_ref, mu_ref, w0_ref, a0_ref, kk_ref, ka_ref, rk_ref, ww2_ref, wa2_ref, wg2_ref,
                      hsum_ref, r_out, lw_out, k_out, v_out, kk_out, a_out, g_out, bon_out, sh_out, prev_sc):
    si = pl.program_id(1)

    @pl.when(si == 0)
    def _():
        prev_sc[...] = sp_ref[0]

    u = _dot(x_ref[0], w_ref[...])
    ts = u.shape[0]
    row = lax.broadcasted_iota(jnp.int32, u.shape, 0)
    shifted = jnp.where(row == 0, prev_sc[...], pltpu.roll(u, 1, axis=0))
    prev_sc[...] = u[ts - 1:ts, :]
    sh_out[0] = u[ts - 1:ts, :]
    m = u + (shifted - u) * mu_ref[...]
    r = m[:, :RWKV_DIM]
    k = m[:, RWKV_DIM:2 * RWKV_DIM]
    v = m[:, 2 * RWKV_DIM:3 * RWKV_DIM]
    o = 3 * RWKV_DIM
    wl = m[:, o:o + W_LORA]
    al = m[:, o + W_LORA:o + W_LORA + A_LORA]
    gl = m[:, o + W_LORA + A_LORA:]
    d = w0_ref[...] + _dot(jnp.tanh(wl), ww2_ref[...])
    lw_out[0] = -jnp.exp(-_softplus(-d) - 0.5)
    a = _sigmoid(a0_ref[...] + _dot(al, wa2_ref[...]))
    g_out[0] = _dot(_sigmoid(gl), wg2_ref[...])
    hsum = hsum_ref[...]
    kk = k * kk_ref[...]
    kk2 = kk * kk
    kk2_hi = kk2.astype(BF16)
    kk2_lo = (kk2 - kk2_hi.astype(F32)).astype(BF16)
    nrm = jnp.dot(kk2_hi, hsum, preferred_element_type=F32) + jnp.dot(kk2_lo, hsum, preferred_element_type=F32)
    kk_out[0] = kk * lax.rsqrt(nrm + 1e-12)
    kh = k * (1.0 + (a - 1.0) * ka_ref[...])
    rkr = r * kh * rk_ref[...]
    rkr_hi = rkr.astype(BF16)
    rkr_lo = (rkr - rkr_hi.astype(F32)).astype(BF16)
    bsum = jnp.dot(rkr_hi, hsum, preferred_element_type=F32) + jnp.dot(rkr_lo, hsum, preferred_element_type=F32)
    bon_out[0] = bsum * v
    r_out[0] = r
    k_out[0] = kh
    v_out[0] = v
    a_out[0] = a


def _rwkv_chunk_prepare(ins, c):
    r, lw, k, v, kk, a = (list(t) for t in zip(*ins))
    n = len(ins)
    L = r[0].shape[0]
    L2 = 2 * L
    lo = c["lane_lo"]

    def stack(xv):
        return jnp.concatenate([jnp.where(lo, xv, 0.0), jnp.where(lo, 0.0, xv)], axis=0)

    g = [_dot_exact_lhs(c["tri"], x) for x in lw]
    gl = [x[L - 1:L, :] for x in g]
    e_neg = [jnp.exp(-x) for x in g]
    at_s = [stack(-kk[i] * jnp.exp(g[i] - lw[i])) for i in range(n)]
    rt_s = [stack(r[i] * jnp.exp(g[i])) for i in range(n)]
    beta = [kk[i] * a[i] for i in range(n)]
    mm = []
    for i in range(n):
        bt = beta[i] * e_neg[i]
        kt = k[i] * e_neg[i]
        mm.append(_dot_nt(jnp.concatenate([at_s[i], rt_s[i]], axis=0), jnp.concatenate([bt, bt, kt, kt], axis=0)))
    nmat = [jnp.where(c["strict"], x[:L2, :L2], 0.0) for x in mm]
    mak = [jnp.where(c["strict"], x[:L2, L2:], 0.0) for x in mm]
    mrb = [jnp.where(c["incl"], x[L2:, :L2], 0.0) for x in mm]
    mrk = [jnp.where(c["incl"], x[L2:, L2:], 0.0) for x in mm]
    xinv = [c["eye2"] + x for x in nmat]
    pw = nmat
    for _ in range(int(math.log2(L)) - 1):
        pw = [_dot(x, x) for x in pw]
        xinv = [xinv[i] + _dot(xinv[i], pw[i]) for i in range(n)]
    vs = [stack(x) for x in v]
    w1 = [_dot(mak[i], vs[i]) for i in range(n)]
    au = [_dot(xinv[i], jnp.concatenate([at_s[i], w1[i]], axis=1)) for i in range(n)]
    ry = [_dot(mrb[i], au[i]) for i in range(n)]
    mv = [_dot(mrk[i], vs[i]) for i in range(n)]
    rh = [rt_s[i] + ry[i][:, :LANES] for i in range(n)]
    yh = [ry[i][:, LANES:] + mv[i] for i in range(n)]
    bs = [stack(beta[i] * jnp.exp(gl[i] - g[i])) for i in range(n)]
    ks = [stack(k[i] * jnp.exp(gl[i] - g[i])) for i in range(n)]
    pt_lr = [_dot_tn(bs[i], au[i][:, :LANES]) for i in range(n)]
    qt = [_dot_tn(jnp.concatenate([bs[i], ks[i]], axis=0), jnp.concatenate([au[i][:, LANES:], vs[i]], axis=0)) for i in range(n)]
    return [(rh[i], yh[i], pt_lr[i], qt[i], gl[i]) for i in range(n)]


def _rwkv_scan_kernel(r_ref, lw_ref, k_ref, v_ref, kk_ref, a_ref, g_ref, bon_ref, gng_ref, gnb_ref, z0_ref,
                      y_ref, zf_ref, z_sc, *, L, n_chunks):
    si = pl.program_id(1)

    @pl.when(si == 0)
    def _():
        z_sc[...] = z0_ref[0]

    L2 = 2 * L
    ri = lax.broadcasted_iota(jnp.int32, (L2, L2), 0)
    ci = lax.broadcasted_iota(jnp.int32, (L2, L2), 1)
    same = (ri // L) == (ci // L)
    rl = lax.broadcasted_iota(jnp.int32, (L, L), 0)
    cl = lax.broadcasted_iota(jnp.int32, (L, L), 1)
    r128 = lax.broadcasted_iota(jnp.int32, (LANES, LANES), 0)
    c128 = lax.broadcasted_iota(jnp.int32, (LANES, LANES), 1)
    consts = dict(
        tri=(cl <= rl).astype(BF16),
        strict=jnp.logical_and(same, (ci % L) < (ri % L)),
        incl=jnp.logical_and(same, (ci % L) <= (ri % L)),
        eye2=(ri == ci).astype(F32),
        lane_lo=lax.broadcasted_iota(jnp.int32, (L, LANES), 1) < RWKV_HEAD,
    )
    diag = r128 == c128
    gmean = jnp.where((r128 // RWKV_HEAD) == (c128 // RWKV_HEAD), 1.0 / RWKV_HEAD, 0.0).astype(BF16)
    npair = RWKV_HEADS // 2

    def head_mean(xv):
        hi = xv.astype(BF16)
        lo = (xv - hi.astype(F32)).astype(BF16)
        return jnp.dot(hi, gmean, preferred_element_type=F32) + jnp.dot(lo, gmean, preferred_element_type=F32)

    cpi = 2 if n_chunks % 2 == 0 else 1
    lanes = [slice(p * LANES, (p + 1) * LANES) for p in range(npair)]

    def chunk(ci_, carry):
        rows = [pl.ds(pl.multiple_of((ci_ * cpi + j) * L, L), L) for j in range(cpi)]
        ins = [(r_ref[0, rw, ln], lw_ref[0, rw, ln], k_ref[0, rw, ln], v_ref[0, rw, ln], kk_ref[0, rw, ln], a_ref[0, rw, ln])
               for rw in rows for ln in lanes]
        prep = _rwkv_chunk_prepare(ins, consts)
        z = [z_sc[p] for p in range(npair)]
        for j in range(cpi):
            pj = prep[j * npair:(j + 1) * npair]
            ys = [_dot(pj[p][0], z[p]) + pj[p][1] for p in range(npair)]
            zlr = [_dot(pj[p][2], z[p]) for p in range(npair)]
            y = [x[:L] + x[L:] for x in ys]
            mean = [head_mean(x) for x in y]
            yc = [y[p] - mean[p] for p in range(npair)]
            var = [head_mean(x * x) for x in yc]
            for p in range(npair):
                ln = lanes[p]
                gcol = jnp.sum(jnp.where(diag, jnp.broadcast_to(jnp.exp(pj[p][4]), (LANES, LANES)), 0.0), axis=1, keepdims=True)
                z[p] = gcol * z[p] + zlr[p] + pj[p][3]
                o = yc[p] * lax.rsqrt(var[p] + RWKV_GN_EPS) * gng_ref[:, ln] + gnb_ref[:, ln]
                y_ref[0, rows[j], ln] = ((o + bon_ref[0, rows[j], ln]) * g_ref[0, rows[j], ln]).astype(y_ref.dtype)
        for p in range(npair):
            z_sc[p] = z[p]
        return carry

    lax.fori_loop(0, n_chunks // cpi, chunk, 0)

    @pl.when(si == pl.num_programs(1) - 1)
    def _():
        zf_ref[0] = z_sc[...]


def _rwkv_branch(x, s0, shift_prev, p):
    b, S, _ = x.shape
    ts = min(512, S)
    tok = lambda w: pl.BlockSpec((1, ts, w), lambda bi, si: (bi, si, 0))
    row = lambda w: pl.BlockSpec((1, 1, w), lambda bi, si: (bi, 0, 0))
    outs = pl.pallas_call(
        _rwkv_prep_kernel,
        grid=(b, S // ts),
        in_specs=[tok(D_MODEL), row(RWKV_COLS), _full((D_MODEL, RWKV_COLS)), _full((1, RWKV_COLS))]
        + [_full((1, RWKV_DIM))] * 5
        + [_full((W_LORA, RWKV_DIM)), _full((A_LORA, RWKV_DIM)), _full((G_LORA, RWKV_DIM)), _full((RWKV_DIM, RWKV_DIM))],
        out_specs=[tok(RWKV_DIM)] * 8 + [row(RWKV_COLS)],
        out_shape=[jax.ShapeDtypeStruct((b, S, RWKV_DIM), F32)] * 8 + [jax.ShapeDtypeStruct((b, 1, RWKV_COLS), F32)],
        scratch_shapes=[pltpu.VMEM((1, RWKV_COLS), F32)],
        compiler_params=_cparams("parallel", "arbitrary"),
        name="rwkv_prep",
    )(x, shift_prev, p["w_rwkv"], p["mu"], p["w0"], p["a0"], p["k_k"], p["k_a"], p["r_k"], p["w_w2"], p["w_a2"], p["w_g2"],
      p["head_sum"])
    r, lw, kh, v, kk, a, g, bonus, shift_new = outs

    L = min(CHUNK, S)
    tb = min(4 * L, S)
    npair = RWKV_HEADS // 2
    zt = jnp.swapaxes(s0.astype(F32), 2, 3).reshape(b, npair, 2, RWKV_HEAD, RWKV_HEAD)
    z0 = jnp.einsum("bpikv,ij->bpikjv", zt, jnp.eye(2, dtype=F32)).reshape(b, npair, LANES, LANES)
    tokb = lambda: pl.BlockSpec((1, tb, RWKV_DIM), lambda bi, si: (bi, si, 0))
    zspec = pl.BlockSpec((1, npair, LANES, LANES), lambda bi, si: (bi, 0, 0, 0))
    y, zf = pl.pallas_call(
        functools.partial(_rwkv_scan_kernel, L=L, n_chunks=tb // L),
        grid=(b, S // tb),
        in_specs=[tokb() for _ in range(8)] + [_full((1, RWKV_DIM)), _full((1, RWKV_DIM)), zspec],
        out_specs=[tokb(), zspec],
        out_shape=[jax.ShapeDtypeStruct((b, S, RWKV_DIM), BF16), jax.ShapeDtypeStruct((b, npair, LANES, LANES), F32)],
        scratch_shapes=[pltpu.VMEM((npair, LANES, LANES), F32)],
        compiler_params=_cparams("parallel", "arbitrary"),
        name="rwkv_scan",
    )(r, lw, kh, v, kk, a, g, bonus, p["gn_g"], p["gn_b"], z0)
    zd = jnp.einsum("bpikiv->bpikv", zf.reshape(b, npair, 2, RWKV_HEAD, 2, RWKV_HEAD))
    s_new = jnp.swapaxes(zd.reshape(b, RWKV_HEADS, RWKV_HEAD, RWKV_HEAD), 2, 3)
    return y, s_new, shift_new


def _ssd_kernel(x_ref, w_ref, s0_ref, cp_ref, cw_ref, cb_ref, dtb_ref, a_ref, dsk_ref, ng_ref,
                y_ref, sf_ref, ct_ref, st_sc, tail_sc, *, L):
    si = pl.program_id(1)

    @pl.when(si == 0)
    def _():
        st_sc[...] = s0_ref[0]
        tail_sc[...] = cp_ref[0]

    u = _dot(x_ref[0], w_ref[...])
    z = u[:, :SSM_DIM]
    xbc = u[:, SSM_DIM:SSM_DIM + CONV_DIM]
    dtr = u[:, SSM_DIM + CONV_DIM:]
    tail = tail_sc[...]
    row = lax.broadcasted_iota(jnp.int32, xbc.shape, 0)
    sh1 = jnp.where(row == 0, tail[7:8], pltpu.roll(xbc, 1, axis=0))
    sh2 = jnp.where(row == 0, tail[6:7], pltpu.roll(sh1, 1, axis=0))
    sh3 = jnp.where(row == 0, tail[5:6], pltpu.roll(sh2, 1, axis=0))
    tail_sc[...] = xbc[L - 8:L]
    ct_ref[0] = xbc[L - 8:L]
    conv = cb_ref[...] + cw_ref[3:4] * xbc + cw_ref[2:3] * sh1 + cw_ref[1:2] * sh2 + cw_ref[0:1] * sh3
    act = conv * _sigmoid(conv)
    xs = act[:, :SSM_DIM]
    gw = SSM_STATE
    dt = _softplus(dtr + dtb_ref[...])
    a = dt * a_ref[...]
    rl = lax.broadcasted_iota(jnp.int32, (L, L), 0)
    cl = lax.broadcasted_iota(jnp.int32, (L, L), 1)
    causal = cl <= rl
    cum = _dot_exact_lhs(causal.astype(BF16), a)
    sel = (lax.broadcasted_iota(jnp.int32, (16, LANES), 0) == lax.broadcasted_iota(jnp.int32, (16, LANES), 1)).astype(BF16)
    cum_t = _dot_nt_exact_lhs(sel, cum)
    dt_t = _dot_nt_exact_lhs(sel, dt)
    ys = []
    hpg = SSM_HEADS // SSM_GROUPS
    for gi in range(SSM_GROUPS):
        bg = act[:, SSM_DIM + gi * gw:SSM_DIM + (gi + 1) * gw]
        cg = act[:, SSM_DIM + SSM_GROUPS * gw + gi * gw:SSM_DIM + SSM_GROUPS * gw + (gi + 1) * gw]
        cb = _dot_nt(cg, bg)
        for h in range(gi * hpg, (gi + 1) * hpg):
            xh = xs[:, h * SSM_HEADDIM:(h + 1) * SSM_HEADDIM]
            cc = cum[:, h:h + 1]
            seg = cc - cum_t[h:h + 1, :]
            dec = jnp.where(causal, jnp.exp(jnp.minimum(seg, 0.0)), 0.0)
            sc = cb * dec * dt_t[h:h + 1, :]
            st = st_sc[h]
            yh = _dot(sc, xh) + _dot_nt(cg, st) * jnp.exp(cc)
            clast = cum[L - 1:L, h:h + 1]
            wcol = jnp.exp(clast - cc) * dt[:, h:h + 1]
            st_sc[h] = st * jnp.exp(clast) + _dot_tn_split(xh * wcol, bg)
            ys.append(yh)
    y = jnp.concatenate(ys, axis=1) + dsk_ref[...] * xs
    y = y * (z * _sigmoid(z))
    gdim = SSM_DIM // SSM_GROUPS
    outs = []
    for gi in range(SSM_GROUPS):
        yg = y[:, gi * gdim:(gi + 1) * gdim]
        outs.append(yg * lax.rsqrt(jnp.mean(yg * yg, axis=-1, keepdims=True) + RMS_EPS))
    y_ref[0] = (jnp.concatenate(outs, axis=1) * ng_ref[...]).astype(y_ref.dtype)

    @pl.when(si == pl.num_programs(1) - 1)
    def _():
        sf_ref[0] = st_sc[...]


def _ssm_branch(x, s0, conv_prev, p):
    b, S, _ = x.shape
    L = min(256, S)
    cp = jnp.concatenate([jnp.zeros((b, 8 - (CONV_W - 1), CONV_DIM), F32), conv_prev.astype(F32)], axis=1)
    sspec = pl.BlockSpec((1, SSM_HEADS, SSM_HEADDIM, SSM_STATE), lambda bi, si: (bi, 0, 0, 0))
    cspec = pl.BlockSpec((1, 8, CONV_DIM), lambda bi, si: (bi, 0, 0))
    y, s_new, ctail = pl.pallas_call(
        functools.partial(_ssd_kernel, L=L),
        grid=(b, S // L),
        in_specs=[
            pl.BlockSpec((1, L, D_MODEL), lambda bi, si: (bi, si, 0)),
            _full((D_MODEL, SSM_PROJ_COLS)), sspec, cspec,
            _full((CONV_W, CONV_DIM)), _full((1, CONV_DIM)), _full((1, LANES)), _full((1, LANES)),
            _full((1, SSM_DIM)), _full((1, SSM_DIM)),
        ],
        out_specs=[pl.BlockSpec((1, L, SSM_DIM), lambda bi, si: (bi, si, 0)), sspec, cspec],
        out_shape=[
            jax.ShapeDtypeStruct((b, S, SSM_DIM), BF16),
            jax.ShapeDtypeStruct((b, SSM_HEADS, SSM_HEADDIM, SSM_STATE), F32),
            jax.ShapeDtypeStruct((b, 8, CONV_DIM), F32),
        ],
        scratch_shapes=[pltpu.VMEM((SSM_HEADS, SSM_HEADDIM, SSM_STATE), F32), pltpu.VMEM((8, CONV_DIM), F32)],
        compiler_params=_cparams("parallel", "arbitrary"),
        name="ssd",
    )(x, p["w_ssm"], s0.astype(F32), cp, p["conv_w"], p["conv_b"], p["dt_bias"], p["a_neg"], p["d_skip"], p["ssm_norm_g"])
    return y, s_new, ctail[:, 8 - (CONV_W - 1):]


def _merge_kernel(x_ref, ym_ref, yr_ref, ys_ref, wg_ref, wb_ref, wo_ref, g_ref, b_ref, rwh_ref, rwl_ref, rb_ref,
                  h_ref, hb_ref, ti_ref, tg_ref):
    x = x_ref[...]
    gates = _sigmoid(_dot(x, wg_ref[...]))
    mix = gates[:, :D_MODEL] * _dot(ym_ref[...], wb_ref[0])
    mix = mix + gates[:, D_MODEL:2 * D_MODEL] * _dot(yr_ref[...], wb_ref[1])
    mix = mix + gates[:, 2 * D_MODEL:] * _dot(ys_ref[...], wb_ref[2])
    h = _layer_norm(DEEPNORM_ALPHA * x + _dot(mix, wo_ref[...]), g_ref[...], b_ref[...])
    h_ref[...] = h
    hb_ref[...] = h.astype(BF16)
    h_hi = h.astype(BF16)
    h_lo = (h - h_hi.astype(F32)).astype(BF16)
    logits = (jnp.dot(h_hi, rwh_ref[...], preferred_element_type=F32) + jnp.dot(h_lo, rwh_ref[...], preferred_element_type=F32)
              + jnp.dot(h_hi, rwl_ref[...], preferred_element_type=F32) + rb_ref[...])
    lane = lax.broadcasted_iota(jnp.int32, logits.shape, 1)
    idx_out = jnp.zeros(logits.shape, jnp.int32)
    val_out = jnp.zeros(logits.shape, F32)
    top = None
    den = None
    for kth in range(TOP_K):
        mval = jnp.max(logits, axis=-1, keepdims=True)
        midx = jnp.min(jnp.where(logits == mval, lane, LANES), axis=-1, keepdims=True)
        if kth == 0:
            top = mval
            e = jnp.ones_like(mval)
            den = e
        else:
            e = jnp.exp(mval - top)
            den = den + e
        idx_out = jnp.where(lane == kth, midx, idx_out)
        val_out = jnp.where(lane == kth, e, val_out)
        logits = jnp.where(lane == midx, NEG_BIG * 2.0, logits)
    ti_ref[...] = idx_out
    tg_ref[...] = val_out / den


def _merge(x2, ym, yr, ys, p):
    T = x2.shape[0]
    tm = min(512, T)
    tok = lambda w: pl.BlockSpec((tm, w), lambda i: (i, 0))
    return pl.pallas_call(
        _merge_kernel,
        grid=(T // tm,),
        in_specs=[tok(D_MODEL), tok(BRANCH_DIM), tok(BRANCH_DIM), tok(BRANCH_DIM),
                  _full((D_MODEL, N_BRANCH * D_MODEL)), _full((N_BRANCH, BRANCH_DIM, D_MODEL)), _full((D_MODEL, D_MODEL)),
                  _full((1, D_MODEL)), _full((1, D_MODEL)), _full((D_MODEL, LANES)), _full((D_MODEL, LANES)), _full((1, LANES))],
        out_specs=[tok(D_MODEL), tok(D_MODEL), tok(LANES), tok(LANES)],
        out_shape=[jax.ShapeDtypeStruct((T, D_MODEL), F32), jax.ShapeDtypeStruct((T, D_MODEL), BF16),
                   jax.ShapeDtypeStruct((T, LANES), jnp.int32), jax.ShapeDtypeStruct((T, LANES), F32)],
        compiler_params=_cparams("parallel"),
        name="merge_router",
    )(x2, ym, yr, ys, p["w_gate"], p["w_branch"], p["w_out"], p["ln1_g"], p["ln1_b"], p["router_hi"], p["router_lo"], p["router_b"])


def _expert_kernel(be_ref, nu_ref, x_ref, wgu_ref, bgu_ref, wd_ref, bd_ref, o_ref, wgu_sc, wd_sc):
    i = pl.program_id(0)
    used = i < nu_ref[0]
    new_expert = jnp.logical_or(i == 0, be_ref[i] != be_ref[jnp.maximum(i - 1, 0)])

    @pl.when(jnp.logical_and(used, new_expert))
    def _():
        wgu_sc[...] = wgu_ref[0, 0].astype(BF16)
        wd_sc[...] = wd_ref[0, 0].astype(BF16)

    @pl.when(used)
    def _():
        hgu = jnp.dot(x_ref[...], wgu_sc[...], preferred_element_type=F32) + bgu_ref[0]
        gate = jnp.minimum(hgu[:, :D_FF], SWIGLU_LIMIT)
        up = jnp.clip(hgu[:, D_FF:], -SWIGLU_LIMIT, SWIGLU_LIMIT)
        hid = gate * _sigmoid(SWIGLU_ALPHA * gate) * (up + 1.0)
        o_ref[...] = (jnp.dot(hid.astype(BF16), wd_sc[...], preferred_element_type=F32) + bd_ref[0]).astype(o_ref.dtype)

    @pl.when(jnp.logical_not(used))
    def _():
        o_ref[...] = jnp.zeros(o_ref.shape, o_ref.dtype)


def _combine_kernel(y0_ref, y1_ref, y2_ref, y3_ref, tg_ref, h_ref, g_ref, b_ref, o_ref):
    tg = tg_ref[...]
    f = tg[:, 0:1] * y0_ref[...].astype(F32)
    for kth, y_ref in ((1, y1_ref), (2, y2_ref), (3, y3_ref)):
        f = f + tg[:, kth:kth + 1] * y_ref[...].astype(F32)
    o_ref[...] = _layer_norm(DEEPNORM_ALPHA * h_ref[...] + f, g_ref[...], b_ref[...])


def _moe(h, h_bf, top_i, top_g, p, layer):
    T = h.shape[0]
    n_assign = T * TOP_K
    flat_e = top_i[:, :TOP_K].reshape(-1)
    ids = jnp.arange(n_assign, dtype=jnp.int32)
    skey = jnp.sort(flat_e * n_assign + ids)
    sorted_e = skey // n_assign
    order = skey - sorted_e * n_assign
    experts = jnp.arange(N_EXPERTS, dtype=jnp.int32)
    counts = jnp.sum((flat_e[:, None] == experts[None, :]).astype(jnp.int32), axis=0)
    padded = (counts + MOE_ROWS - 1) // MOE_ROWS * MOE_ROWS
    pad_end = jnp.cumsum(padded)
    pad_start = pad_end - padded
    start = jnp.cumsum(counts) - counts
    n_blocks = -(-(n_assign + N_EXPERTS * (MOE_ROWS - 1)) // MOE_ROWS)
    n_rows = n_blocks * MOE_ROWS
    blk_row0 = jnp.arange(n_blocks, dtype=jnp.int32) * MOE_ROWS
    block_e = jnp.minimum(jnp.sum((pad_end[None, :] <= blk_row0[:, None]).astype(jnp.int32), axis=1), N_EXPERTS - 1)
    n_used = (pad_end[-1:] // MOE_ROWS).astype(jnp.int32)
    within = (blk_row0 - pad_start[block_e])[:, None] + jnp.arange(MOE_ROWS, dtype=jnp.int32)[None, :]
    valid = within < counts[block_e][:, None]
    src = jnp.clip(start[block_e][:, None] + within, 0, n_assign - 1)
    row_tok = jnp.where(valid, order[src.reshape(-1)].reshape(n_blocks, MOE_ROWS) // TOP_K, 0).reshape(-1)
    xb = h_bf[row_tok]

    yb = pl.pallas_call(
        _expert_kernel,
        grid_spec=pltpu.PrefetchScalarGridSpec(
            num_scalar_prefetch=2,
            grid=(n_blocks,),
            in_specs=[
                pl.BlockSpec((MOE_ROWS, D_MODEL), lambda i, be, nu: (i, 0)),
                pl.BlockSpec((1, 1, D_MODEL, 2 * D_FF), lambda i, be, nu: (layer, be[i], 0, 0)),
                pl.BlockSpec((1, 1, 2 * D_FF), lambda i, be, nu: (be[i], 0, 0)),
                pl.BlockSpec((1, 1, D_FF, D_MODEL), lambda i, be, nu: (layer, be[i], 0, 0)),
                pl.BlockSpec((1, 1, D_MODEL), lambda i, be, nu: (be[i], 0, 0)),
            ],
            out_specs=pl.BlockSpec((MOE_ROWS, D_MODEL), lambda i, be, nu: (i, 0)),
            scratch_shapes=[pltpu.VMEM((D_MODEL, 2 * D_FF), BF16), pltpu.VMEM((D_FF, D_MODEL), BF16)],
        ),
        out_shape=jax.ShapeDtypeStruct((n_rows, D_MODEL), BF16),
        compiler_params=_cparams("arbitrary"),
        name="moe_experts",
    )(block_e, n_used, xb, p["w_gu"], p["b_gu"], p["w_down"], p["b_down"])

    dest_sorted = pad_start[sorted_e] + (ids - start[sorted_e])
    _, dest = lax.sort((order, dest_sorted), num_keys=1)
    dest = dest.reshape(T, TOP_K)
    ygs = [yb[dest[:, kth]] for kth in range(TOP_K)]
    tm = min(512, T)
    tok = lambda w: pl.BlockSpec((tm, w), lambda i: (i, 0))
    return pl.pallas_call(
        _combine_kernel,
        grid=(T // tm,),
        in_specs=[tok(D_MODEL)] * TOP_K + [tok(LANES), tok(D_MODEL), _full((1, D_MODEL)), _full((1, D_MODEL))],
        out_specs=tok(D_MODEL),
        out_shape=jax.ShapeDtypeStruct((T, D_MODEL), F32),
        compiler_params=_cparams("parallel"),
        name="moe_combine_ln",
    )(*ygs, top_g, h, p["ln2_g"], p["ln2_b"])


def _prep_layer(l, w):
    w_in = w["w_in"][l]
    kr_cols = w_in[:, MLA_COLS - QK_ROPE:MLA_COLS]
    w_mla = jnp.concatenate([w_in[:, :MLA_COLS], _rot_half(kr_cols),
                             jnp.zeros((D_MODEL, MLA_PROJ_COLS - MLA_COLS - QK_ROPE), F32)], axis=1)
    wq = w["mla_w_uq"][l].reshape(Q_LORA, MLA_HEADS, QK_DIM)
    wq_rope = wq[:, :, QK_NOPE:]
    w_q = jnp.concatenate([wq[:, :, :QK_NOPE].reshape(Q_LORA, -1), wq_rope.reshape(Q_LORA, -1),
                           _rot_half(wq_rope).reshape(Q_LORA, -1)], axis=1)
    o_r = MLA_COLS
    o_s = o_r + RWKV_COLS
    o_g = o_s + SSM_COLS
    w_ssm = jnp.concatenate([w_in[:, o_s:o_s + SSM_DIM + CONV_DIM], w_in[:, o_s + SSM_DIM + CONV_DIM:o_g],
                             jnp.zeros((D_MODEL, LANES - SSM_HEADS), F32)], axis=1)
    pad8 = lambda v: jnp.concatenate([v.astype(F32), jnp.zeros((LANES - SSM_HEADS,), F32)])[None]
    row = lambda v: v.astype(F32)[None]
    hid = jnp.arange(RWKV_DIM) // RWKV_HEAD
    rw = jnp.concatenate([w["router_w"][l], jnp.zeros((D_MODEL, LANES - N_EXPERTS), F32)], axis=1)
    rw_hi = rw.astype(BF16)
    return dict(
        w_mla=w_mla.astype(BF16), q_norm=row(w["mla_q_norm"][l]), kv_norm=row(w["mla_kv_norm"][l]), w_q=w_q.astype(BF16),
        w_ukv=w["mla_w_ukv"][l].astype(BF16),
        w_rwkv=w_in[:, o_r:o_s].astype(BF16), mu=row(w["rwkv_mu"][l]), w0=row(w["rwkv_w0"][l]), a0=row(w["rwkv_a0"][l]),
        k_k=row(w["rwkv_k_k"][l]), k_a=row(w["rwkv_k_a"][l]), r_k=row(w["rwkv_r_k"][l]),
        w_w2=w["rwkv_w_w2"][l].astype(BF16), w_a2=w["rwkv_w_a2"][l].astype(BF16), w_g2=w["rwkv_w_g2"][l].astype(BF16),
        head_sum=(hid[:, None] == hid[None, :]).astype(BF16),
        gn_g=row(w["rwkv_gn_g"][l]), gn_b=row(w["rwkv_gn_b"][l]),
        w_ssm=w_ssm.astype(BF16), conv_w=w["ssm_conv_w"][l].astype(F32), conv_b=row(w["ssm_conv_b"][l]),
        dt_bias=pad8(w["ssm_dt_bias"][l]), a_neg=pad8(-jnp.exp(w["ssm_a_log"][l].astype(F32))),
        d_skip=row(jnp.repeat(w["ssm_d"][l], SSM_HEADDIM)), ssm_norm_g=row(w["ssm_norm_g"][l]),
        w_gate=w_in[:, o_g:].astype(BF16), w_branch=w["w_branch"][l].astype(BF16), w_out=w["w_out"][l].astype(BF16),
        ln1_g=row(w["ln1_g"][l]), ln1_b=row(w["ln1_b"][l]),
        router_hi=rw_hi, router_lo=(rw - rw_hi.astype(F32)).astype(BF16),
        router_b=jnp.concatenate([w["router_b"][l].astype(F32), jnp.full((LANES - N_EXPERTS,), NEG_BIG, F32)])[None],
        w_gu=w["expert_w_gu"], b_gu=w["expert_b_gu"][l].astype(F32)[:, None, :],
        w_down=w["expert_w_down"], b_down=w["expert_b_down"][l].astype(F32)[:, None, :], layer=l,
        ln2_g=row(w["ln2_g"][l]), ln2_b=row(w["ln2_b"][l]),
    )


def _trunk_layer(x, pos0, past_lat, past_kr, rwkv_s0, shift_prev, ssm_s0, conv_prev, p):
    b, S, _ = x.shape
    y_mla, lat, kr = _mla_branch(x, pos0, past_lat, past_kr, p)
    y_rwkv, rwkv_s, shift_new = _rwkv_branch(x, rwkv_s0, shift_prev, p)
    y_ssm, ssm_s, conv_new = _ssm_branch(x, ssm_s0, conv_prev, p)
    T = b * S
    flat = lambda t: t.reshape(T, t.shape[-1])
    h, h_bf, top_i, top_g = _merge(flat(x), flat(y_mla), flat(y_rwkv), flat(y_ssm), p)
    out = _moe(h, h_bf, top_i, top_g, p, p["layer"])
    return out.reshape(b, S, D_MODEL), (lat, kr, rwkv_s, shift_new, ssm_s, conv_new)


def kernel(x_prompt, x_sample, cache_mla_latent, cache_mla_krope, state_rwkv, state_rwkv_shift, state_ssm, state_ssm_conv, w_in, mla_q_norm, mla_kv_norm, mla_w_uq, mla_w_ukv, rwkv_mu, rwkv_w0, rwkv_w_w2, rwkv_a0, rwkv_w_a2, rwkv_w_g2, rwkv_k_k, rwkv_k_a, rwkv_r_k, rwkv_gn_g, rwkv_gn_b, ssm_conv_w, ssm_conv_b, ssm_dt_bias, ssm_a_log, ssm_d, ssm_norm_g, w_branch, w_out, ln1_g, ln1_b, router_w, router_b, expert_w_gu, expert_b_gu, expert_w_down, expert_b_down, ln2_g, ln2_b):
    w = dict(w_in=w_in, mla_q_norm=mla_q_norm, mla_kv_norm=mla_kv_norm, mla_w_uq=mla_w_uq, mla_w_ukv=mla_w_ukv,
             rwkv_mu=rwkv_mu, rwkv_w0=rwkv_w0, rwkv_w_w2=rwkv_w_w2, rwkv_a0=rwkv_a0, rwkv_w_a2=rwkv_w_a2, rwkv_w_g2=rwkv_w_g2,
             rwkv_k_k=rwkv_k_k, rwkv_k_a=rwkv_k_a, rwkv_r_k=rwkv_r_k, rwkv_gn_g=rwkv_gn_g, rwkv_gn_b=rwkv_gn_b,
             ssm_conv_w=ssm_conv_w, ssm_conv_b=ssm_conv_b, ssm_dt_bias=ssm_dt_bias, ssm_a_log=ssm_a_log, ssm_d=ssm_d,
             ssm_norm_g=ssm_norm_g, w_branch=w_branch, w_out=w_out, ln1_g=ln1_g, ln1_b=ln1_b, router_w=router_w,
             router_b=router_b, expert_w_gu=expert_w_gu, expert_b_gu=expert_b_gu, expert_w_down=expert_w_down,
             expert_b_down=expert_b_down, ln2_g=ln2_g, ln2_b=ln2_b)
    bp = x_prompt.shape[0]
    past_len = cache_mla_latent.shape[2]
    zero_rwkv = jnp.zeros((bp, RWKV_HEADS, RWKV_HEAD, RWKV_HEAD), F32)
    zero_shift = jnp.zeros((bp, 1, RWKV_COLS), F32)
    zero_ssm = jnp.zeros((bp, SSM_HEADS, SSM_HEADDIM, SSM_STATE), F32)
    zero_conv = jnp.zeros((bp, CONV_W - 1, CONV_DIM), F32)
    yp, ys = x_prompt, x_sample
    st_p = [[] for _ in range(6)]
    st_s = [[] for _ in range(6)]
    for l in range(DEPTH):
        p = _prep_layer(l, w)
        yp, new_p = _trunk_layer(yp, 0, None, None, zero_rwkv, zero_shift, zero_ssm, zero_conv, p)
        ys, new_s = _trunk_layer(ys, past_len, cache_mla_latent[l], cache_mla_krope[l], state_rwkv[l], state_rwkv_shift[l],
                                 state_ssm[l], state_ssm_conv[l], p)
        for i in range(6):
            st_p[i].append(new_p[i])
            st_s[i].append(new_s[i])
    outs_p = [jnp.stack(t, axis=0) for t in st_p]
    outs_s = [jnp.stack(t, axis=0) for t in st_s]
    return (yp, ys, *outs_p, *outs_s)
```

```python
import functools
import math

import jax
import jax.numpy as jnp
from jax import lax
from jax.experimental import pallas as pl
from jax.experimental.pallas import tpu as pltpu

F32 = jnp.float32
BF16 = jnp.bfloat16

D_MODEL = 1024
DEPTH = 4
CHUNK = 64
MLA_HEADS = 8
QK_NOPE = 64
QK_ROPE = 32
QK_DIM = QK_NOPE + QK_ROPE
V_DIM = 64
VT_ROWS = V_DIM + 16
Q_LORA = 384
KV_LORA = 256
ROPE_THETA = 10000.0
RWKV_HEADS = 8
RWKV_HEAD = 64
RWKV_DIM = RWKV_HEADS * RWKV_HEAD
W_LORA = 64
A_LORA = 64
G_LORA = 128
RWKV_GN_EPS = 64e-5
SSM_HEADS = 8
SSM_HEADDIM = 64
SSM_DIM = SSM_HEADS * SSM_HEADDIM
SSM_STATE = 64
SSM_GROUPS = 2
CONV_W = 4
CONV_DIM = SSM_DIM + 2 * SSM_GROUPS * SSM_STATE
N_BRANCH = 3
BRANCH_DIM = 512
N_EXPERTS = 32
TOP_K = 4
D_FF = 1024
SWIGLU_LIMIT = 7.0
SWIGLU_ALPHA = 1.702
DEEPNORM_ALPHA = (2.0 * DEPTH) ** 0.25
LN_EPS = 1e-5
RMS_EPS = 1e-6
MLA_COLS = Q_LORA + KV_LORA + QK_ROPE
RWKV_COLS = 3 * RWKV_DIM + W_LORA + A_LORA + G_LORA
SSM_COLS = SSM_DIM + CONV_DIM + SSM_HEADS

LANES = 128
MLA_PROJ_COLS = 768
SSM_PROJ_COLS = SSM_DIM + CONV_DIM + LANES
MOE_ROWS = 512
NEG_BIG = -1e30
EXP2_SCALE = QK_DIM ** -0.5 * math.log2(math.e)
FAST_MARGIN = 100.0
VMEM_LIMIT = 56 * 1024 * 1024


def _cparams(*sem):
    return pltpu.CompilerParams(dimension_semantics=sem, vmem_limit_bytes=VMEM_LIMIT)


def _dot(a, b):
    return jnp.dot(a.astype(BF16), b.astype(BF16), preferred_element_type=F32)


def _dot_nt(a, b):
    return lax.dot_general(a.astype(BF16), b.astype(BF16), (((1,), (1,)), ((), ())), preferred_element_type=F32)


def _dot_tn(a, b):
    return lax.dot_general(a.astype(BF16), b.astype(BF16), (((0,), (0,)), ((), ())), preferred_element_type=F32)


def _dot_tn_split(a, b):
    a_hi = a.astype(BF16)
    a_lo = (a - a_hi.astype(F32)).astype(BF16)
    b_hi = b.astype(BF16)
    b_lo = (b - b_hi.astype(F32)).astype(BF16)
    dn = (((0,), (0,)), ((), ()))
    return (lax.dot_general(a_hi, b_hi, dn, preferred_element_type=F32) + lax.dot_general(a_lo, b_hi, dn, preferred_element_type=F32)
            + lax.dot_general(a_hi, b_lo, dn, preferred_element_type=F32))


def _split3(x):
    hi = x.astype(BF16)
    r1 = x - hi.astype(F32)
    mid = r1.astype(BF16)
    lo = (r1 - mid.astype(F32)).astype(BF16)
    return hi, mid, lo


def _dot_exact_lhs(m, x):
    hi, mid, lo = _split3(x)
    return (jnp.dot(m, hi, preferred_element_type=F32) + jnp.dot(m, mid, preferred_element_type=F32)
            + jnp.dot(m, lo, preferred_element_type=F32))


def _dot_nt_exact_lhs(m, x):
    dn = (((1,), (1,)), ((), ()))
    hi, mid, lo = _split3(x)
    return (lax.dot_general(m, hi, dn, preferred_element_type=F32) + lax.dot_general(m, mid, dn, preferred_element_type=F32)
            + lax.dot_general(m, lo, dn, preferred_element_type=F32))


def _sigmoid(x):
    return 1.0 / (1.0 + jnp.exp(-x))


def _softplus(x):
    return jnp.maximum(x, 0.0) + jnp.log(1.0 + jnp.exp(-jnp.abs(x)))


def _rms(x, g):
    return x * lax.rsqrt(jnp.mean(x * x, axis=-1, keepdims=True) + RMS_EPS) * g


def _layer_norm(x, g, b):
    mu = jnp.mean(x, axis=-1, keepdims=True)
    xc = x - mu
    var = jnp.mean(xc * xc, axis=-1, keepdims=True)
    return xc * lax.rsqrt(var + LN_EPS) * g + b


def _full(shape):
    return pl.BlockSpec(shape, lambda *_: (0,) * len(shape))


def _mla_prep_kernel(x_ref, w_ref, qn_ref, kvn_ref, wq_ref, cq_ref, sq_ref, ck_ref, sk_ref, q_ref, lat_ref, kr_ref):
    u = _dot(x_ref[0], w_ref[...])
    c_q = u[:, :Q_LORA]
    c_kv = u[:, Q_LORA:Q_LORA + KV_LORA]
    kr = u[:, MLA_COLS - QK_ROPE:MLA_COLS]
    kr_rot = u[:, MLA_COLS:MLA_COLS + QK_ROPE]
    qall = _dot(_rms(c_q, qn_ref[...]), wq_ref[...])
    nope_w = MLA_HEADS * QK_NOPE
    rope_w = MLA_HEADS * QK_ROPE
    q_rope = qall[:, nope_w:nope_w + rope_w] * cq_ref[0] + qall[:, nope_w + rope_w:] * sq_ref[0]
    for h in range(MLA_HEADS):
        qh = jnp.concatenate([qall[:, h * QK_NOPE:(h + 1) * QK_NOPE], q_rope[:, h * QK_ROPE:(h + 1) * QK_ROPE]], axis=1)
        q_ref[0, h] = qh.astype(BF16)
    lat_ref[0] = _rms(c_kv, kvn_ref[...])
    kr_ref[0] = kr * ck_ref[0] + kr_rot * sk_ref[0]


def _kv_up_kernel(lat_ref, kr_ref, w_ref, k_ref, vt_ref, kn_ref):
    kv = _dot(lat_ref[0], w_ref[...])
    kr = kr_ref[0]
    hw = QK_NOPE + V_DIM
    ones_rows = (lax.broadcasted_iota(jnp.int32, (VT_ROWS - V_DIM, kv.shape[0]), 0) == 0).astype(F32)
    norms = []
    for h in range(MLA_HEADS):
        kh = jnp.concatenate([kv[:, h * hw:h * hw + QK_NOPE], kr], axis=1).astype(BF16)
        k_ref[0, h] = kh
        khf = kh.astype(F32)
        norms.append(jnp.broadcast_to(jnp.max(jnp.sum(khf * khf, axis=1, keepdims=True), axis=0, keepdims=True), (1, LANES)))
        vt_ref[0, h] = jnp.concatenate([kv[:, h * hw + QK_NOPE:(h + 1) * hw].T, ones_rows], axis=0).astype(BF16)
    kn_ref[0, 0] = jnp.concatenate(norms, axis=0)


def _last_kv_block(qi, tq, tk, q_off, nk):
    q_hi = q_off + qi * tq + tq - 1
    return min(nk - 1, ((q_hi // CHUNK) * CHUNK + CHUNK - 1) // tk)


def _flash_kernel(qi_ref, kj_ref, last_ref, kmax_ref, q_ref, k_ref, vt_ref, o_ref, m_sc, mrun_sc, qn_sc, acc_sc, ok_sc,
                  *, tq, tk, q_off, sk_valid, nk):
    bi = pl.program_id(0)
    t = pl.program_id(1)
    qi = qi_ref[t]
    kj = kj_ref[t]

    @pl.when(kj == 0)
    def _():
        m_sc[...] = jnp.full(m_sc.shape, NEG_BIG, F32)
        mrun_sc[...] = jnp.full(mrun_sc.shape, NEG_BIG, F32)
        acc_sc[...] = jnp.zeros(acc_sc.shape, F32)
        ones = jnp.ones((8, QK_DIM), BF16)
        for h in range(MLA_HEADS):
            qf = q_ref[0, h].astype(F32)
            qn_sc[h] = jnp.sqrt(1.02 * _dot_nt(ones, qf * qf)[0:1, :])

    q_lo = q_off + qi * tq
    k_lo = kj * tk
    k_hi = k_lo + tk - 1
    full = jnp.logical_and((k_hi // CHUNK) <= (q_lo // CHUNK), k_hi < sk_valid)
    fast = jnp.logical_and(kj > 0, ok_sc[0] == 1)
    kj_next = jnp.minimum(kj + 1, nk - 1)

    def body(masked, lagged):
        if masked:
            key = k_lo + lax.broadcasted_iota(jnp.int32, (tk, tq), 0)
            qry = q_lo + lax.broadcasted_iota(jnp.int32, (tk, tq), 1)
            mask = jnp.logical_and((key // CHUNK) <= (qry // CHUNK), key < sk_valid)
        excess = None
        s_next = _dot_nt(k_ref[0, 0], q_ref[0, 0])
        for h in range(MLA_HEADS):
            s = s_next
            if h + 1 < MLA_HEADS:
                s_next = _dot_nt(k_ref[0, h + 1], q_ref[0, h + 1])
            if masked:
                s = jnp.where(mask, s, NEG_BIG)
            m_ref_old = m_sc[h]
            m_run = mrun_sc[h]
            m_blk = jnp.max(s, axis=0, keepdims=True)
            m_run_new = jnp.maximum(m_run, m_blk)
            m_ref_new = m_run if lagged else m_run_new
            p = jnp.exp2((s - m_ref_new) * EXP2_SCALE).astype(BF16)
            if masked:
                p = jnp.where(mask, p, jnp.zeros_like(p))
            alpha = jnp.exp2((m_ref_old - m_ref_new) * EXP2_SCALE)
            acc_sc[h] = alpha * acc_sc[h] + jnp.dot(vt_ref[0, h], p, preferred_element_type=F32)
            m_sc[h] = m_ref_new
            mrun_sc[h] = m_run_new
            ex = (qn_sc[h] * kmax_ref[(bi * MLA_HEADS + h) * nk + kj_next] - m_run_new) * EXP2_SCALE
            excess = ex if excess is None else jnp.maximum(excess, ex)
        ok_sc[0] = (jnp.max(excess) <= FAST_MARGIN).astype(jnp.int32)

    for masked in (False, True):
        for lagged in (False, True):
            cond = jnp.logical_and(full != masked, fast == lagged)
            pl.when(cond)(functools.partial(body, masked, lagged))

    @pl.when(last_ref[t] == 1)
    def _():
        o_ref[0] = jnp.concatenate([(acc_sc[h, :V_DIM] / acc_sc[h, V_DIM:V_DIM + 1]).T for h in range(MLA_HEADS)],
                                   axis=1).astype(o_ref.dtype)


def _rot_half(w):
    half = w.shape[-1] // 2
    return jnp.concatenate([-w[..., half:], w[..., :half]], axis=-1)


def _rope_tables(pos0, S):
    half = QK_ROPE // 2
    inv_freq = ROPE_THETA ** (-jnp.arange(half, dtype=F32) / half)
    ang = (pos0 + jnp.arange(S, dtype=jnp.int32)).astype(F32)[:, None] * inv_freq[None, :]
    cos = jnp.concatenate([jnp.cos(ang), jnp.cos(ang)], axis=-1)[None]
    sin = jnp.concatenate([jnp.sin(ang), jnp.sin(ang)], axis=-1)[None]
    return cos, sin, jnp.tile(cos, (1, 1, MLA_HEADS)), jnp.tile(sin, (1, 1, MLA_HEADS))


def _mla_branch(x, pos0, past_lat, past_kr, p):
    b, S, _ = x.shape
    ts = min(512, S)
    cos_k, sin_k, cos_q, sin_q = _rope_tables(pos0, S)
    q, lat, kr = pl.pallas_call(
        _mla_prep_kernel,
        grid=(b, S // ts),
        in_specs=[
            pl.BlockSpec((1, ts, D_MODEL), lambda bi, si: (bi, si, 0)),
            _full((D_MODEL, MLA_PROJ_COLS)), _full((1, Q_LORA)), _full((1, KV_LORA)),
            _full((Q_LORA, MLA_HEADS * (QK_NOPE + 2 * QK_ROPE))),
            pl.BlockSpec((1, ts, MLA_HEADS * QK_ROPE), lambda bi, si: (0, si, 0)),
            pl.BlockSpec((1, ts, MLA_HEADS * QK_ROPE), lambda bi, si: (0, si, 0)),
            pl.BlockSpec((1, ts, QK_ROPE), lambda bi, si: (0, si, 0)),
            pl.BlockSpec((1, ts, QK_ROPE), lambda bi, si: (0, si, 0)),
        ],
        out_specs=[
            pl.BlockSpec((1, MLA_HEADS, ts, QK_DIM), lambda bi, si: (bi, 0, si, 0)),
            pl.BlockSpec((1, ts, KV_LORA), lambda bi, si: (bi, si, 0)),
            pl.BlockSpec((1, ts, QK_ROPE), lambda bi, si: (bi, si, 0)),
        ],
        out_shape=[
            jax.ShapeDtypeStruct((b, MLA_HEADS, S, QK_DIM), BF16),
            jax.ShapeDtypeStruct((b, S, KV_LORA), F32),
            jax.ShapeDtypeStruct((b, S, QK_ROPE), F32),
        ],
        compiler_params=_cparams("parallel", "parallel"),
        name="mla_prep",
    )(x, p["w_mla"], p["q_norm"], p["kv_norm"], p["w_q"], cos_q, sin_q, cos_k, sin_k)

    if past_lat is None:
        lat_all, kr_all, q_off, sk_valid = lat, kr, 0, S
        tk = min(512, S)
    else:
        past_len = past_lat.shape[1]
        sk_valid = past_len + S
        tk = -(-sk_valid // LANES) * LANES
        pad = tk - sk_valid
        lat_all = jnp.concatenate([past_lat, lat, jnp.zeros((b, pad, KV_LORA), F32)], axis=1)
        kr_all = jnp.concatenate([past_kr, kr, jnp.zeros((b, pad, QK_ROPE), F32)], axis=1)
        q_off = past_len
    Sk = lat_all.shape[1]
    tku = tk
    k, vt, ksq = pl.pallas_call(
        _kv_up_kernel,
        grid=(b, Sk // tku),
        in_specs=[
            pl.BlockSpec((1, tku, KV_LORA), lambda bi, si: (bi, si, 0)),
            pl.BlockSpec((1, tku, QK_ROPE), lambda bi, si: (bi, si, 0)),
            _full((KV_LORA, MLA_HEADS * (QK_NOPE + V_DIM))),
        ],
        out_specs=[
            pl.BlockSpec((1, MLA_HEADS, tku, QK_DIM), lambda bi, si: (bi, 0, si, 0)),
            pl.BlockSpec((1, MLA_HEADS, VT_ROWS, tku), lambda bi, si: (bi, 0, 0, si)),
            pl.BlockSpec((1, 1, MLA_HEADS, LANES), lambda bi, si: (bi, si, 0, 0)),
        ],
        out_shape=[
            jax.ShapeDtypeStruct((b, MLA_HEADS, Sk, QK_DIM), BF16),
            jax.ShapeDtypeStruct((b, MLA_HEADS, VT_ROWS, Sk), BF16),
            jax.ShapeDtypeStruct((b, Sk // tku, MLA_HEADS, LANES), F32),
        ],
        compiler_params=_cparams("parallel", "parallel"),
        name="mla_kv_up",
    )(lat_all, kr_all, p["w_ukv"])

    tq = min(512, S)
    nq, nk = S // tq, Sk // tk
    pairs = [(qi, kj) for qi in range(nq) for kj in range(_last_kv_block(qi, tq, tk, q_off, nk) + 1)]
    qi_tab = jnp.asarray([pq for pq, _ in pairs], jnp.int32)
    kj_tab = jnp.asarray([pk for _, pk in pairs], jnp.int32)
    last_tab = jnp.asarray([int(pk == _last_kv_block(pq, tq, tk, q_off, nk)) for pq, pk in pairs], jnp.int32)
    kmax = jnp.sqrt(jnp.swapaxes(ksq[:, :, :, 0], 1, 2)).reshape(-1)
    y = pl.pallas_call(
        functools.partial(_flash_kernel, tq=tq, tk=tk, q_off=q_off, sk_valid=sk_valid, nk=nk),
        grid_spec=pltpu.PrefetchScalarGridSpec(
            num_scalar_prefetch=4,
            grid=(b, len(pairs)),
            in_specs=[
                pl.BlockSpec((1, MLA_HEADS, tq, QK_DIM), lambda bi, t, qt, kt, lt, km: (bi, 0, qt[t], 0)),
                pl.BlockSpec((1, MLA_HEADS, tk, QK_DIM), lambda bi, t, qt, kt, lt, km: (bi, 0, kt[t], 0)),
                pl.BlockSpec((1, MLA_HEADS, VT_ROWS, tk), lambda bi, t, qt, kt, lt, km: (bi, 0, 0, kt[t])),
            ],
            out_specs=pl.BlockSpec((1, tq, MLA_HEADS * V_DIM), lambda bi, t, qt, kt, lt, km: (bi, qt[t], 0)),
            scratch_shapes=[
                pltpu.VMEM((MLA_HEADS, 1, tq), F32),
                pltpu.VMEM((MLA_HEADS, 1, tq), F32),
                pltpu.VMEM((MLA_HEADS, 1, tq), F32),
                pltpu.VMEM((MLA_HEADS, VT_ROWS, tq), F32),
                pltpu.SMEM((1,), jnp.int32),
            ],
        ),
        out_shape=jax.ShapeDtypeStruct((b, S, MLA_HEADS * V_DIM), BF16),
        compiler_params=_cparams("parallel", "arbitrary"),
        name="mla_flash",
    )(qi_tab, kj_tab, last_tab, kmax, q, k, vt)
    return y, lat, kr


def _rwkv_prep_kernel(x_ref, sp_ref, w_ref, mu_ref, w0_ref, a0_ref, kk_ref, ka_ref, rk_ref, ww2_ref, wa2_ref, wg2_ref,
                      hsum_ref, r_out, lw_out, k_out, v_out, kk_out, a_out, g_out, bon_out, sh_out, prev_sc):
    si = pl.program_id(1)

    @pl.when(si == 0)
    def _():
        prev_sc[...] = sp_ref[0]

    u = _dot(x_ref[0], w_ref[...])
    ts = u.shape[0]
    row = lax.broadcasted_iota(jnp.int32, u.shape, 0)
    shifted = jnp.where(row == 0, prev_sc[...], pltpu.roll(u, 1, axis=0))
    prev_sc[...] = u[ts - 1:ts, :]
    sh_out[0] = u[ts - 1:ts, :]
    m = u + (shifted - u) * mu_ref[...]
    r = m[:, :RWKV_DIM]
    k = m[:, RWKV_DIM:2 * RWKV_DIM]
    v = m[:, 2 * RWKV_DIM:3 * RWKV_DIM]
    o = 3 * RWKV_DIM
    wl = m[:, o:o + W_LORA]
    al = m[:, o + W_LORA:o + W_LORA + A_LORA]
    gl = m[:, o + W_LORA + A_LORA:]
    d = w0_ref[...] + _dot(jnp.tanh(wl), ww2_ref[...])
    lw_out[0] = -jnp.exp(-_softplus(-d) - 0.5)
    a = _sigmoid(a0_ref[...] + _dot(al, wa2_ref[...]))
    g_out[0] = _dot(_sigmoid(gl), wg2_ref[...])
    hsum = hsum_ref[...]
    kk = k * kk_ref[...]
    kk2 = kk * kk
    kk2_hi = kk2.astype(BF16)
    kk2_lo = (kk2 - kk2_hi.astype(F32)).astype(BF16)
    nrm = jnp.dot(kk2_hi, hsum, preferred_element_type=F32) + jnp.dot(kk2_lo, hsum, preferred_element_type=F32)
    kk_out[0] = kk * lax.rsqrt(nrm + 1e-12)
    kh = k * (1.0 + (a - 1.0) * ka_ref[...])
    rkr = r * kh * rk_ref[...]
    rkr_hi = rkr.astype(BF16)
    rkr_lo = (rkr - rkr_hi.astype(F32)).astype(BF16)
    bsum = jnp.dot(rkr_hi, hsum, preferred_element_type=F32) + jnp.dot(rkr_lo, hsum, preferred_element_type=F32)
    bon_out[0] = bsum * v
    r_out[0] = r
    k_out[0] = kh
    v_out[0] = v
    a_out[0] = a


def _rwkv_chunk_prepare(ins, c):
    r, lw, k, v, kk, a = (list(t) for t in zip(*ins))
    n = len(ins)
    L = r[0].shape[0]
    L2 = 2 * L
    lo = c["lane_lo"]

    def stack(xv):
        return jnp.concatenate([jnp.where(lo, xv, 0.0), jnp.where(lo, 0.0, xv)], axis=0)

    g = [_dot_exact_lhs(c["tri"], x) for x in lw]
    gl = [x[L - 1:L, :] for x in g]
    e_neg = [jnp.exp(-x) for x in g]
    at_s = [stack(-kk[i] * jnp.exp(g[i] - lw[i])) for i in range(n)]
    rt_s = [stack(r[i] * jnp.exp(g[i])) for i in range(n)]
    beta = [kk[i] * a[i] for i in range(n)]
    mm = []
    for i in range(n):
        bt = beta[i] * e_neg[i]
        kt = k[i] * e_neg[i]
        mm.append(_dot_nt(jnp.concatenate([at_s[i], rt_s[i]], axis=0), jnp.concatenate([bt, bt, kt, kt], axis=0)))
    nmat = [jnp.where(c["strict"], x[:L2, :L2], 0.0) for x in mm]
    mak = [jnp.where(c["strict"], x[:L2, L2:], 0.0) for x in mm]
    mrb = [jnp.where(c["incl"], x[L2:, :L2], 0.0) for x in mm]
    mrk = [jnp.where(c["incl"], x[L2:, L2:], 0.0) for x in mm]
    xinv = [c["eye2"] + x for x in nmat]
    pw = nmat
    for _ in range(int(math.log2(L)) - 1):
        pw = [_dot(x, x) for x in pw]
        xinv = [xinv[i] + _dot(xinv[i], pw[i]) for i in range(n)]
    vs = [stack(x) for x in v]
    w1 = [_dot(mak[i], vs[i]) for i in range(n)]
    au = [_dot(xinv[i], jnp.concatenate([at_s[i], w1[i]], axis=1)) for i in range(n)]
    ry = [_dot(mrb[i], au[i]) for i in range(n)]
    mv = [_dot(mrk[i], vs[i]) for i in range(n)]
    rh = [rt_s[i] + ry[i][:, :LANES] for i in range(n)]
    yh = [ry[i][:, LANES:] + mv[i] for i in range(n)]
    bs = [stack(beta[i] * jnp.exp(gl[i] - g[i])) for i in range(n)]
    ks = [stack(k[i] * jnp.exp(gl[i] - g[i])) for i in range(n)]
    pt_lr = [_dot_tn(bs[i], au[i][:, :LANES]) for i in range(n)]
    qt = [_dot_tn(jnp.concatenate([bs[i], ks[i]], axis=0), jnp.concatenate([au[i][:, LANES:], vs[i]], axis=0)) for i in range(n)]
    return [(rh[i], yh[i], pt_lr[i], qt[i], gl[i]) for i in range(n)]


def _rwkv_scan_kernel(r_ref, lw_ref, k_ref, v_ref, kk_ref, a_ref, g_ref, bon_ref, gng_ref, gnb_ref, z0_ref,
                      y_ref, zf_ref, z_sc, *, L, n_chunks):
    si = pl.program_id(1)

    @pl.when(si == 0)
    def _():
        z_sc[...] = z0_ref[0]

    L2 = 2 * L
    ri = lax.broadcasted_iota(jnp.int32, (L2, L2), 0)
    ci = lax.broadcasted_iota(jnp.int32, (L2, L2), 1)
    same = (ri // L) == (ci // L)
    rl = lax.broadcasted_iota(jnp.int32, (L, L), 0)
    cl = lax.broadcasted_iota(jnp.int32, (L, L), 1)
    r128 = lax.broadcasted_iota(jnp.int32, (LANES, LANES), 0)
    c128 = lax.broadcasted_iota(jnp.int32, (LANES, LANES), 1)
    consts = dict(
        tri=(cl <= rl).astype(BF16),
        strict=jnp.logical_and(same, (ci % L) < (ri % L)),
        incl=jnp.logical_and(same, (ci % L) <= (ri % L)),
        eye2=(ri == ci).astype(F32),
        lane_lo=lax.broadcasted_iota(jnp.int32, (L, LANES), 1) < RWKV_HEAD,
    )
    diag = r128 == c128
    gmean = jnp.where((r128 // RWKV_HEAD) == (c128 // RWKV_HEAD), 1.0 / RWKV_HEAD, 0.0).astype(BF16)
    npair = RWKV_HEADS // 2

    def head_mean(xv):
        hi = xv.astype(BF16)
        lo = (xv - hi.astype(F32)).astype(BF16)
        return jnp.dot(hi, gmean, preferred_element_type=F32) + jnp.dot(lo, gmean, preferred_element_type=F32)

    cpi = 2 if n_chunks % 2 == 0 else 1
    lanes = [slice(p * LANES, (p + 1) * LANES) for p in range(npair)]

    def chunk(ci_, carry):
        rows = [pl.ds(pl.multiple_of((ci_ * cpi + j) * L, L), L) for j in range(cpi)]
        ins = [(r_ref[0, rw, ln], lw_ref[0, rw, ln], k_ref[0, rw, ln], v_ref[0, rw, ln], kk_ref[0, rw, ln], a_ref[0, rw, ln])
               for rw in rows for ln in lanes]
        prep = _rwkv_chunk_prepare(ins, consts)
        z = [z_sc[p] for p in range(npair)]
        for j in range(cpi):
            pj = prep[j * npair:(j + 1) * npair]
            ys = [_dot(pj[p][0], z[p]) + pj[p][1] for p in range(npair)]
            zlr = [_dot(pj[p][2], z[p]) for p in range(npair)]
            y = [x[:L] + x[L:] for x in ys]
            mean = [head_mean(x) for x in y]
            yc = [y[p] - mean[p] for p in range(npair)]
            var = [head_mean(x * x) for x in yc]
            for p in range(npair):
                ln = lanes[p]
                gcol = jnp.sum(jnp.where(diag, jnp.broadcast_to(jnp.exp(pj[p][4]), (LANES, LANES)), 0.0), axis=1, keepdims=True)
                z[p] = gcol * z[p] + zlr[p] + pj[p][3]
                o = yc[p] * lax.rsqrt(var[p] + RWKV_GN_EPS) * gng_ref[:, ln] + gnb_ref[:, ln]
                y_ref[0, rows[j], ln] = ((o + bon_ref[0, rows[j], ln]) * g_ref[0, rows[j], ln]).astype(y_ref.dtype)
        for p in range(npair):
            z_sc[p] = z[p]
        return carry

    lax.fori_loop(0, n_chunks // cpi, chunk, 0)

    @pl.when(si == pl.num_programs(1) - 1)
    def _():
        zf_ref[0] = z_sc[...]


def _rwkv_branch(x, s0, shift_prev, p):
    b, S, _ = x.shape
    ts = min(512, S)
    tok = lambda w: pl.BlockSpec((1, ts, w), lambda bi, si: (bi, si, 0))
    row = lambda w: pl.BlockSpec((1, 1, w), lambda bi, si: (bi, 0, 0))
    outs = pl.pallas_call(
        _rwkv_prep_kernel,
        grid=(b, S // ts),
        in_specs=[tok(D_MODEL), row(RWKV_COLS), _full((D_MODEL, RWKV_COLS)), _full((1, RWKV_COLS))]
        + [_full((1, RWKV_DIM))] * 5
        + [_full((W_LORA, RWKV_DIM)), _full((A_LORA, RWKV_DIM)), _full((G_LORA, RWKV_DIM)), _full((RWKV_DIM, RWKV_DIM))],
        out_specs=[tok(RWKV_DIM)] * 8 + [row(RWKV_COLS)],
        out_shape=[jax.ShapeDtypeStruct((b, S, RWKV_DIM), F32)] * 8 + [jax.ShapeDtypeStruct((b, 1, RWKV_COLS), F32)],
        scratch_shapes=[pltpu.VMEM((1, RWKV_COLS), F32)],
        compiler_params=_cparams("parallel", "arbitrary"),
        name="rwkv_prep",
    )(x, shift_prev, p["w_rwkv"], p["mu"], p["w0"], p["a0"], p["k_k"], p["k_a"], p["r_k"], p["w_w2"], p["w_a2"], p["w_g2"],
      p["head_sum"])
    r, lw, kh, v, kk, a, g, bonus, shift_new = outs

    L = min(CHUNK, S)
    tb = min(4 * L, S)
    npair = RWKV_HEADS // 2
    zt = jnp.swapaxes(s0.astype(F32), 2, 3).reshape(b, npair, 2, RWKV_HEAD, RWKV_HEAD)
    z0 = jnp.einsum("bpikv,ij->bpikjv", zt, jnp.eye(2, dtype=F32)).reshape(b, npair, LANES, LANES)
    tokb = lambda: pl.BlockSpec((1, tb, RWKV_DIM), lambda bi, si: (bi, si, 0))
    zspec = pl.BlockSpec((1, npair, LANES, LANES), lambda bi, si: (bi, 0, 0, 0))
    y, zf = pl.pallas_call(
        functools.partial(_rwkv_scan_kernel, L=L, n_chunks=tb // L),
        grid=(b, S // tb),
        in_specs=[tokb() for _ in range(8)] + [_full((1, RWKV_DIM)), _full((1, RWKV_DIM)), zspec],
        out_specs=[tokb(), zspec],
        out_shape=[jax.ShapeDtypeStruct((b, S, RWKV_DIM), BF16), jax.ShapeDtypeStruct((b, npair, LANES, LANES), F32)],
        scratch_shapes=[pltpu.VMEM((npair, LANES, LANES), F32)],
        compiler_params=_cparams("parallel", "arbitrary"),
        name="rwkv_scan",
    )(r, lw, kh, v, kk, a, g, bonus, p["gn_g"], p["gn_b"], z0)
    zd = jnp.einsum("bpikiv->bpikv", zf.reshape(b, npair, 2, RWKV_HEAD, 2, RWKV_HEAD))
    s_new = jnp.swapaxes(zd.reshape(b, RWKV_HEADS, RWKV_HEAD, RWKV_HEAD), 2, 3)
    return y, s_new, shift_new


def _ssd_kernel(x_ref, w_ref, s0_ref, cp_ref, cw_ref, cb_ref, dtb_ref, a_ref, dsk_ref, ng_ref,
                y_ref, sf_ref, ct_ref, st_sc, tail_sc, *, L):
    si = pl.program_id(1)

    @pl.when(si == 0)
    def _():
        st_sc[...] = s0_ref[0]
        tail_sc[...] = cp_ref[0]

    u = _dot(x_ref[0], w_ref[...])
    z = u[:, :SSM_DIM]
    xbc = u[:, SSM_DIM:SSM_DIM + CONV_DIM]
    dtr = u[:, SSM_DIM + CONV_DIM:]
    tail = tail_sc[...]
    row = lax.broadcasted_iota(jnp.int32, xbc.shape, 0)
    sh1 = jnp.where(row == 0, tail[7:8], pltpu.roll(xbc, 1, axis=0))
    sh2 = jnp.where(row == 0, tail[6:7], pltpu.roll(sh1, 1, axis=0))
    sh3 = jnp.where(row == 0, tail[5:6], pltpu.roll(sh2, 1, axis=0))
    tail_sc[...] = xbc[L - 8:L]
    ct_ref[0] = xbc[L - 8:L]
    conv = cb_ref[...] + cw_ref[3:4] * xbc + cw_ref[2:3] * sh1 + cw_ref[1:2] * sh2 + cw_ref[0:1] * sh3
    act = conv * _sigmoid(conv)
    xs = act[:, :SSM_DIM]
    gw = SSM_STATE
    dt = _softplus(dtr + dtb_ref[...])
    a = dt * a_ref[...]
    rl = lax.broadcasted_iota(jnp.int32, (L, L), 0)
    cl = lax.broadcasted_iota(jnp.int32, (L, L), 1)
    causal = cl <= rl
    cum = _dot_exact_lhs(causal.astype(BF16), a)
    sel = (lax.broadcasted_iota(jnp.int32, (16, LANES), 0) == lax.broadcasted_iota(jnp.int32, (16, LANES), 1)).astype(BF16)
    cum_t = _dot_nt_exact_lhs(sel, cum)
    dt_t = _dot_nt_exact_lhs(sel, dt)
    ys = []
    hpg = SSM_HEADS // SSM_GROUPS
    for gi in range(SSM_GROUPS):
        bg = act[:, SSM_DIM + gi * gw:SSM_DIM + (gi + 1) * gw]
        cg = act[:, SSM_DIM + SSM_GROUPS * gw + gi * gw:SSM_DIM + SSM_GROUPS * gw + (gi + 1) * gw]
        cb = _dot_nt(cg, bg)
        for h in range(gi * hpg, (gi + 1) * hpg):
            xh = xs[:, h * SSM_HEADDIM:(h + 1) * SSM_HEADDIM]
            cc = cum[:, h:h + 1]
            seg = cc - cum_t[h:h + 1, :]
            dec = jnp.where(causal, jnp.exp(jnp.minimum(seg, 0.0)), 0.0)
            sc = cb * dec * dt_t[h:h + 1, :]
            st = st_sc[h]
            yh = _dot(sc, xh) + _dot_nt(cg, st) * jnp.exp(cc)
            clast = cum[L - 1:L, h:h + 1]
            wcol = jnp.exp(clast - cc) * dt[:, h:h + 1]
            st_sc[h] = st * jnp.exp(clast) + _dot_tn_split(xh * wcol, bg)
            ys.append(yh)
    y = jnp.concatenate(ys, axis=1) + dsk_ref[...] * xs
    y = y * (z * _sigmoid(z))
    gdim = SSM_DIM // SSM_GROUPS
    outs = []
    for gi in range(SSM_GROUPS):
        yg = y[:, gi * gdim:(gi + 1) * gdim]
        outs.append(yg * lax.rsqrt(jnp.mean(yg * yg, axis=-1, keepdims=True) + RMS_EPS))
    y_ref[0] = (jnp.concatenate(outs, axis=1) * ng_ref[...]).astype(y_ref.dtype)

    @pl.when(si == pl.num_programs(1) - 1)
    def _():
        sf_ref[0] = st_sc[...]


def _ssm_branch(x, s0, conv_prev, p):
    b, S, _ = x.shape
    L = min(256, S)
    cp = jnp.concatenate([jnp.zeros((b, 8 - (CONV_W - 1), CONV_DIM), F32), conv_prev.astype(F32)], axis=1)
    sspec = pl.BlockSpec((1, SSM_HEADS, SSM_HEADDIM, SSM_STATE), lambda bi, si: (bi, 0, 0, 0))
    cspec = pl.BlockSpec((1, 8, CONV_DIM), lambda bi, si: (bi, 0, 0))
    y, s_new, ctail = pl.pallas_call(
        functools.partial(_ssd_kernel, L=L),
        grid=(b, S // L),
        in_specs=[
            pl.BlockSpec((1, L, D_MODEL), lambda bi, si: (bi, si, 0)),
            _full((D_MODEL, SSM_PROJ_COLS)), sspec, cspec,
            _full((CONV_W, CONV_DIM)), _full((1, CONV_DIM)), _full((1, LANES)), _full((1, LANES)),
            _full((1, SSM_DIM)), _full((1, SSM_DIM)),
        ],
        out_specs=[pl.BlockSpec((1, L, SSM_DIM), lambda bi, si: (bi, si, 0)), sspec, cspec],
        out_shape=[
            jax.ShapeDtypeStruct((b, S, SSM_DIM), BF16),
            jax.ShapeDtypeStruct((b, SSM_HEADS, SSM_HEADDIM, SSM_STATE), F32),
            jax.ShapeDtypeStruct((b, 8, CONV_DIM), F32),
        ],
        scratch_shapes=[pltpu.VMEM((SSM_HEADS, SSM_HEADDIM, SSM_STATE), F32), pltpu.VMEM((8, CONV_DIM), F32)],
        compiler_params=_cparams("parallel", "arbitrary"),
        name="ssd",
    )(x, p["w_ssm"], s0.astype(F32), cp, p["conv_w"], p["conv_b"], p["dt_bias"], p["a_neg"], p["d_skip"], p["ssm_norm_g"])
    return y, s_new, ctail[:, 8 - (CONV_W - 1):]


def _merge_kernel(x_ref, ym_ref, yr_ref, ys_ref, wg_ref, wb_ref, wo_ref, g_ref, b_ref, rwh_ref, rwl_ref, rb_ref,
                  h_ref, hb_ref, ti_ref, tg_ref):
    x = x_ref[...]
    gates = _sigmoid(_dot(x, wg_ref[...]))
    mix = gates[:, :D_MODEL] * _dot(ym_ref[...], wb_ref[0])
    mix = mix + gates[:, D_MODEL:2 * D_MODEL] * _dot(yr_ref[...], wb_ref[1])
    mix = mix + gates[:, 2 * D_MODEL:] * _dot(ys_ref[...], wb_ref[2])
    h = _layer_norm(DEEPNORM_ALPHA * x + _dot(mix, wo_ref[...]), g_ref[...], b_ref[...])
    h_ref[...] = h
    hb_ref[...] = h.astype(BF16)
    h_hi = h.astype(BF16)
    h_lo = (h - h_hi.astype(F32)).astype(BF16)
    logits = (jnp.dot(h_hi, rwh_ref[...], preferred_element_type=F32) + jnp.dot(h_lo, rwh_ref[...], preferred_element_type=F32)
              + jnp.dot(h_hi, rwl_ref[...], preferred_element_type=F32) + rb_ref[...])
    lane = lax.broadcasted_iota(jnp.int32, logits.shape, 1)
    idx_out = jnp.zeros(logits.shape, jnp.int32)
    val_out = jnp.zeros(logits.shape, F32)
    top = None
    den = None
    for kth in range(TOP_K):
        mval = jnp.max(logits, axis=-1, keepdims=True)
        midx = jnp.min(jnp.where(logits == mval, lane, LANES), axis=-1, keepdims=True)
        if kth == 0:
            top = mval
            e = jnp.ones_like(mval)
            den = e
        else:
            e = jnp.exp(mval - top)
            den = den + e
        idx_out = jnp.where(lane == kth, midx, idx_out)
        val_out = jnp.where(lane == kth, e, val_out)
        logits = jnp.where(lane == midx, NEG_BIG * 2.0, logits)
    ti_ref[...] = idx_out
    tg_ref[...] = val_out / den


def _merge(x2, ym, yr, ys, p):
    T = x2.shape[0]
    tm = min(512, T)
    tok = lambda w: pl.BlockSpec((tm, w), lambda i: (i, 0))
    return pl.pallas_call(
        _merge_kernel,
        grid=(T // tm,),
        in_specs=[tok(D_MODEL), tok(BRANCH_DIM), tok(BRANCH_DIM), tok(BRANCH_DIM),
                  _full((D_MODEL, N_BRANCH * D_MODEL)), _full((N_BRANCH, BRANCH_DIM, D_MODEL)), _full((D_MODEL, D_MODEL)),
                  _full((1, D_MODEL)), _full((1, D_MODEL)), _full((D_MODEL, LANES)), _full((D_MODEL, LANES)), _full((1, LANES))],
        out_specs=[tok(D_MODEL), tok(D_MODEL), tok(LANES), tok(LANES)],
        out_shape=[jax.ShapeDtypeStruct((T, D_MODEL), F32), jax.ShapeDtypeStruct((T, D_MODEL), BF16),
                   jax.ShapeDtypeStruct((T, LANES), jnp.int32), jax.ShapeDtypeStruct((T, LANES), F32)],
        compiler_params=_cparams("parallel"),
        name="merge_router",
    )(x2, ym, yr, ys, p["w_gate"], p["w_branch"], p["w_out"], p["ln1_g"], p["ln1_b"], p["router_hi"], p["router_lo"], p["router_b"])


def _expert_kernel(be_ref, nu_ref, x_ref, wgu_ref, bgu_ref, wd_ref, bd_ref, o_ref, wgu_sc, wd_sc):
    i = pl.program_id(0)
    used = i < nu_ref[0]
    new_expert = jnp.logical_or(i == 0, be_ref[i] != be_ref[jnp.maximum(i - 1, 0)])

    @pl.when(jnp.logical_and(used, new_expert))
    def _():
        wgu_sc[...] = wgu_ref[0, 0].astype(BF16)
        wd_sc[...] = wd_ref[0, 0].astype(BF16)

    @pl.when(used)
    def _():
        hgu = jnp.dot(x_ref[...], wgu_sc[...], preferred_element_type=F32) + bgu_ref[0]
        gate = jnp.minimum(hgu[:, :D_FF], SWIGLU_LIMIT)
        up = jnp.clip(hgu[:, D_FF:], -SWIGLU_LIMIT, SWIGLU_LIMIT)
        hid = gate * _sigmoid(SWIGLU_ALPHA * gate) * (up + 1.0)
        o_ref[...] = (jnp.dot(hid.astype(BF16), wd_sc[...], preferred_element_type=F32) + bd_ref[0]).astype(o_ref.dtype)

    @pl.when(jnp.logical_not(used))
    def _():
        o_ref[...] = jnp.zeros(o_ref.shape, o_ref.dtype)


def _combine_kernel(y0_ref, y1_ref, y2_ref, y3_ref, tg_ref, h_ref, g_ref, b_ref, o_ref):
    tg = tg_ref[...]
    f = tg[:, 0:1] * y0_ref[...].astype(F32)
    for kth, y_ref in ((1, y1_ref), (2, y2_ref), (3, y3_ref)):
        f = f + tg[:, kth:kth + 1] * y_ref[...].astype(F32)
    o_ref[...] = _layer_norm(DEEPNORM_ALPHA * h_ref[...] + f, g_ref[...], b_ref[...])


def _moe(h, h_bf, top_i, top_g, p, layer):
    T = h.shape[0]
    n_assign = T * TOP_K
    flat_e = top_i[:, :TOP_K].reshape(-1)
    ids = jnp.arange(n_assign, dtype=jnp.int32)
    skey = jnp.sort(flat_e * n_assign + ids)
    sorted_e = skey // n_assign
    order = skey - sorted_e * n_assign
    experts = jnp.arange(N_EXPERTS, dtype=jnp.int32)
    counts = jnp.sum((flat_e[:, None] == experts[None, :]).astype(jnp.int32), axis=0)
    padded = (counts + MOE_ROWS - 1) // MOE_ROWS * MOE_ROWS
    pad_end = jnp.cumsum(padded)
    pad_start = pad_end - padded
    start = jnp.cumsum(counts) - counts
    n_blocks = -(-(n_assign + N_EXPERTS * (MOE_ROWS - 1)) // MOE_ROWS)
    n_rows = n_blocks * MOE_ROWS
    blk_row0 = jnp.arange(n_blocks, dtype=jnp.int32) * MOE_ROWS
    block_e = jnp.minimum(jnp.sum((pad_end[None, :] <= blk_row0[:, None]).astype(jnp.int32), axis=1), N_EXPERTS - 1)
    n_used = (pad_end[-1:] // MOE_ROWS).astype(jnp.int32)
    within = (blk_row0 - pad_start[block_e])[:, None] + jnp.arange(MOE_ROWS, dtype=jnp.int32)[None, :]
    valid = within < counts[block_e][:, None]
    src = jnp.clip(start[block_e][:, None] + within, 0, n_assign - 1)
    own_row = (blk_row0[:, None] + jnp.arange(MOE_ROWS, dtype=jnp.int32)[None, :]) % T
    row_tok = jnp.where(valid, order[src.reshape(-1)].reshape(n_blocks, MOE_ROWS) // TOP_K, own_row).reshape(-1)
    xb = h_bf[row_tok]

    yb = pl.pallas_call(
        _expert_kernel,
        grid_spec=pltpu.PrefetchScalarGridSpec(
            num_scalar_prefetch=2,
            grid=(n_blocks,),
            in_specs=[
                pl.BlockSpec((MOE_ROWS, D_MODEL), lambda i, be, nu: (i, 0)),
                pl.BlockSpec((1, 1, D_MODEL, 2 * D_FF), lambda i, be, nu: (layer, be[i], 0, 0)),
                pl.BlockSpec((1, 1, 2 * D_FF), lambda i, be, nu: (be[i], 0, 0)),
                pl.BlockSpec((1, 1, D_FF, D_MODEL), lambda i, be, nu: (layer, be[i], 0, 0)),
                pl.BlockSpec((1, 1, D_MODEL), lambda i, be, nu: (be[i], 0, 0)),
            ],
            out_specs=pl.BlockSpec((MOE_ROWS, D_MODEL), lambda i, be, nu: (i, 0)),
            scratch_shapes=[pltpu.VMEM((D_MODEL, 2 * D_FF), BF16), pltpu.VMEM((D_FF, D_MODEL), BF16)],
        ),
        out_shape=jax.ShapeDtypeStruct((n_rows, D_MODEL), BF16),
        compiler_params=_cparams("arbitrary"),
        name="moe_experts",
    )(block_e, n_used, xb, p["w_gu"], p["b_gu"], p["w_down"], p["b_down"])

    dest_sorted = pad_start[sorted_e] + (ids - start[sorted_e])
    _, dest = lax.sort((order, dest_sorted), num_keys=1)
    dest = dest.reshape(T, TOP_K)
    ygs = [yb[dest[:, kth]] for kth in range(TOP_K)]
    tm = min(512, T)
    tok = lambda w: pl.BlockSpec((tm, w), lambda i: (i, 0))
    return pl.pallas_call(
        _combine_kernel,
        grid=(T // tm,),
        in_specs=[tok(D_MODEL)] * TOP_K + [tok(LANES), tok(D_MODEL), _full((1, D_MODEL)), _full((1, D_MODEL))],
        out_specs=tok(D_MODEL),
        out_shape=jax.ShapeDtypeStruct((T, D_MODEL), F32),
        compiler_params=_cparams("parallel"),
        name="moe_combine_ln",
    )(*ygs, top_g, h, p["ln2_g"], p["ln2_b"])


def _prep_layer(l, w):
    w_in = w["w_in"][l]
    kr_cols = w_in[:, MLA_COLS - QK_ROPE:MLA_COLS]
    w_mla = jnp.concatenate([w_in[:, :MLA_COLS], _rot_half(kr_cols),
                             jnp.zeros((D_MODEL, MLA_PROJ_COLS - MLA_COLS - QK_ROPE), F32)], axis=1)
    wq = w["mla_w_uq"][l].reshape(Q_LORA, MLA_HEADS, QK_DIM)
    wq_rope = wq[:, :, QK_NOPE:]
    w_q = jnp.concatenate([wq[:, :, :QK_NOPE].reshape(Q_LORA, -1), wq_rope.reshape(Q_LORA, -1),
                           _rot_half(wq_rope).reshape(Q_LORA, -1)], axis=1)
    o_r = MLA_COLS
    o_s = o_r + RWKV_COLS
    o_g = o_s + SSM_COLS
    w_ssm = jnp.concatenate([w_in[:, o_s:o_s + SSM_DIM + CONV_DIM], w_in[:, o_s + SSM_DIM + CONV_DIM:o_g],
                             jnp.zeros((D_MODEL, LANES - SSM_HEADS), F32)], axis=1)
    pad8 = lambda v: jnp.concatenate([v.astype(F32), jnp.zeros((LANES - SSM_HEADS,), F32)])[None]
    row = lambda v: v.astype(F32)[None]
    hid = jnp.arange(RWKV_DIM) // RWKV_HEAD
    rw = jnp.concatenate([w["router_w"][l], jnp.zeros((D_MODEL, LANES - N_EXPERTS), F32)], axis=1)
    rw_hi = rw.astype(BF16)
    return dict(
        w_mla=w_mla.astype(BF16), q_norm=row(w["mla_q_norm"][l]), kv_norm=row(w["mla_kv_norm"][l]), w_q=w_q.astype(BF16),
        w_ukv=w["mla_w_ukv"][l].astype(BF16),
        w_rwkv=w_in[:, o_r:o_s].astype(BF16), mu=row(w["rwkv_mu"][l]), w0=row(w["rwkv_w0"][l]), a0=row(w["rwkv_a0"][l]),
        k_k=row(w["rwkv_k_k"][l]), k_a=row(w["rwkv_k_a"][l]), r_k=row(w["rwkv_r_k"][l]),
        w_w2=w["rwkv_w_w2"][l].astype(BF16), w_a2=w["rwkv_w_a2"][l].astype(BF16), w_g2=w["rwkv_w_g2"][l].astype(BF16),
        head_sum=(hid[:, None] == hid[None, :]).astype(BF16),
        gn_g=row(w["rwkv_gn_g"][l]), gn_b=row(w["rwkv_gn_b"][l]),
        w_ssm=w_ssm.astype(BF16), conv_w=w["ssm_conv_w"][l].astype(F32), conv_b=row(w["ssm_conv_b"][l]),
        dt_bias=pad8(w["ssm_dt_bias"][l]), a_neg=pad8(-jnp.exp(w["ssm_a_log"][l].astype(F32))),
        d_skip=row(jnp.repeat(w["ssm_d"][l], SSM_HEADDIM)), ssm_norm_g=row(w["ssm_norm_g"][l]),
        w_gate=w_in[:, o_g:].astype(BF16), w_branch=w["w_branch"][l].astype(BF16), w_out=w["w_out"][l].astype(BF16),
        ln1_g=row(w["ln1_g"][l]), ln1_b=row(w["ln1_b"][l]),
        router_hi=rw_hi, router_lo=(rw - rw_hi.astype(F32)).astype(BF16),
        router_b=jnp.concatenate([w["router_b"][l].astype(F32), jnp.full((LANES - N_EXPERTS,), NEG_BIG, F32)])[None],
        w_gu=w["expert_w_gu"], b_gu=w["expert_b_gu"][l].astype(F32)[:, None, :],
        w_down=w["expert_w_down"], b_down=w["expert_b_down"][l].astype(F32)[:, None, :], layer=l,
        ln2_g=row(w["ln2_g"][l]), ln2_b=row(w["ln2_b"][l]),
    )


def _trunk_layer(x, pos0, past_lat, past_kr, rwkv_s0, shift_prev, ssm_s0, conv_prev, p):
    b, S, _ = x.shape
    y_mla, lat, kr = _mla_branch(x, pos0, past_lat, past_kr, p)
    y_rwkv, rwkv_s, shift_new = _rwkv_branch(x, rwkv_s0, shift_prev, p)
    y_ssm, ssm_s, conv_new = _ssm_branch(x, ssm_s0, conv_prev, p)
    T = b * S
    flat = lambda t: t.reshape(T, t.shape[-1])
    h, h_bf, top_i, top_g = _merge(flat(x), flat(y_mla), flat(y_rwkv), flat(y_ssm), p)
    out = _moe(h, h_bf, top_i, top_g, p, p["layer"])
    return out.reshape(b, S, D_MODEL), (lat, kr, rwkv_s, shift_new, ssm_s, conv_new)


def kernel(x_prompt, x_sample, cache_mla_latent, cache_mla_krope, state_rwkv, state_rwkv_shift, state_ssm, state_ssm_conv, w_in, mla_q_norm, mla_kv_norm, mla_w_uq, mla_w_ukv, rwkv_mu, rwkv_w0, rwkv_w_w2, rwkv_a0, rwkv_w_a2, rwkv_w_g2, rwkv_k_k, rwkv_k_a, rwkv_r_k, rwkv_gn_g, rwkv_gn_b, ssm_conv_w, ssm_conv_b, ssm_dt_bias, ssm_a_log, ssm_d, ssm_norm_g, w_branch, w_out, ln1_g, ln1_b, router_w, router_b, expert_w_gu, expert_b_gu, expert_w_down, expert_b_down, ln2_g, ln2_b):
    w = dict(w_in=w_in, mla_q_norm=mla_q_norm, mla_kv_norm=mla_kv_norm, mla_w_uq=mla_w_uq, mla_w_ukv=mla_w_ukv,
             rwkv_mu=rwkv_mu, rwkv_w0=rwkv_w0, rwkv_w_w2=rwkv_w_w2, rwkv_a0=rwkv_a0, rwkv_w_a2=rwkv_w_a2, rwkv_w_g2=rwkv_w_g2,
             rwkv_k_k=rwkv_k_k, rwkv_k_a=rwkv_k_a, rwkv_r_k=rwkv_r_k, rwkv_gn_g=rwkv_gn_g, rwkv_gn_b=rwkv_gn_b,
             ssm_conv_w=ssm_conv_w, ssm_conv_b=ssm_conv_b, ssm_dt_bias=ssm_dt_bias, ssm_a_log=ssm_a_log, ssm_d=ssm_d,
             ssm_norm_g=ssm_norm_g, w_branch=w_branch, w_out=w_out, ln1_g=ln1_g, ln1_b=ln1_b, router_w=router_w,
             router_b=router_b, expert_w_gu=expert_w_gu, expert_b_gu=expert_b_gu, expert_w_down=expert_w_down,
             expert_b_down=expert_b_down, ln2_g=ln2_g, ln2_b=ln2_b)
    bp = x_prompt.shape[0]
    past_len = cache_mla_latent.shape[2]
    zero_rwkv = jnp.zeros((bp, RWKV_HEADS, RWKV_HEAD, RWKV_HEAD), F32)
    zero_shift = jnp.zeros((bp, 1, RWKV_COLS), F32)
    zero_ssm = jnp.zeros((bp, SSM_HEADS, SSM_HEADDIM, SSM_STATE), F32)
    zero_conv = jnp.zeros((bp, CONV_W - 1, CONV_DIM), F32)
    yp, ys = x_prompt, x_sample
    st_p = [[] for _ in range(6)]
    st_s = [[] for _ in range(6)]
    for l in range(DEPTH):
        p = _prep_layer(l, w)
        yp, new_p = _trunk_layer(yp, 0, None, None, zero_rwkv, zero_shift, zero_ssm, zero_conv, p)
        ys, new_s = _trunk_layer(ys, past_len, cache_mla_latent[l], cache_mla_krope[l], state_rwkv[l], state_rwkv_shift[l],
                                 state_ssm[l], state_ssm_conv[l], p)
        for i in range(6):
            st_p[i].append(new_p[i])
            st_s[i].append(new_s[i])
    outs_p = [jnp.stack(t, axis=0) for t in st_p]
    outs_s = [jnp.stack(t, axis=0) for t in st_s]
    return (yp, ys, *outs_p, *outs_s)
```

```python
import functools
import math

import jax
import jax.numpy as jnp
from jax import lax
from jax.experimental import pallas as pl
from jax.experimental.pallas import tpu as pltpu

F32 = jnp.float32
BF16 = jnp.bfloat16

D_MODEL = 1024
DEPTH = 4
CHUNK = 64
MLA_HEADS = 8
QK_NOPE = 64
QK_ROPE = 32
QK_DIM = QK_NOPE + QK_ROPE
V_DIM = 64
VT_ROWS = V_DIM + 16
Q_LORA = 384
KV_LORA = 256
ROPE_THETA = 10000.0
RWKV_HEADS = 8
RWKV_HEAD = 64
RWKV_DIM = RWKV_HEADS * RWKV_HEAD
W_LORA = 64
A_LORA = 64
G_LORA = 128
RWKV_GN_EPS = 64e-5
SSM_HEADS = 8
SSM_HEADDIM = 64
SSM_DIM = SSM_HEADS * SSM_HEADDIM
SSM_STATE = 64
SSM_GROUPS = 2
CONV_W = 4
CONV_DIM = SSM_DIM + 2 * SSM_GROUPS * SSM_STATE
N_BRANCH = 3
BRANCH_DIM = 512
N_EXPERTS = 32
TOP_K = 4
D_FF = 1024
SWIGLU_LIMIT = 7.0
SWIGLU_ALPHA = 1.702
DEEPNORM_ALPHA = (2.0 * DEPTH) ** 0.25
LN_EPS = 1e-5
RMS_EPS = 1e-6
MLA_COLS = Q_LORA + KV_LORA + QK_ROPE
RWKV_COLS = 3 * RWKV_DIM + W_LORA + A_LORA + G_LORA
SSM_COLS = SSM_DIM + CONV_DIM + SSM_HEADS

LANES = 128
MLA_PROJ_COLS = 768
SSM_PROJ_COLS = SSM_DIM + CONV_DIM + LANES
MOE_ROWS = 512
NEG_BIG = -1e30
EXP2_SCALE = QK_DIM ** -0.5 * math.log2(math.e)
FAST_MARGIN = 100.0
VMEM_LIMIT = 56 * 1024 * 1024


def _cparams(*sem):
    return pltpu.CompilerParams(dimension_semantics=sem, vmem_limit_bytes=VMEM_LIMIT)


def _dot(a, b):
    return jnp.dot(a.astype(BF16), b.astype(BF16), preferred_element_type=F32)


def _dot_nt(a, b):
    return lax.dot_general(a.astype(BF16), b.astype(BF16), (((1,), (1,)), ((), ())), preferred_element_type=F32)


def _dot_tn(a, b):
    return lax.dot_general(a.astype(BF16), b.astype(BF16), (((0,), (0,)), ((), ())), preferred_element_type=F32)


def _dot_split_lhs(a, m):
    a_hi = a.astype(BF16)
    a_lo = (a - a_hi.astype(F32)).astype(BF16)
    return jnp.dot(a_hi, m, preferred_element_type=F32) + jnp.dot(a_lo, m, preferred_element_type=F32)


def _dot_tn_split(a, b):
    a_hi = a.astype(BF16)
    a_lo = (a - a_hi.astype(F32)).astype(BF16)
    b_hi = b.astype(BF16)
    b_lo = (b - b_hi.astype(F32)).astype(BF16)
    dn = (((0,), (0,)), ((), ()))
    return (lax.dot_general(a_hi, b_hi, dn, preferred_element_type=F32) + lax.dot_general(a_lo, b_hi, dn, preferred_element_type=F32)
            + lax.dot_general(a_hi, b_lo, dn, preferred_element_type=F32))


def _split3(x):
    hi = x.astype(BF16)
    r1 = x - hi.astype(F32)
    mid = r1.astype(BF16)
    lo = (r1 - mid.astype(F32)).astype(BF16)
    return hi, mid, lo


def _dot_exact_lhs(m, x):
    hi, mid, lo = _split3(x)
    return (jnp.dot(m, hi, preferred_element_type=F32) + jnp.dot(m, mid, preferred_element_type=F32)
            + jnp.dot(m, lo, preferred_element_type=F32))


def _dot_nt_exact_lhs(m, x):
    dn = (((1,), (1,)), ((), ()))
    hi, mid, lo = _split3(x)
    return (lax.dot_general(m, hi, dn, preferred_element_type=F32) + lax.dot_general(m, mid, dn, preferred_element_type=F32)
            + lax.dot_general(m, lo, dn, preferred_element_type=F32))


def _sigmoid(x):
    return 1.0 / (1.0 + jnp.exp(-x))


def _softplus(x):
    return jnp.maximum(x, 0.0) + jnp.log(1.0 + jnp.exp(-jnp.abs(x)))


def _rms(x, g):
    return x * lax.rsqrt(jnp.mean(x * x, axis=-1, keepdims=True) + RMS_EPS) * g


def _layer_norm(x, g, b):
    mu = jnp.mean(x, axis=-1, keepdims=True)
    xc = x - mu
    var = jnp.mean(xc * xc, axis=-1, keepdims=True)
    return xc * lax.rsqrt(var + LN_EPS) * g + b


def _full(shape):
    return pl.BlockSpec(shape, lambda *_: (0,) * len(shape))


def _mla_prep_kernel(x_ref, w_ref, qn_ref, kvn_ref, wq_ref, cq_ref, sq_ref, ck_ref, sk_ref, q_ref, lat_ref, kr_ref):
    u = _dot(x_ref[0], w_ref[...])
    c_q = u[:, :Q_LORA]
    c_kv = u[:, Q_LORA:Q_LORA + KV_LORA]
    kr = u[:, MLA_COLS - QK_ROPE:MLA_COLS]
    kr_rot = u[:, MLA_COLS:MLA_COLS + QK_ROPE]
    qall = _dot(_rms(c_q, qn_ref[...]), wq_ref[...])
    nope_w = MLA_HEADS * QK_NOPE
    rope_w = MLA_HEADS * QK_ROPE
    q_rope = qall[:, nope_w:nope_w + rope_w] * cq_ref[0] + qall[:, nope_w + rope_w:] * sq_ref[0]
    for h in range(MLA_HEADS):
        qh = jnp.concatenate([qall[:, h * QK_NOPE:(h + 1) * QK_NOPE], q_rope[:, h * QK_ROPE:(h + 1) * QK_ROPE]], axis=1)
        q_ref[0, h] = qh.astype(BF16)
    lat_ref[0] = _rms(c_kv, kvn_ref[...])
    kr_ref[0] = kr * ck_ref[0] + kr_rot * sk_ref[0]


def _kv_up_kernel(lat_ref, kr_ref, wk_ref, wvt_ref, k_ref, vt_ref, kn_ref):
    lat = lat_ref[0].astype(BF16)
    kn = jnp.dot(lat, wk_ref[...], preferred_element_type=F32)
    vt = _dot_nt(wvt_ref[...], lat)
    kr = kr_ref[0]
    ts = kn.shape[0]
    ones_rows = (lax.broadcasted_iota(jnp.int32, (VT_ROWS - V_DIM, ts), 0) == 0).astype(F32)
    ones = jnp.ones((8, QK_DIM), BF16)
    norms = []
    for h in range(MLA_HEADS):
        kh = jnp.concatenate([kn[:, h * QK_NOPE:(h + 1) * QK_NOPE], kr], axis=1).astype(BF16)
        k_ref[0, h] = kh
        khf = kh.astype(F32)
        sq = 1.02 * _dot_nt(ones, khf * khf)[0:1, :]
        norms.append(jnp.broadcast_to(jnp.max(sq, axis=1, keepdims=True), (1, LANES)))
        vt_ref[0, h] = jnp.concatenate([vt[h * V_DIM:(h + 1) * V_DIM], ones_rows], axis=0).astype(BF16)
    kn_ref[0, 0] = jnp.concatenate(norms, axis=0)


def _last_kv_block(qi, tq, tk, q_off, nk):
    q_hi = q_off + qi * tq + tq - 1
    return min(nk - 1, ((q_hi // CHUNK) * CHUNK + CHUNK - 1) // tk)


def _flash_kernel(qi_ref, kj_ref, last_ref, kmax_ref, q_ref, k_ref, vt_ref, o_ref, m_sc, mrun_sc, qn_sc, acc_sc, ok_sc,
                  *, tq, tk, q_off, sk_valid, nk):
    bi = pl.program_id(0)
    t = pl.program_id(1)
    qi = qi_ref[t]
    kj = kj_ref[t]

    @pl.when(kj == 0)
    def _():
        m_sc[...] = jnp.full(m_sc.shape, NEG_BIG, F32)
        mrun_sc[...] = jnp.full(mrun_sc.shape, NEG_BIG, F32)
        acc_sc[...] = jnp.zeros(acc_sc.shape, F32)
        ones = jnp.ones((8, QK_DIM), BF16)
        for h in range(MLA_HEADS):
            qf = q_ref[0, h].astype(F32)
            qn_sc[h] = jnp.sqrt(1.02 * _dot_nt(ones, qf * qf)[0:1, :])

    q_lo = q_off + qi * tq
    k_lo = kj * tk
    k_hi = k_lo + tk - 1
    full = jnp.logical_and((k_hi // CHUNK) <= (q_lo // CHUNK), k_hi < sk_valid)
    fast = jnp.logical_and(kj > 0, ok_sc[0] == 1)
    kj_next = jnp.minimum(kj + 1, nk - 1)

    def body(masked, lagged):
        if masked:
            key = k_lo + lax.broadcasted_iota(jnp.int32, (tk, tq), 0)
            qry = q_lo + lax.broadcasted_iota(jnp.int32, (tk, tq), 1)
            mask = jnp.logical_and((key // CHUNK) <= (qry // CHUNK), key < sk_valid)
        excess = None
        s_next = _dot_nt(k_ref[0, 0], q_ref[0, 0])
        for h in range(MLA_HEADS):
            s = s_next
            if h + 1 < MLA_HEADS:
                s_next = _dot_nt(k_ref[0, h + 1], q_ref[0, h + 1])
            if masked:
                s = jnp.where(mask, s, NEG_BIG)
            m_ref_old = m_sc[h]
            m_run = mrun_sc[h]
            m_blk = jnp.max(s, axis=0, keepdims=True)
            m_run_new = jnp.maximum(m_run, m_blk)
            m_ref_new = m_run if lagged else m_run_new
            p = jnp.exp2((s - m_ref_new) * EXP2_SCALE).astype(BF16)
            if masked:
                p = jnp.where(mask, p, jnp.zeros_like(p))
            alpha = jnp.exp2((m_ref_old - m_ref_new) * EXP2_SCALE)
            acc_sc[h] = alpha * acc_sc[h] + jnp.dot(vt_ref[0, h], p, preferred_element_type=F32)
            m_sc[h] = m_ref_new
            mrun_sc[h] = m_run_new
            ex = (qn_sc[h] * kmax_ref[(bi * MLA_HEADS + h) * nk + kj_next] - m_run_new) * EXP2_SCALE
            excess = ex if excess is None else jnp.maximum(excess, ex)
        ok_sc[0] = (jnp.max(excess) <= FAST_MARGIN).astype(jnp.int32)

    for masked in (False, True):
        for lagged in (False, True):
            cond = jnp.logical_and(full != masked, fast == lagged)
            pl.when(cond)(functools.partial(body, masked, lagged))

    @pl.when(last_ref[t] == 1)
    def _():
        o_ref[0] = jnp.concatenate([(acc_sc[h, :V_DIM] / acc_sc[h, V_DIM:V_DIM + 1]).T for h in range(MLA_HEADS)],
                                   axis=1).astype(o_ref.dtype)


def _rot_half(w):
    half = w.shape[-1] // 2
    return jnp.concatenate([-w[..., half:], w[..., :half]], axis=-1)


def _rope_tables(pos0, S):
    half = QK_ROPE // 2
    inv_freq = ROPE_THETA ** (-jnp.arange(half, dtype=F32) / half)
    ang = (pos0 + jnp.arange(S, dtype=jnp.int32)).astype(F32)[:, None] * inv_freq[None, :]
    cos = jnp.concatenate([jnp.cos(ang), jnp.cos(ang)], axis=-1)[None]
    sin = jnp.concatenate([jnp.sin(ang), jnp.sin(ang)], axis=-1)[None]
    return cos, sin, jnp.tile(cos, (1, 1, MLA_HEADS)), jnp.tile(sin, (1, 1, MLA_HEADS))


def _mla_branch(x, pos0, past_lat, past_kr, p):
    b, S, _ = x.shape
    ts = min(512, S)
    cos_k, sin_k, cos_q, sin_q = _rope_tables(pos0, S)
    q, lat, kr = pl.pallas_call(
        _mla_prep_kernel,
        grid=(b, S // ts),
        in_specs=[
            pl.BlockSpec((1, ts, D_MODEL), lambda bi, si: (bi, si, 0)),
            _full((D_MODEL, MLA_PROJ_COLS)), _full((1, Q_LORA)), _full((1, KV_LORA)),
            _full((Q_LORA, MLA_HEADS * (QK_NOPE + 2 * QK_ROPE))),
            pl.BlockSpec((1, ts, MLA_HEADS * QK_ROPE), lambda bi, si: (0, si, 0)),
            pl.BlockSpec((1, ts, MLA_HEADS * QK_ROPE), lambda bi, si: (0, si, 0)),
            pl.BlockSpec((1, ts, QK_ROPE), lambda bi, si: (0, si, 0)),
            pl.BlockSpec((1, ts, QK_ROPE), lambda bi, si: (0, si, 0)),
        ],
        out_specs=[
            pl.BlockSpec((1, MLA_HEADS, ts, QK_DIM), lambda bi, si: (bi, 0, si, 0)),
            pl.BlockSpec((1, ts, KV_LORA), lambda bi, si: (bi, si, 0)),
            pl.BlockSpec((1, ts, QK_ROPE), lambda bi, si: (bi, si, 0)),
        ],
        out_shape=[
            jax.ShapeDtypeStruct((b, MLA_HEADS, S, QK_DIM), BF16),
            jax.ShapeDtypeStruct((b, S, KV_LORA), F32),
            jax.ShapeDtypeStruct((b, S, QK_ROPE), F32),
        ],
        compiler_params=_cparams("parallel", "parallel"),
        name="mla_prep",
    )(x, p["w_mla"], p["q_norm"], p["kv_norm"], p["w_q"], cos_q, sin_q, cos_k, sin_k)

    if past_lat is None:
        lat_all, kr_all, q_off, sk_valid = lat, kr, 0, S
        tk = min(512, S)
    else:
        past_len = past_lat.shape[1]
        sk_valid = past_len + S
        tk = -(-sk_valid // LANES) * LANES
        pad = tk - sk_valid
        lat_all = jnp.concatenate([past_lat, lat, jnp.zeros((b, pad, KV_LORA), F32)], axis=1)
        kr_all = jnp.concatenate([past_kr, kr, jnp.zeros((b, pad, QK_ROPE), F32)], axis=1)
        q_off = past_len
    Sk = lat_all.shape[1]
    tku = tk
    k, vt, ksq = pl.pallas_call(
        _kv_up_kernel,
        grid=(b, Sk // tku),
        in_specs=[
            pl.BlockSpec((1, tku, KV_LORA), lambda bi, si: (bi, si, 0)),
            pl.BlockSpec((1, tku, QK_ROPE), lambda bi, si: (bi, si, 0)),
            _full((KV_LORA, MLA_HEADS * QK_NOPE)), _full((MLA_HEADS * V_DIM, KV_LORA)),
        ],
        out_specs=[
            pl.BlockSpec((1, MLA_HEADS, tku, QK_DIM), lambda bi, si: (bi, 0, si, 0)),
            pl.BlockSpec((1, MLA_HEADS, VT_ROWS, tku), lambda bi, si: (bi, 0, 0, si)),
            pl.BlockSpec((1, 1, MLA_HEADS, LANES), lambda bi, si: (bi, si, 0, 0)),
        ],
        out_shape=[
            jax.ShapeDtypeStruct((b, MLA_HEADS, Sk, QK_DIM), BF16),
            jax.ShapeDtypeStruct((b, MLA_HEADS, VT_ROWS, Sk), BF16),
            jax.ShapeDtypeStruct((b, Sk // tku, MLA_HEADS, LANES), F32),
        ],
        compiler_params=_cparams("parallel", "parallel"),
        name="mla_kv_up",
    )(lat_all, kr_all, p["w_uk"], p["w_uvt"])

    tq = min(512, S)
    nq, nk = S // tq, Sk // tk
    pairs = [(qi, kj) for qi in range(nq) for kj in range(_last_kv_block(qi, tq, tk, q_off, nk) + 1)]
    qi_tab = jnp.asarray([pq for pq, _ in pairs], jnp.int32)
    kj_tab = jnp.asarray([pk for _, pk in pairs], jnp.int32)
    last_tab = jnp.asarray([int(pk == _last_kv_block(pq, tq, tk, q_off, nk)) for pq, pk in pairs], jnp.int32)
    kmax = jnp.sqrt(jnp.swapaxes(ksq[:, :, :, 0], 1, 2)).reshape(-1)
    y = pl.pallas_call(
        functools.partial(_flash_kernel, tq=tq, tk=tk, q_off=q_off, sk_valid=sk_valid, nk=nk),
        grid_spec=pltpu.PrefetchScalarGridSpec(
            num_scalar_prefetch=4,
            grid=(b, len(pairs)),
            in_specs=[
                pl.BlockSpec((1, MLA_HEADS, tq, QK_DIM), lambda bi, t, qt, kt, lt, km: (bi, 0, qt[t], 0)),
                pl.BlockSpec((1, MLA_HEADS, tk, QK_DIM), lambda bi, t, qt, kt, lt, km: (bi, 0, kt[t], 0)),
                pl.BlockSpec((1, MLA_HEADS, VT_ROWS, tk), lambda bi, t, qt, kt, lt, km: (bi, 0, 0, kt[t])),
            ],
            out_specs=pl.BlockSpec((1, tq, MLA_HEADS * V_DIM), lambda bi, t, qt, kt, lt, km: (bi, qt[t], 0)),
            scratch_shapes=[
                pltpu.VMEM((MLA_HEADS, 1, tq), F32),
                pltpu.VMEM((MLA_HEADS, 1, tq), F32),
                pltpu.VMEM((MLA_HEADS, 1, tq), F32),
                pltpu.VMEM((MLA_HEADS, VT_ROWS, tq), F32),
                pltpu.SMEM((1,), jnp.int32),
            ],
        ),
        out_shape=jax.ShapeDtypeStruct((b, S, MLA_HEADS * V_DIM), BF16),
        compiler_params=_cparams("parallel", "arbitrary"),
        name="mla_flash",
    )(qi_tab, kj_tab, last_tab, kmax, q, k, vt)
    return y, lat, kr


def _rwkv_prep_kernel(x_ref, sp_ref, w_ref, mu_ref, w0_ref, a0_ref, kk_ref, ka_ref, rk_ref, ww2_ref, wa2_ref, wg2_ref,
                      hsum_ref, r_out, lw_out, k_out, v_out, kk_out, a_out, g_out, bon_out, sh_out, prev_sc):
    si = pl.program_id(1)

    @pl.when(si == 0)
    def _():
        prev_sc[...] = sp_ref[0]

    u = _dot(x_ref[0], w_ref[...])
    ts = u.shape[0]
    row = lax.broadcasted_iota(jnp.int32, u.shape, 0)
    shifted = jnp.where(row == 0, prev_sc[...], pltpu.roll(u, 1, axis=0))
    prev_sc[...] = u[ts - 1:ts, :]
    sh_out[0] = u[ts - 1:ts, :]
    m = u + (shifted - u) * mu_ref[...]
    r = m[:, :RWKV_DIM]
    k = m[:, RWKV_DIM:2 * RWKV_DIM]
    v = m[:, 2 * RWKV_DIM:3 * RWKV_DIM]
    o = 3 * RWKV_DIM
    wl = m[:, o:o + W_LORA]
    al = m[:, o + W_LORA:o + W_LORA + A_LORA]
    gl = m[:, o + W_LORA + A_LORA:]
    d = w0_ref[...] + _dot(jnp.tanh(wl), ww2_ref[...])
    lw_out[0] = -jnp.exp(-_softplus(-d) - 0.5)
    a = _sigmoid(a0_ref[...] + _dot(al, wa2_ref[...]))
    g_out[0] = _dot(_sigmoid(gl), wg2_ref[...])
    hsum = hsum_ref[...]
    kk = k * kk_ref[...]
    kk_out[0] = kk * lax.rsqrt(_dot_split_lhs(kk * kk, hsum) + 1e-12)
    kh = k * (1.0 + (a - 1.0) * ka_ref[...])
    bon_out[0] = _dot_split_lhs(r * kh * rk_ref[...], hsum) * v
    r_out[0] = r
    k_out[0] = kh
    v_out[0] = v
    a_out[0] = a


def _rwkv_chunk_prepare(ins, c):
    r, lw, k, v, kk, a = (list(t) for t in zip(*ins))
    n = len(ins)
    L = r[0].shape[0]
    L2 = 2 * L
    lo = c["lane_lo"]

    def stack(xv):
        return jnp.concatenate([jnp.where(lo, xv, 0.0), jnp.where(lo, 0.0, xv)], axis=0)

    g = [_dot_exact_lhs(c["tri"], x) for x in lw]
    gl = [x[L - 1:L, :] for x in g]
    e_neg = [jnp.exp(-x) for x in g]
    at_s = [stack(-kk[i] * jnp.exp(g[i] - lw[i])) for i in range(n)]
    rt_s = [stack(r[i] * jnp.exp(g[i])) for i in range(n)]
    beta = [kk[i] * a[i] for i in range(n)]
    mm = []
    for i in range(n):
        bt = beta[i] * e_neg[i]
        kt = k[i] * e_neg[i]
        mm.append(_dot_nt(jnp.concatenate([at_s[i], rt_s[i]], axis=0), jnp.concatenate([bt, bt, kt, kt], axis=0)))
    nmat = [jnp.where(c["strict"], x[:L2, :L2], 0.0) for x in mm]
    mak = [jnp.where(c["strict"], x[:L2, L2:], 0.0) for x in mm]
    mrb = [jnp.where(c["incl"], x[L2:, :L2], 0.0) for x in mm]
    mrk = [jnp.where(c["incl"], x[L2:, L2:], 0.0) for x in mm]
    xinv = [c["eye2"] + x for x in nmat]
    pw = nmat
    for _ in range(int(math.log2(L)) - 1):
        pw = [_dot(x, x) for x in pw]
        xinv = [xinv[i] + _dot(xinv[i], pw[i]) for i in range(n)]
    vs = [stack(x) for x in v]
    w1 = [_dot(mak[i], vs[i]) for i in range(n)]
    au = [_dot(xinv[i], jnp.concatenate([at_s[i], w1[i]], axis=1)) for i in range(n)]
    ry = [_dot(mrb[i], au[i]) for i in range(n)]
    mv = [_dot(mrk[i], vs[i]) for i in range(n)]
    rh = [rt_s[i] + ry[i][:, :LANES] for i in range(n)]
    yh = [ry[i][:, LANES:] + mv[i] for i in range(n)]
    bs = [stack(beta[i] * jnp.exp(gl[i] - g[i])) for i in range(n)]
    ks = [stack(k[i] * jnp.exp(gl[i] - g[i])) for i in range(n)]
    pt_lr = [_dot_tn(bs[i], au[i][:, :LANES]) for i in range(n)]
    qt = [_dot_tn(jnp.concatenate([bs[i], ks[i]], axis=0), jnp.concatenate([au[i][:, LANES:], vs[i]], axis=0)) for i in range(n)]
    return [(rh[i], yh[i], pt_lr[i], qt[i], gl[i]) for i in range(n)]


def _rwkv_scan_kernel(r_ref, lw_ref, k_ref, v_ref, kk_ref, a_ref, g_ref, bon_ref, gng_ref, gnb_ref, z0_ref,
                      y_ref, zf_ref, z_sc, *, L, n_chunks):
    si = pl.program_id(1)

    @pl.when(si == 0)
    def _():
        z_sc[...] = z0_ref[0]

    L2 = 2 * L
    ri = lax.broadcasted_iota(jnp.int32, (L2, L2), 0)
    ci = lax.broadcasted_iota(jnp.int32, (L2, L2), 1)
    same = (ri // L) == (ci // L)
    rl = lax.broadcasted_iota(jnp.int32, (L, L), 0)
    cl = lax.broadcasted_iota(jnp.int32, (L, L), 1)
    r128 = lax.broadcasted_iota(jnp.int32, (LANES, LANES), 0)
    c128 = lax.broadcasted_iota(jnp.int32, (LANES, LANES), 1)
    consts = dict(
        tri=(cl <= rl).astype(BF16),
        strict=jnp.logical_and(same, (ci % L) < (ri % L)),
        incl=jnp.logical_and(same, (ci % L) <= (ri % L)),
        eye2=(ri == ci).astype(F32),
        lane_lo=lax.broadcasted_iota(jnp.int32, (L, LANES), 1) < RWKV_HEAD,
    )
    diag = r128 == c128
    gmean = jnp.where((r128 // RWKV_HEAD) == (c128 // RWKV_HEAD), 1.0 / RWKV_HEAD, 0.0).astype(BF16)
    npair = RWKV_HEADS // 2

    def head_mean(xv):
        hi = xv.astype(BF16)
        lo = (xv - hi.astype(F32)).astype(BF16)
        return jnp.dot(hi, gmean, preferred_element_type=F32) + jnp.dot(lo, gmean, preferred_element_type=F32)

    cpi = 4 if n_chunks % 4 == 0 else (2 if n_chunks % 2 == 0 else 1)
    lanes = [slice(p * LANES, (p + 1) * LANES) for p in range(npair)]

    def chunk(ci_, carry):
        rows = [pl.ds(pl.multiple_of((ci_ * cpi + j) * L, L), L) for j in range(cpi)]
        ins = [(r_ref[0, rw, ln], lw_ref[0, rw, ln], k_ref[0, rw, ln], v_ref[0, rw, ln], kk_ref[0, rw, ln], a_ref[0, rw, ln])
               for rw in rows for ln in lanes]
        prep = _rwkv_chunk_prepare(ins, consts)
        z = [z_sc[p] for p in range(npair)]
        for j in range(cpi):
            pj = prep[j * npair:(j + 1) * npair]
            ys = [_dot(pj[p][0], z[p]) + pj[p][1] for p in range(npair)]
            zlr = [_dot(pj[p][2], z[p]) for p in range(npair)]
            y = [x[:L] + x[L:] for x in ys]
            mean = [head_mean(x) for x in y]
            yc = [y[p] - mean[p] for p in range(npair)]
            var = [head_mean(x * x) for x in yc]
            for p in range(npair):
                ln = lanes[p]
                gcol = jnp.sum(jnp.where(diag, jnp.broadcast_to(jnp.exp(pj[p][4]), (LANES, LANES)), 0.0), axis=1, keepdims=True)
                z[p] = gcol * z[p] + zlr[p] + pj[p][3]
                o = yc[p] * lax.rsqrt(var[p] + RWKV_GN_EPS) * gng_ref[:, ln] + gnb_ref[:, ln]
                y_ref[0, rows[j], ln] = ((o + bon_ref[0, rows[j], ln]) * g_ref[0, rows[j], ln]).astype(y_ref.dtype)
        for p in range(npair):
            z_sc[p] = z[p]
        return carry

    lax.fori_loop(0, n_chunks // cpi, chunk, 0)

    @pl.when(si == pl.num_programs(1) - 1)
    def _():
        zf_ref[0] = z_sc[...]


def _rwkv_branch(x, s0, shift_prev, p):
    b, S, _ = x.shape
    ts = min(512, S)
    tok = lambda w: pl.BlockSpec((1, ts, w), lambda bi, si: (bi, si, 0))
    row = lambda w: pl.BlockSpec((1, 1, w), lambda bi, si: (bi, 0, 0))
    outs = pl.pallas_call(
        _rwkv_prep_kernel,
        grid=(b, S // ts),
        in_specs=[tok(D_MODEL), row(RWKV_COLS), _full((D_MODEL, RWKV_COLS)), _full((1, RWKV_COLS))]
        + [_full((1, RWKV_DIM))] * 5
        + [_full((W_LORA, RWKV_DIM)), _full((A_LORA, RWKV_DIM)), _full((G_LORA, RWKV_DIM)), _full((RWKV_DIM, RWKV_DIM))],
        out_specs=[tok(RWKV_DIM)] * 8 + [row(RWKV_COLS)],
        out_shape=[jax.ShapeDtypeStruct((b, S, RWKV_DIM), F32)] * 8 + [jax.ShapeDtypeStruct((b, 1, RWKV_COLS), F32)],
        scratch_shapes=[pltpu.VMEM((1, RWKV_COLS), F32)],
        compiler_params=_cparams("parallel", "arbitrary"),
        name="rwkv_prep",
    )(x, shift_prev, p["w_rwkv"], p["mu"], p["w0"], p["a0"], p["k_k"], p["k_a"], p["r_k"], p["w_w2"], p["w_a2"], p["w_g2"],
      p["head_sum"])
    r, lw, kh, v, kk, a, g, bonus, shift_new = outs

    L = min(CHUNK, S)
    tb = min(4 * L, S)
    npair = RWKV_HEADS // 2
    zt = jnp.swapaxes(s0.astype(F32), 2, 3).reshape(b, npair, 2, RWKV_HEAD, RWKV_HEAD)
    z0 = jnp.einsum("bpikv,ij->bpikjv", zt, jnp.eye(2, dtype=F32)).reshape(b, npair, LANES, LANES)
    tokb = lambda: pl.BlockSpec((1, tb, RWKV_DIM), lambda bi, si: (bi, si, 0))
    zspec = pl.BlockSpec((1, npair, LANES, LANES), lambda bi, si: (bi, 0, 0, 0))
    y, zf = pl.pallas_call(
        functools.partial(_rwkv_scan_kernel, L=L, n_chunks=tb // L),
        grid=(b, S // tb),
        in_specs=[tokb() for _ in range(8)] + [_full((1, RWKV_DIM)), _full((1, RWKV_DIM)), zspec],
        out_specs=[tokb(), zspec],
        out_shape=[jax.ShapeDtypeStruct((b, S, RWKV_DIM), BF16), jax.ShapeDtypeStruct((b, npair, LANES, LANES), F32)],
        scratch_shapes=[pltpu.VMEM((npair, LANES, LANES), F32)],
        compiler_params=_cparams("parallel", "arbitrary"),
        name="rwkv_scan",
    )(r, lw, kh, v, kk, a, g, bonus, p["gn_g"], p["gn_b"], z0)
    zd = jnp.einsum("bpikiv->bpikv", zf.reshape(b, npair, 2, RWKV_HEAD, 2, RWKV_HEAD))
    s_new = jnp.swapaxes(zd.reshape(b, RWKV_HEADS, RWKV_HEAD, RWKV_HEAD), 2, 3)
    return y, s_new, shift_new


def _ssd_kernel(x_ref, w_ref, s0_ref, cp_ref, cw_ref, cb_ref, dtb_ref, a_ref, dsk_ref, ng_ref,
                y_ref, sf_ref, ct_ref, st_sc, tail_sc, *, L):
    si = pl.program_id(1)

    @pl.when(si == 0)
    def _():
        st_sc[...] = s0_ref[0]
        tail_sc[...] = cp_ref[0]

    u = _dot(x_ref[0], w_ref[...])
    z = u[:, :SSM_DIM]
    xbc = u[:, SSM_DIM:SSM_DIM + CONV_DIM]
    dtr = u[:, SSM_DIM + CONV_DIM:]
    tail = tail_sc[...]
    row = lax.broadcasted_iota(jnp.int32, xbc.shape, 0)
    sh1 = jnp.where(row == 0, tail[7:8], pltpu.roll(xbc, 1, axis=0))
    sh2 = jnp.where(row == 0, tail[6:7], pltpu.roll(sh1, 1, axis=0))
    sh3 = jnp.where(row == 0, tail[5:6], pltpu.roll(sh2, 1, axis=0))
    tail_sc[...] = xbc[L - 8:L]
    ct_ref[0] = xbc[L - 8:L]
    conv = cb_ref[...] + cw_ref[3:4] * xbc + cw_ref[2:3] * sh1 + cw_ref[1:2] * sh2 + cw_ref[0:1] * sh3
    act = conv * _sigmoid(conv)
    xs = act[:, :SSM_DIM]
    gw = SSM_STATE
    dt = _softplus(dtr + dtb_ref[...])
    a = dt * a_ref[...]
    rl = lax.broadcasted_iota(jnp.int32, (L, L), 0)
    cl = lax.broadcasted_iota(jnp.int32, (L, L), 1)
    causal = cl <= rl
    cum = _dot_exact_lhs(causal.astype(BF16), a)
    sel = (lax.broadcasted_iota(jnp.int32, (16, LANES), 0) == lax.broadcasted_iota(jnp.int32, (16, LANES), 1)).astype(BF16)
    cum_t = _dot_nt_exact_lhs(sel, cum)
    dt_t = _dot_nt_exact_lhs(sel, dt)
    ys = []
    hpg = SSM_HEADS // SSM_GROUPS
    for gi in range(SSM_GROUPS):
        bg = act[:, SSM_DIM + gi * gw:SSM_DIM + (gi + 1) * gw]
        cg = act[:, SSM_DIM + SSM_GROUPS * gw + gi * gw:SSM_DIM + SSM_GROUPS * gw + (gi + 1) * gw]
        cb = _dot_nt(cg, bg)
        for h in range(gi * hpg, (gi + 1) * hpg):
            xh = xs[:, h * SSM_HEADDIM:(h + 1) * SSM_HEADDIM]
            cc = cum[:, h:h + 1]
            seg = cc - cum_t[h:h + 1, :]
            dec = jnp.where(causal, jnp.exp(jnp.minimum(seg, 0.0)), 0.0)
            sc = cb * dec * dt_t[h:h + 1, :]
            st = st_sc[h]
            yh = _dot(sc, xh) + _dot_nt(cg, st) * jnp.exp(cc)
            clast = cum[L - 1:L, h:h + 1]
            wcol = jnp.exp(clast - cc) * dt[:, h:h + 1]
            st_sc[h] = st * jnp.exp(clast) + _dot_tn_split(xh * wcol, bg)
            ys.append(yh)
    y = jnp.concatenate(ys, axis=1) + dsk_ref[...] * xs
    y = y * (z * _sigmoid(z))
    gdim = SSM_DIM // SSM_GROUPS
    outs = []
    for gi in range(SSM_GROUPS):
        yg = y[:, gi * gdim:(gi + 1) * gdim]
        outs.append(yg * lax.rsqrt(jnp.mean(yg * yg, axis=-1, keepdims=True) + RMS_EPS))
    y_ref[0] = (jnp.concatenate(outs, axis=1) * ng_ref[...]).astype(y_ref.dtype)

    @pl.when(si == pl.num_programs(1) - 1)
    def _():
        sf_ref[0] = st_sc[...]


def _ssm_branch(x, s0, conv_prev, p):
    b, S, _ = x.shape
    L = min(256, S)
    cp = jnp.concatenate([jnp.zeros((b, 8 - (CONV_W - 1), CONV_DIM), F32), conv_prev.astype(F32)], axis=1)
    sspec = pl.BlockSpec((1, SSM_HEADS, SSM_HEADDIM, SSM_STATE), lambda bi, si: (bi, 0, 0, 0))
    cspec = pl.BlockSpec((1, 8, CONV_DIM), lambda bi, si: (bi, 0, 0))
    y, s_new, ctail = pl.pallas_call(
        functools.partial(_ssd_kernel, L=L),
        grid=(b, S // L),
        in_specs=[
            pl.BlockSpec((1, L, D_MODEL), lambda bi, si: (bi, si, 0)),
            _full((D_MODEL, SSM_PROJ_COLS)), sspec, cspec,
            _full((CONV_W, CONV_DIM)), _full((1, CONV_DIM)), _full((1, LANES)), _full((1, LANES)),
            _full((1, SSM_DIM)), _full((1, SSM_DIM)),
        ],
        out_specs=[pl.BlockSpec((1, L, SSM_DIM), lambda bi, si: (bi, si, 0)), sspec, cspec],
        out_shape=[
            jax.ShapeDtypeStruct((b, S, SSM_DIM), BF16),
            jax.ShapeDtypeStruct((b, SSM_HEADS, SSM_HEADDIM, SSM_STATE), F32),
            jax.ShapeDtypeStruct((b, 8, CONV_DIM), F32),
        ],
        scratch_shapes=[pltpu.VMEM((SSM_HEADS, SSM_HEADDIM, SSM_STATE), F32), pltpu.VMEM((8, CONV_DIM), F32)],
        compiler_params=_cparams("parallel", "arbitrary"),
        name="ssd",
    )(x, p["w_ssm"], s0.astype(F32), cp, p["conv_w"], p["conv_b"], p["dt_bias"], p["a_neg"], p["d_skip"], p["ssm_norm_g"])
    return y, s_new, ctail[:, 8 - (CONV_W - 1):]


def _merge_kernel(x_ref, ym_ref, yr_ref, ys_ref, wg_ref, wb_ref, wo_ref, g_ref, b_ref, rw_ref, rb_ref,
                  h_ref, hb_ref, ti_ref, tg_ref):
    x = x_ref[...]
    gates = _sigmoid(_dot(x, wg_ref[...]))
    mix = gates[:, :D_MODEL] * _dot(ym_ref[...], wb_ref[0])
    mix = mix + gates[:, D_MODEL:2 * D_MODEL] * _dot(yr_ref[...], wb_ref[1])
    mix = mix + gates[:, 2 * D_MODEL:] * _dot(ys_ref[...], wb_ref[2])
    h = _layer_norm(DEEPNORM_ALPHA * x + _dot(mix, wo_ref[...]), g_ref[...], b_ref[...])
    h_ref[...] = h
    hb_ref[...] = h.astype(BF16)
    logits = _dot(h, rw_ref[...]) + rb_ref[...]
    lane = lax.broadcasted_iota(jnp.int32, logits.shape, 1)
    idx_out = jnp.zeros(logits.shape, jnp.int32)
    val_out = jnp.zeros(logits.shape, F32)
    top = None
    den = None
    for kth in range(TOP_K):
        mval = jnp.max(logits, axis=-1, keepdims=True)
        midx = jnp.min(jnp.where(logits == mval, lane, LANES), axis=-1, keepdims=True)
        if kth == 0:
            top = mval
            e = jnp.ones_like(mval)
            den = e
        else:
            e = jnp.exp(mval - top)
            den = den + e
        idx_out = jnp.where(lane == kth, midx, idx_out)
        val_out = jnp.where(lane == kth, e, val_out)
        logits = jnp.where(lane == midx, NEG_BIG * 2.0, logits)
    ti_ref[...] = idx_out
    tg_ref[...] = val_out / den


def _merge(x2, ym, yr, ys, p):
    T = x2.shape[0]
    tm = min(512, T)
    tok = lambda w: pl.BlockSpec((tm, w), lambda i: (i, 0))
    return pl.pallas_call(
        _merge_kernel,
        grid=(T // tm,),
        in_specs=[tok(D_MODEL), tok(BRANCH_DIM), tok(BRANCH_DIM), tok(BRANCH_DIM),
                  _full((D_MODEL, N_BRANCH * D_MODEL)), _full((N_BRANCH, BRANCH_DIM, D_MODEL)), _full((D_MODEL, D_MODEL)),
                  _full((1, D_MODEL)), _full((1, D_MODEL)), _full((D_MODEL, LANES)), _full((1, LANES))],
        out_specs=[tok(D_MODEL), tok(D_MODEL), tok(LANES), tok(LANES)],
        out_shape=[jax.ShapeDtypeStruct((T, D_MODEL), F32), jax.ShapeDtypeStruct((T, D_MODEL), BF16),
                   jax.ShapeDtypeStruct((T, LANES), jnp.int32), jax.ShapeDtypeStruct((T, LANES), F32)],
        compiler_params=_cparams("parallel"),
        name="merge_router",
    )(x2, ym, yr, ys, p["w_gate"], p["w_branch"], p["w_out"], p["ln1_g"], p["ln1_b"], p["router_w"], p["router_b"])


def _expert_kernel(be_ref, nu_ref, x_ref, wgu_ref, bgu_ref, wd_ref, bd_ref, o_ref, wgu_sc, wd_sc):
    i = pl.program_id(0)
    used = i < nu_ref[0]
    new_expert = jnp.logical_or(i == 0, be_ref[i] != be_ref[jnp.maximum(i - 1, 0)])

    @pl.when(jnp.logical_and(used, new_expert))
    def _():
        wgu_sc[...] = wgu_ref[0, 0].astype(BF16)
        wd_sc[...] = wd_ref[0, 0].astype(BF16)

    @pl.when(used)
    def _():
        hgu = jnp.dot(x_ref[...], wgu_sc[...], preferred_element_type=F32) + bgu_ref[0]
        gate = jnp.minimum(hgu[:, :D_FF], SWIGLU_LIMIT)
        up = jnp.clip(hgu[:, D_FF:], -SWIGLU_LIMIT, SWIGLU_LIMIT)
        hid = gate * _sigmoid(SWIGLU_ALPHA * gate) * (up + 1.0)
        o_ref[...] = (jnp.dot(hid.astype(BF16), wd_sc[...], preferred_element_type=F32) + bd_ref[0]).astype(o_ref.dtype)

    @pl.when(jnp.logical_not(used))
    def _():
        o_ref[...] = jnp.zeros(o_ref.shape, o_ref.dtype)


def _combine_kernel(y0_ref, y1_ref, y2_ref, y3_ref, tg_ref, h_ref, g_ref, b_ref, o_ref):
    tg = tg_ref[...]
    f = tg[:, 0:1] * y0_ref[...].astype(F32)
    for kth, y_ref in ((1, y1_ref), (2, y2_ref), (3, y3_ref)):
        f = f + tg[:, kth:kth + 1] * y_ref[...].astype(F32)
    o_ref[...] = _layer_norm(DEEPNORM_ALPHA * h_ref[...] + f, g_ref[...], b_ref[...])


def _moe(h, h_bf, top_i, top_g, p, layer):
    T = h.shape[0]
    n_assign = T * TOP_K
    flat_e = top_i[:, :TOP_K].reshape(-1)
    ids = jnp.arange(n_assign, dtype=jnp.int32)
    skey = jnp.sort(flat_e * n_assign + ids)
    sorted_e = skey // n_assign
    order = skey - sorted_e * n_assign
    experts = jnp.arange(N_EXPERTS, dtype=jnp.int32)
    counts = jnp.sum((flat_e[:, None] == experts[None, :]).astype(jnp.int32), axis=0)
    padded = (counts + MOE_ROWS - 1) // MOE_ROWS * MOE_ROWS
    pad_end = jnp.cumsum(padded)
    pad_start = pad_end - padded
    start = jnp.cumsum(counts) - counts
    n_blocks = -(-(n_assign + N_EXPERTS * (MOE_ROWS - 1)) // MOE_ROWS)
    n_rows = n_blocks * MOE_ROWS
    blk_row0 = jnp.arange(n_blocks, dtype=jnp.int32) * MOE_ROWS
    block_e = jnp.minimum(jnp.sum((pad_end[None, :] <= blk_row0[:, None]).astype(jnp.int32), axis=1), N_EXPERTS - 1)
    n_used = (pad_end[-1:] // MOE_ROWS).astype(jnp.int32)
    within = (blk_row0 - pad_start[block_e])[:, None] + jnp.arange(MOE_ROWS, dtype=jnp.int32)[None, :]
    valid = within < counts[block_e][:, None]
    src = jnp.clip(start[block_e][:, None] + within, 0, n_assign - 1)
    own_row = (blk_row0[:, None] + jnp.arange(MOE_ROWS, dtype=jnp.int32)[None, :]) % T
    row_tok = jnp.where(valid, order[src.reshape(-1)].reshape(n_blocks, MOE_ROWS) // TOP_K, own_row).reshape(-1)
    xb = h_bf[row_tok]

    yb = pl.pallas_call(
        _expert_kernel,
        grid_spec=pltpu.PrefetchScalarGridSpec(
            num_scalar_prefetch=2,
            grid=(n_blocks,),
            in_specs=[
                pl.BlockSpec((MOE_ROWS, D_MODEL), lambda i, be, nu: (i, 0)),
                pl.BlockSpec((1, 1, D_MODEL, 2 * D_FF), lambda i, be, nu: (layer, be[i], 0, 0)),
                pl.BlockSpec((1, 1, 2 * D_FF), lambda i, be, nu: (be[i], 0, 0)),
                pl.BlockSpec((1, 1, D_FF, D_MODEL), lambda i, be, nu: (layer, be[i], 0, 0)),
                pl.BlockSpec((1, 1, D_MODEL), lambda i, be, nu: (be[i], 0, 0)),
            ],
            out_specs=pl.BlockSpec((MOE_ROWS, D_MODEL), lambda i, be, nu: (i, 0)),
            scratch_shapes=[pltpu.VMEM((D_MODEL, 2 * D_FF), BF16), pltpu.VMEM((D_FF, D_MODEL), BF16)],
        ),
        out_shape=jax.ShapeDtypeStruct((n_rows, D_MODEL), BF16),
        compiler_params=_cparams("arbitrary"),
        name="moe_experts",
    )(block_e, n_used, xb, p["w_gu"], p["b_gu"], p["w_down"], p["b_down"])

    dest_sorted = pad_start[sorted_e] + (ids - start[sorted_e])
    _, dest = lax.sort((order, dest_sorted), num_keys=1)
    dest = dest.reshape(T, TOP_K)
    ygs = [yb[dest[:, kth]] for kth in range(TOP_K)]
    tm = min(512, T)
    tok = lambda w: pl.BlockSpec((tm, w), lambda i: (i, 0))
    return pl.pallas_call(
        _combine_kernel,
        grid=(T // tm,),
        in_specs=[tok(D_MODEL)] * TOP_K + [tok(LANES), tok(D_MODEL), _full((1, D_MODEL)), _full((1, D_MODEL))],
        out_specs=tok(D_MODEL),
        out_shape=jax.ShapeDtypeStruct((T, D_MODEL), F32),
        compiler_params=_cparams("parallel"),
        name="moe_combine_ln",
    )(*ygs, top_g, h, p["ln2_g"], p["ln2_b"])


def _prep_layer(l, w):
    w_in = w["w_in"][l]
    kr_cols = w_in[:, MLA_COLS - QK_ROPE:MLA_COLS]
    w_mla = jnp.concatenate([w_in[:, :MLA_COLS], _rot_half(kr_cols),
                             jnp.zeros((D_MODEL, MLA_PROJ_COLS - MLA_COLS - QK_ROPE), F32)], axis=1)
    wq = w["mla_w_uq"][l].reshape(Q_LORA, MLA_HEADS, QK_DIM)
    wq_rope = wq[:, :, QK_NOPE:]
    w_q = jnp.concatenate([wq[:, :, :QK_NOPE].reshape(Q_LORA, -1), wq_rope.reshape(Q_LORA, -1),
                           _rot_half(wq_rope).reshape(Q_LORA, -1)], axis=1)
    wkv = w["mla_w_ukv"][l].reshape(KV_LORA, MLA_HEADS, QK_NOPE + V_DIM)
    o_r = MLA_COLS
    o_s = o_r + RWKV_COLS
    o_g = o_s + SSM_COLS
    w_ssm = jnp.concatenate([w_in[:, o_s:o_s + SSM_DIM + CONV_DIM], w_in[:, o_s + SSM_DIM + CONV_DIM:o_g],
                             jnp.zeros((D_MODEL, LANES - SSM_HEADS), F32)], axis=1)
    pad8 = lambda v: jnp.concatenate([v.astype(F32), jnp.zeros((LANES - SSM_HEADS,), F32)])[None]
    row = lambda v: v.astype(F32)[None]
    hid = jnp.arange(RWKV_DIM) // RWKV_HEAD
    rw = jnp.concatenate([w["router_w"][l], jnp.zeros((D_MODEL, LANES - N_EXPERTS), F32)], axis=1)
    return dict(
        w_mla=w_mla.astype(BF16), q_norm=row(w["mla_q_norm"][l]), kv_norm=row(w["mla_kv_norm"][l]), w_q=w_q.astype(BF16),
        w_uk=wkv[:, :, :QK_NOPE].reshape(KV_LORA, -1).astype(BF16),
        w_uvt=wkv[:, :, QK_NOPE:].reshape(KV_LORA, -1).T.astype(BF16),
        w_rwkv=w_in[:, o_r:o_s].astype(BF16), mu=row(w["rwkv_mu"][l]), w0=row(w["rwkv_w0"][l]), a0=row(w["rwkv_a0"][l]),
        k_k=row(w["rwkv_k_k"][l]), k_a=row(w["rwkv_k_a"][l]), r_k=row(w["rwkv_r_k"][l]),
        w_w2=w["rwkv_w_w2"][l].astype(BF16), w_a2=w["rwkv_w_a2"][l].astype(BF16), w_g2=w["rwkv_w_g2"][l].astype(BF16),
        head_sum=(hid[:, None] == hid[None, :]).astype(BF16),
        gn_g=row(w["rwkv_gn_g"][l]), gn_b=row(w["rwkv_gn_b"][l]),
        w_ssm=w_ssm.astype(BF16), conv_w=w["ssm_conv_w"][l].astype(F32), conv_b=row(w["ssm_conv_b"][l]),
        dt_bias=pad8(w["ssm_dt_bias"][l]), a_neg=pad8(-jnp.exp(w["ssm_a_log"][l].astype(F32))),
        d_skip=row(jnp.repeat(w["ssm_d"][l], SSM_HEADDIM)), ssm_norm_g=row(w["ssm_norm_g"][l]),
        w_gate=w_in[:, o_g:].astype(BF16), w_branch=w["w_branch"][l].astype(BF16), w_out=w["w_out"][l].astype(BF16),
        ln1_g=row(w["ln1_g"][l]), ln1_b=row(w["ln1_b"][l]),
        router_w=rw.astype(BF16),
        router_b=jnp.concatenate([w["router_b"][l].astype(F32), jnp.full((LANES - N_EXPERTS,), NEG_BIG, F32)])[None],
        w_gu=w["expert_w_gu"], b_gu=w["expert_b_gu"][l].astype(F32)[:, None, :],
        w_down=w["expert_w_down"], b_down=w["expert_b_down"][l].astype(F32)[:, None, :], layer=l,
        ln2_g=row(w["ln2_g"][l]), ln2_b=row(w["ln2_b"][l]),
    )


def _trunk_layer(x, pos0, past_lat, past_kr, rwkv_s0, shift_prev, ssm_s0, conv_prev, p):
    b, S, _ = x.shape
    y_mla, lat, kr = _mla_branch(x, pos0, past_lat, past_kr, p)
    y_rwkv, rwkv_s, shift_new = _rwkv_branch(x, rwkv_s0, shift_prev, p)
    y_ssm, ssm_s, conv_new = _ssm_branch(x, ssm_s0, conv_prev, p)
    T = b * S
    flat = lambda t: t.reshape(T, t.shape[-1])
    h, h_bf, top_i, top_g = _merge(flat(x), flat(y_mla), flat(y_rwkv), flat(y_ssm), p)
    out = _moe(h, h_bf, top_i, top_g, p, p["layer"])
    return out.reshape(b, S, D_MODEL), (lat, kr, rwkv_s, shift_new, ssm_s, conv_new)


def kernel(x_prompt, x_sample, cache_mla_latent, cache_mla_krope, state_rwkv, state_rwkv_shift, state_ssm, state_ssm_conv, w_in, mla_q_norm, mla_kv_norm, mla_w_uq, mla_w_ukv, rwkv_mu, rwkv_w0, rwkv_w_w2, rwkv_a0, rwkv_w_a2, rwkv_w_g2, rwkv_k_k, rwkv_k_a, rwkv_r_k, rwkv_gn_g, rwkv_gn_b, ssm_conv_w, ssm_conv_b, ssm_dt_bias, ssm_a_log, ssm_d, ssm_norm_g, w_branch, w_out, ln1_g, ln1_b, router_w, router_b, expert_w_gu, expert_b_gu, expert_w_down, expert_b_down, ln2_g, ln2_b):
    w = dict(w_in=w_in, mla_q_norm=mla_q_norm, mla_kv_norm=mla_kv_norm, mla_w_uq=mla_w_uq, mla_w_ukv=mla_w_ukv,
             rwkv_mu=rwkv_mu, rwkv_w0=rwkv_w0, rwkv_w_w2=rwkv_w_w2, rwkv_a0=rwkv_a0, rwkv_w_a2=rwkv_w_a2, rwkv_w_g2=rwkv_w_g2,
             rwkv_k_k=rwkv_k_k, rwkv_k_a=rwkv_k_a, rwkv_r_k=rwkv_r_k, rwkv_gn_g=rwkv_gn_g, rwkv_gn_b=rwkv_gn_b,
             ssm_conv_w=ssm_conv_w, ssm_conv_b=ssm_conv_b, ssm_dt_bias=ssm_dt_bias, ssm_a_log=ssm_a_log, ssm_d=ssm_d,
             ssm_norm_g=ssm_norm_g, w_branch=w_branch, w_out=w_out, ln1_g=ln1_g, ln1_b=ln1_b, router_w=router_w,
             router_b=router_b, expert_w_gu=expert_w_gu, expert_b_gu=expert_b_gu, expert_w_down=expert_w_down,
             expert_b_down=expert_b_down, ln2_g=ln2_g, ln2_b=ln2_b)
    bp = x_prompt.shape[0]
    past_len = cache_mla_latent.shape[2]
    zero_rwkv = jnp.zeros((bp, RWKV_HEADS, RWKV_HEAD, RWKV_HEAD), F32)
    zero_shift = jnp.zeros((bp, 1, RWKV_COLS), F32)
    zero_ssm = jnp.zeros((bp, SSM_HEADS, SSM_HEADDIM, SSM_STATE), F32)
    zero_conv = jnp.zeros((bp, CONV_W - 1, CONV_DIM), F32)
    yp, ys = x_prompt, x_sample
    st_p = [[] for _ in range(6)]
    st_s = [[] for _ in range(6)]
    for l in range(DEPTH):
        p = _prep_layer(l, w)
        yp, new_p = _trunk_layer(yp, 0, None, None, zero_rwkv, zero_shift, zero_ssm, zero_conv, p)
        ys, new_s = _trunk_layer(ys, past_len, cache_mla_latent[l], cache_mla_krope[l], state_rwkv[l], state_rwkv_shift[l],
                                 state_ssm[l], state_ssm_conv[l], p)
        for i in range(6):
            st_p[i].append(new_p[i])
            st_s[i].append(new_s[i])
    outs_p = [jnp.stack(t, axis=0) for t in st_p]
    outs_s = [jnp.stack(t, axis=0) for t in st_s]
    return (yp, ys, *outs_p, *outs_s)
```

```python
import functools
import math

import jax
import jax.numpy as jnp
from jax import lax
from jax.experimental import pallas as pl
from jax.experimental.pallas import tpu as pltpu

F32 = jnp.float32
BF16 = jnp.bfloat16

D_MODEL = 1024
DEPTH = 4
CHUNK = 64
MLA_HEADS = 8
QK_NOPE = 64
QK_ROPE = 32
QK_DIM = QK_NOPE + QK_ROPE
V_DIM = 64
VT_ROWS = V_DIM + 16
Q_LORA = 384
KV_LORA = 256
ROPE_THETA = 10000.0
RWKV_HEADS = 8
RWKV_HEAD = 64
RWKV_DIM = RWKV_HEADS * RWKV_HEAD
W_LORA = 64
A_LORA = 64
G_LORA = 128
RWKV_GN_EPS = 64e-5
SSM_HEADS = 8
SSM_HEADDIM = 64
SSM_DIM = SSM_HEADS * SSM_HEADDIM
SSM_STATE = 64
SSM_GROUPS = 2
CONV_W = 4
CONV_DIM = SSM_DIM + 2 * SSM_GROUPS * SSM_STATE
N_BRANCH = 3
BRANCH_DIM = 512
N_EXPERTS = 32
TOP_K = 4
D_FF = 1024
SWIGLU_LIMIT = 7.0
SWIGLU_ALPHA = 1.702
DEEPNORM_ALPHA = (2.0 * DEPTH) ** 0.25
LN_EPS = 1e-5
RMS_EPS = 1e-6
MLA_COLS = Q_LORA + KV_LORA + QK_ROPE
RWKV_COLS = 3 * RWKV_DIM + W_LORA + A_LORA + G_LORA
SSM_COLS = SSM_DIM + CONV_DIM + SSM_HEADS

LANES = 128
MLA_PROJ_COLS = 768
SSM_PROJ_COLS = SSM_DIM + CONV_DIM + LANES
MOE_ROWS = 512
NEG_BIG = -1e30
EXP2_SCALE = QK_DIM ** -0.5 * math.log2(math.e)
FAST_MARGIN = 100.0
VMEM_LIMIT = 56 * 1024 * 1024


def _cparams(*sem):
    return pltpu.CompilerParams(dimension_semantics=sem, vmem_limit_bytes=VMEM_LIMIT)


def _dot(a, b):
    return jnp.dot(a.astype(BF16), b.astype(BF16), preferred_element_type=F32)


def _dot_nt(a, b):
    return lax.dot_general(a.astype(BF16), b.astype(BF16), (((1,), (1,)), ((), ())), preferred_element_type=F32)


def _dot_tn(a, b):
    return lax.dot_general(a.astype(BF16), b.astype(BF16), (((0,), (0,)), ((), ())), preferred_element_type=F32)


def _dot_split_lhs(a, m):
    a_hi = a.astype(BF16)
    a_lo = (a - a_hi.astype(F32)).astype(BF16)
    return jnp.dot(a_hi, m, preferred_element_type=F32) + jnp.dot(a_lo, m, preferred_element_type=F32)


def _dot_tn_split(a, b):
    a_hi = a.astype(BF16)
    a_lo = (a - a_hi.astype(F32)).astype(BF16)
    b_hi = b.astype(BF16)
    b_lo = (b - b_hi.astype(F32)).astype(BF16)
    dn = (((0,), (0,)), ((), ()))
    return (lax.dot_general(a_hi, b_hi, dn, preferred_element_type=F32) + lax.dot_general(a_lo, b_hi, dn, preferred_element_type=F32)
            + lax.dot_general(a_hi, b_lo, dn, preferred_element_type=F32))


def _split3(x):
    hi = x.astype(BF16)
    r1 = x - hi.astype(F32)
    mid = r1.astype(BF16)
    lo = (r1 - mid.astype(F32)).astype(BF16)
    return hi, mid, lo


def _dot_exact_lhs(m, x):
    hi, mid, lo = _split3(x)
    return (jnp.dot(m, hi, preferred_element_type=F32) + jnp.dot(m, mid, preferred_element_type=F32)
            + jnp.dot(m, lo, preferred_element_type=F32))


def _dot_nt_exact_lhs(m, x):
    dn = (((1,), (1,)), ((), ()))
    hi, mid, lo = _split3(x)
    return (lax.dot_general(m, hi, dn, preferred_element_type=F32) + lax.dot_general(m, mid, dn, preferred_element_type=F32)
            + lax.dot_general(m, lo, dn, preferred_element_type=F32))


def _sigmoid(x):
    return 1.0 / (1.0 + jnp.exp(-x))


def _softplus(x):
    return jnp.maximum(x, 0.0) + jnp.log(1.0 + jnp.exp(-jnp.abs(x)))


def _rms(x, g):
    return x * lax.rsqrt(jnp.mean(x * x, axis=-1, keepdims=True) + RMS_EPS) * g


def _layer_norm(x, g, b):
    mu = jnp.mean(x, axis=-1, keepdims=True)
    xc = x - mu
    var = jnp.mean(xc * xc, axis=-1, keepdims=True)
    return xc * lax.rsqrt(var + LN_EPS) * g + b


def _full(shape):
    return pl.BlockSpec(shape, lambda *_: (0,) * len(shape))


def _mla_prep_kernel(x_ref, w_ref, qn_ref, kvn_ref, wq_ref, cq_ref, sq_ref, ck_ref, sk_ref, q_ref, lat_ref, kr_ref):
    u = _dot(x_ref[0], w_ref[...])
    c_q = u[:, :Q_LORA]
    c_kv = u[:, Q_LORA:Q_LORA + KV_LORA]
    kr = u[:, MLA_COLS - QK_ROPE:MLA_COLS]
    kr_rot = u[:, MLA_COLS:MLA_COLS + QK_ROPE]
    qall = _dot(_rms(c_q, qn_ref[...]), wq_ref[...])
    nope_w = MLA_HEADS * QK_NOPE
    rope_w = MLA_HEADS * QK_ROPE
    q_rope = qall[:, nope_w:nope_w + rope_w] * cq_ref[0] + qall[:, nope_w + rope_w:] * sq_ref[0]
    for h in range(MLA_HEADS):
        qh = jnp.concatenate([qall[:, h * QK_NOPE:(h + 1) * QK_NOPE], q_rope[:, h * QK_ROPE:(h + 1) * QK_ROPE]], axis=1)
        q_ref[0, h] = qh.astype(BF16)
    lat_ref[0] = _rms(c_kv, kvn_ref[...])
    kr_ref[0] = kr * ck_ref[0] + kr_rot * sk_ref[0]


def _kv_up_kernel(lat_ref, kr_ref, wk_ref, wvt_ref, k_ref, vt_ref, kn_ref):
    lat = lat_ref[0].astype(BF16)
    kn = jnp.dot(lat, wk_ref[...], preferred_element_type=F32)
    vt = _dot_nt(wvt_ref[...], lat)
    kr = kr_ref[0]
    ts = kn.shape[0]
    ones_rows = (lax.broadcasted_iota(jnp.int32, (VT_ROWS - V_DIM, ts), 0) == 0).astype(F32)
    ones = jnp.ones((8, QK_DIM), BF16)
    norms = []
    for h in range(MLA_HEADS):
        kh = jnp.concatenate([kn[:, h * QK_NOPE:(h + 1) * QK_NOPE], kr], axis=1).astype(BF16)
        k_ref[0, h] = kh
        khf = kh.astype(F32)
        sq = 1.02 * _dot_nt(ones, khf * khf)[0:1, :]
        norms.append(jnp.broadcast_to(jnp.max(sq, axis=1, keepdims=True), (1, LANES)))
        vt_ref[0, h] = jnp.concatenate([vt[h * V_DIM:(h + 1) * V_DIM], ones_rows], axis=0).astype(BF16)
    kn_ref[0, 0] = jnp.concatenate(norms, axis=0)


def _last_kv_block(qi, tq, tk, q_off, nk):
    q_hi = q_off + qi * tq + tq - 1
    return min(nk - 1, ((q_hi // CHUNK) * CHUNK + CHUNK - 1) // tk)


def _flash_kernel(qi_ref, kj_ref, last_ref, kmax_ref, q_ref, k_ref, vt_ref, o_ref, m_sc, mrun_sc, qn_sc, acc_sc, ok_sc,
                  *, tq, tk, q_off, sk_valid, nk):
    bi = pl.program_id(0)
    t = pl.program_id(1)
    qi = qi_ref[t]
    kj = kj_ref[t]

    @pl.when(kj == 0)
    def _():
        m_sc[...] = jnp.full(m_sc.shape, NEG_BIG, F32)
        mrun_sc[...] = jnp.full(mrun_sc.shape, NEG_BIG, F32)
        acc_sc[...] = jnp.zeros(acc_sc.shape, F32)
        ones = jnp.ones((8, QK_DIM), BF16)
        for h in range(MLA_HEADS):
            qf = q_ref[0, h].astype(F32)
            qn_sc[h] = jnp.sqrt(1.02 * _dot_nt(ones, qf * qf)[0:1, :])

    q_lo = q_off + qi * tq
    k_lo = kj * tk
    k_hi = k_lo + tk - 1
    full = jnp.logical_and((k_hi // CHUNK) <= (q_lo // CHUNK), k_hi < sk_valid)
    fast = jnp.logical_and(kj > 0, ok_sc[0] == 1)
    kj_next = jnp.minimum(kj + 1, nk - 1)

    def body(masked, lagged):
        if masked:
            key = k_lo + lax.broadcasted_iota(jnp.int32, (tk, tq), 0)
            qry = q_lo + lax.broadcasted_iota(jnp.int32, (tk, tq), 1)
            mask = jnp.logical_and((key // CHUNK) <= (qry // CHUNK), key < sk_valid)
        excess = None
        s_next = _dot_nt(k_ref[0, 0], q_ref[0, 0])
        for h in range(MLA_HEADS):
            s = s_next
            if h + 1 < MLA_HEADS:
                s_next = _dot_nt(k_ref[0, h + 1], q_ref[0, h + 1])
            if masked:
                s = jnp.where(mask, s, NEG_BIG)
            m_ref_old = m_sc[h]
            m_run = mrun_sc[h]
            m_blk = jnp.max(s, axis=0, keepdims=True)
            m_run_new = jnp.maximum(m_run, m_blk)
            m_ref_new = m_run if lagged else m_run_new
            p = jnp.exp2((s - m_ref_new) * EXP2_SCALE).astype(BF16)
            if masked:
                p = jnp.where(mask, p, jnp.zeros_like(p))
            alpha = jnp.exp2((m_ref_old - m_ref_new) * EXP2_SCALE)
            acc_sc[h] = alpha * acc_sc[h] + jnp.dot(vt_ref[0, h], p, preferred_element_type=F32)
            m_sc[h] = m_ref_new
            mrun_sc[h] = m_run_new
            ex = (qn_sc[h] * kmax_ref[(bi * MLA_HEADS + h) * nk + kj_next] - m_run_new) * EXP2_SCALE
            excess = ex if excess is None else jnp.maximum(excess, ex)
        ok_sc[0] = (jnp.max(excess) <= FAST_MARGIN).astype(jnp.int32)

    for masked in (False, True):
        for lagged in (False, True):
            cond = jnp.logical_and(full != masked, fast == lagged)
            pl.when(cond)(functools.partial(body, masked, lagged))

    @pl.when(last_ref[t] == 1)
    def _():
        o_ref[0] = jnp.concatenate([(acc_sc[h, :V_DIM] / acc_sc[h, V_DIM:V_DIM + 1]).T for h in range(MLA_HEADS)],
                                   axis=1).astype(o_ref.dtype)


def _rot_half(w):
    half = w.shape[-1] // 2
    return jnp.concatenate([-w[..., half:], w[..., :half]], axis=-1)


def _rope_tables(pos0, S):
    half = QK_ROPE // 2
    inv_freq = ROPE_THETA ** (-jnp.arange(half, dtype=F32) / half)
    ang = (pos0 + jnp.arange(S, dtype=jnp.int32)).astype(F32)[:, None] * inv_freq[None, :]
    cos = jnp.concatenate([jnp.cos(ang), jnp.cos(ang)], axis=-1)[None]
    sin = jnp.concatenate([jnp.sin(ang), jnp.sin(ang)], axis=-1)[None]
    return cos, sin, jnp.tile(cos, (1, 1, MLA_HEADS)), jnp.tile(sin, (1, 1, MLA_HEADS))


def _mla_branch(x, pos0, past_lat, past_kr, p):
    b, S, _ = x.shape
    ts = min(512, S)
    cos_k, sin_k, cos_q, sin_q = _rope_tables(pos0, S)
    q, lat, kr = pl.pallas_call(
        _mla_prep_kernel,
        grid=(b, S // ts),
        in_specs=[
            pl.BlockSpec((1, ts, D_MODEL), lambda bi, si: (bi, si, 0)),
            _full((D_MODEL, MLA_PROJ_COLS)), _full((1, Q_LORA)), _full((1, KV_LORA)),
            _full((Q_LORA, MLA_HEADS * (QK_NOPE + 2 * QK_ROPE))),
            pl.BlockSpec((1, ts, MLA_HEADS * QK_ROPE), lambda bi, si: (0, si, 0)),
            pl.BlockSpec((1, ts, MLA_HEADS * QK_ROPE), lambda bi, si: (0, si, 0)),
            pl.BlockSpec((1, ts, QK_ROPE), lambda bi, si: (0, si, 0)),
            pl.BlockSpec((1, ts, QK_ROPE), lambda bi, si: (0, si, 0)),
        ],
        out_specs=[
            pl.BlockSpec((1, MLA_HEADS, ts, QK_DIM), lambda bi, si: (bi, 0, si, 0)),
            pl.BlockSpec((1, ts, KV_LORA), lambda bi, si: (bi, si, 0)),
            pl.BlockSpec((1, ts, QK_ROPE), lambda bi, si: (bi, si, 0)),
        ],
        out_shape=[
            jax.ShapeDtypeStruct((b, MLA_HEADS, S, QK_DIM), BF16),
            jax.ShapeDtypeStruct((b, S, KV_LORA), F32),
            jax.ShapeDtypeStruct((b, S, QK_ROPE), F32),
        ],
        compiler_params=_cparams("parallel", "parallel"),
        name="mla_prep",
    )(x, p["w_mla"], p["q_norm"], p["kv_norm"], p["w_q"], cos_q, sin_q, cos_k, sin_k)

    if past_lat is None:
        lat_all, kr_all, q_off, sk_valid = lat, kr, 0, S
        tk = min(512, S)
    else:
        past_len = past_lat.shape[1]
        sk_valid = past_len + S
        tk = -(-sk_valid // LANES) * LANES
        pad = tk - sk_valid
        lat_all = jnp.concatenate([past_lat, lat, jnp.zeros((b, pad, KV_LORA), F32)], axis=1)
        kr_all = jnp.concatenate([past_kr, kr, jnp.zeros((b, pad, QK_ROPE), F32)], axis=1)
        q_off = past_len
    Sk = lat_all.shape[1]
    tku = tk
    k, vt, ksq = pl.pallas_call(
        _kv_up_kernel,
        grid=(b, Sk // tku),
        in_specs=[
            pl.BlockSpec((1, tku, KV_LORA), lambda bi, si: (bi, si, 0)),
            pl.BlockSpec((1, tku, QK_ROPE), lambda bi, si: (bi, si, 0)),
            _full((KV_LORA, MLA_HEADS * QK_NOPE)), _full((MLA_HEADS * V_DIM, KV_LORA)),
        ],
        out_specs=[
            pl.BlockSpec((1, MLA_HEADS, tku, QK_DIM), lambda bi, si: (bi, 0, si, 0)),
            pl.BlockSpec((1, MLA_HEADS, VT_ROWS, tku), lambda bi, si: (bi, 0, 0, si)),
            pl.BlockSpec((1, 1, MLA_HEADS, LANES), lambda bi, si: (bi, si, 0, 0)),
        ],
        out_shape=[
            jax.ShapeDtypeStruct((b, MLA_HEADS, Sk, QK_DIM), BF16),
            jax.ShapeDtypeStruct((b, MLA_HEADS, VT_ROWS, Sk), BF16),
            jax.ShapeDtypeStruct((b, Sk // tku, MLA_HEADS, LANES), F32),
        ],
        compiler_params=_cparams("parallel", "parallel"),
        name="mla_kv_up",
    )(lat_all, kr_all, p["w_uk"], p["w_uvt"])

    tq = min(512, S)
    nq, nk = S // tq, Sk // tk
    pairs = [(qi, kj) for qi in range(nq) for kj in range(_last_kv_block(qi, tq, tk, q_off, nk) + 1)]
    qi_tab = jnp.asarray([pq for pq, _ in pairs], jnp.int32)
    kj_tab = jnp.asarray([pk for _, pk in pairs], jnp.int32)
    last_tab = jnp.asarray([int(pk == _last_kv_block(pq, tq, tk, q_off, nk)) for pq, pk in pairs], jnp.int32)
    kmax = jnp.sqrt(jnp.swapaxes(ksq[:, :, :, 0], 1, 2)).reshape(-1)
    y = pl.pallas_call(
        functools.partial(_flash_kernel, tq=tq, tk=tk, q_off=q_off, sk_valid=sk_valid, nk=nk),
        grid_spec=pltpu.PrefetchScalarGridSpec(
            num_scalar_prefetch=4,
            grid=(b, len(pairs)),
            in_specs=[
                pl.BlockSpec((1, MLA_HEADS, tq, QK_DIM), lambda bi, t, qt, kt, lt, km: (bi, 0, qt[t], 0)),
                pl.BlockSpec((1, MLA_HEADS, tk, QK_DIM), lambda bi, t, qt, kt, lt, km: (bi, 0, kt[t], 0)),
                pl.BlockSpec((1, MLA_HEADS, VT_ROWS, tk), lambda bi, t, qt, kt, lt, km: (bi, 0, 0, kt[t])),
            ],
            out_specs=pl.BlockSpec((1, tq, MLA_HEADS * V_DIM), lambda bi, t, qt, kt, lt, km: (bi, qt[t], 0)),
            scratch_shapes=[
                pltpu.VMEM((MLA_HEADS, 1, tq), F32),
                pltpu.VMEM((MLA_HEADS, 1, tq), F32),
                pltpu.VMEM((MLA_HEADS, 1, tq), F32),
                pltpu.VMEM((MLA_HEADS, VT_ROWS, tq), F32),
                pltpu.SMEM((1,), jnp.int32),
            ],
        ),
        out_shape=jax.ShapeDtypeStruct((b, S, MLA_HEADS * V_DIM), BF16),
        compiler_params=_cparams("parallel", "arbitrary"),
        name="mla_flash",
    )(qi_tab, kj_tab, last_tab, kmax, q, k, vt)
    return y, lat, kr


def _rwkv_prep_kernel(x_ref, sp_ref, w_ref, mu_ref, w0_ref, a0_ref, kk_ref, ka_ref, rk_ref, ww2_ref, wa2_ref, wg2_ref,
                      hsum_ref, r_out, lw_out, k_out, v_out, kk_out, a_out, g_out, bon_out, sh_out, prev_sc):
    si = pl.program_id(1)

    @pl.when(si == 0)
    def _():
        prev_sc[...] = sp_ref[0]

    u = _dot(x_ref[0], w_ref[...])
    ts = u.shape[0]
    row = lax.broadcasted_iota(jnp.int32, u.shape, 0)
    shifted = jnp.where(row == 0, prev_sc[...], pltpu.roll(u, 1, axis=0))
    prev_sc[...] = u[ts - 1:ts, :]
    sh_out[0] = u[ts - 1:ts, :]
    m = u + (shifted - u) * mu_ref[...]
    r = m[:, :RWKV_DIM]
    k = m[:, RWKV_DIM:2 * RWKV_DIM]
    v = m[:, 2 * RWKV_DIM:3 * RWKV_DIM]
    o = 3 * RWKV_DIM
    wl = m[:, o:o + W_LORA]
    al = m[:, o + W_LORA:o + W_LORA + A_LORA]
    gl = m[:, o + W_LORA + A_LORA:]
    d = w0_ref[...] + _dot(jnp.tanh(wl), ww2_ref[...])
    lw_out[0] = -jnp.exp(-_softplus(-d) - 0.5)
    a = _sigmoid(a0_ref[...] + _dot(al, wa2_ref[...]))
    g_out[0] = _dot(_sigmoid(gl), wg2_ref[...])
    hsum = hsum_ref[...]
    kk = k * kk_ref[...]
    kk_out[0] = kk * lax.rsqrt(_dot_split_lhs(kk * kk, hsum) + 1e-12)
    kh = k * (1.0 + (a - 1.0) * ka_ref[...])
    bon_out[0] = _dot_split_lhs(r * kh * rk_ref[...], hsum) * v
    r_out[0] = r
    k_out[0] = kh
    v_out[0] = v
    a_out[0] = a


def _rwkv_chunk_prepare(ins, c):
    r, lw, k, v, kk, a = (list(t) for t in zip(*ins))
    n = len(ins)
    L = r[0].shape[0]
    L2 = 2 * L
    lo = c["lane_lo"]

    def stack(xv):
        return jnp.concatenate([jnp.where(lo, xv, 0.0), jnp.where(lo, 0.0, xv)], axis=0)

    g = [_dot_exact_lhs(c["tri"], x) for x in lw]
    gl = [x[L - 1:L, :] for x in g]
    e_neg = [jnp.exp(-x) for x in g]
    at_s = [stack(-kk[i] * jnp.exp(g[i] - lw[i])) for i in range(n)]
    rt_s = [stack(r[i] * jnp.exp(g[i])) for i in range(n)]
    beta = [kk[i] * a[i] for i in range(n)]
    mm = []
    for i in range(n):
        bt = beta[i] * e_neg[i]
        kt = k[i] * e_neg[i]
        mm.append(_dot_nt(jnp.concatenate([at_s[i], rt_s[i]], axis=0), jnp.concatenate([bt, bt, kt, kt], axis=0)))
    nmat = [jnp.where(c["strict"], x[:L2, :L2], 0.0) for x in mm]
    mak = [jnp.where(c["strict"], x[:L2, L2:], 0.0) for x in mm]
    mrb = [jnp.where(c["incl"], x[L2:, :L2], 0.0) for x in mm]
    mrk = [jnp.where(c["incl"], x[L2:, L2:], 0.0) for x in mm]
    xinv = [c["eye2"] + x for x in nmat]
    pw = nmat
    for _ in range(int(math.log2(L)) - 1):
        pw = [_dot(x, x) for x in pw]
        xinv = [xinv[i] + _dot(xinv[i], pw[i]) for i in range(n)]
    vs = [stack(x) for x in v]
    w1 = [_dot(mak[i], vs[i]) for i in range(n)]
    au = [_dot(xinv[i], jnp.concatenate([at_s[i], w1[i]], axis=1)) for i in range(n)]
    ry = [_dot(mrb[i], au[i]) for i in range(n)]
    mv = [_dot(mrk[i], vs[i]) for i in range(n)]
    rh = [rt_s[i] + ry[i][:, :LANES] for i in range(n)]
    yh = [ry[i][:, LANES:] + mv[i] for i in range(n)]
    bs = [stack(beta[i] * jnp.exp(gl[i] - g[i])) for i in range(n)]
    ks = [stack(k[i] * jnp.exp(gl[i] - g[i])) for i in range(n)]
    pt_lr = [_dot_tn(bs[i], au[i][:, :LANES]) for i in range(n)]
    qt = [_dot_tn(jnp.concatenate([bs[i], ks[i]], axis=0), jnp.concatenate([au[i][:, LANES:], vs[i]], axis=0)) for i in range(n)]
    return [(rh[i], yh[i], pt_lr[i], qt[i], gl[i]) for i in range(n)]


def _rwkv_scan_kernel(r_ref, lw_ref, k_ref, v_ref, kk_ref, a_ref, g_ref, bon_ref, gng_ref, gnb_ref, z0_ref,
                      y_ref, zf_ref, z_sc, *, L, n_chunks):
    si = pl.program_id(1)

    @pl.when(si == 0)
    def _():
        z_sc[...] = z0_ref[0]

    L2 = 2 * L
    ri = lax.broadcasted_iota(jnp.int32, (L2, L2), 0)
    ci = lax.broadcasted_iota(jnp.int32, (L2, L2), 1)
    same = (ri // L) == (ci // L)
    rl = lax.broadcasted_iota(jnp.int32, (L, L), 0)
    cl = lax.broadcasted_iota(jnp.int32, (L, L), 1)
    r128 = lax.broadcasted_iota(jnp.int32, (LANES, LANES), 0)
    c128 = lax.broadcasted_iota(jnp.int32, (LANES, LANES), 1)
    consts = dict(
        tri=(cl <= rl).astype(BF16),
        strict=jnp.logical_and(same, (ci % L) < (ri % L)),
        incl=jnp.logical_and(same, (ci % L) <= (ri % L)),
        eye2=(ri == ci).astype(F32),
        lane_lo=lax.broadcasted_iota(jnp.int32, (L, LANES), 1) < RWKV_HEAD,
    )
    diag = r128 == c128
    gmean = jnp.where((r128 // RWKV_HEAD) == (c128 // RWKV_HEAD), 1.0 / RWKV_HEAD, 0.0).astype(BF16)
    npair = RWKV_HEADS // 2

    def head_mean(xv):
        hi = xv.astype(BF16)
        lo = (xv - hi.astype(F32)).astype(BF16)
        return jnp.dot(hi, gmean, preferred_element_type=F32) + jnp.dot(lo, gmean, preferred_element_type=F32)

    cpi = 4 if n_chunks % 4 == 0 else (2 if n_chunks % 2 == 0 else 1)
    lanes = [slice(p * LANES, (p + 1) * LANES) for p in range(npair)]

    def chunk(ci_, carry):
        rows = [pl.ds(pl.multiple_of((ci_ * cpi + j) * L, L), L) for j in range(cpi)]
        ins = [(r_ref[0, rw, ln], lw_ref[0, rw, ln], k_ref[0, rw, ln], v_ref[0, rw, ln], kk_ref[0, rw, ln], a_ref[0, rw, ln])
               for rw in rows for ln in lanes]
        prep = _rwkv_chunk_prepare(ins, consts)
        z = [z_sc[p] for p in range(npair)]
        for j in range(cpi):
            pj = prep[j * npair:(j + 1) * npair]
            ys = [_dot(pj[p][0], z[p]) + pj[p][1] for p in range(npair)]
            zlr = [_dot(pj[p][2], z[p]) for p in range(npair)]
            y = [x[:L] + x[L:] for x in ys]
            mean = [head_mean(x) for x in y]
            yc = [y[p] - mean[p] for p in range(npair)]
            var = [head_mean(x * x) for x in yc]
            for p in range(npair):
                ln = lanes[p]
                gcol = jnp.sum(jnp.where(diag, jnp.broadcast_to(jnp.exp(pj[p][4]), (LANES, LANES)), 0.0), axis=1, keepdims=True)
                z[p] = gcol * z[p] + zlr[p] + pj[p][3]
                o = yc[p] * lax.rsqrt(var[p] + RWKV_GN_EPS) * gng_ref[:, ln] + gnb_ref[:, ln]
                y_ref[0, rows[j], ln] = ((o + bon_ref[0, rows[j], ln]) * g_ref[0, rows[j], ln]).astype(y_ref.dtype)
        for p in range(npair):
            z_sc[p] = z[p]
        return carry

    lax.fori_loop(0, n_chunks // cpi, chunk, 0)

    @pl.when(si == pl.num_programs(1) - 1)
    def _():
        zf_ref[0] = z_sc[...]


def _rwkv_branch(x, s0, shift_prev, p):
    b, S, _ = x.shape
    ts = min(512, S)
    tok = lambda w: pl.BlockSpec((1, ts, w), lambda bi, si: (bi, si, 0))
    row = lambda w: pl.BlockSpec((1, 1, w), lambda bi, si: (bi, 0, 0))
    outs = pl.pallas_call(
        _rwkv_prep_kernel,
        grid=(b, S // ts),
        in_specs=[tok(D_MODEL), row(RWKV_COLS), _full((D_MODEL, RWKV_COLS)), _full((1, RWKV_COLS))]
        + [_full((1, RWKV_DIM))] * 5
        + [_full((W_LORA, RWKV_DIM)), _full((A_LORA, RWKV_DIM)), _full((G_LORA, RWKV_DIM)), _full((RWKV_DIM, RWKV_DIM))],
        out_specs=[tok(RWKV_DIM)] * 8 + [row(RWKV_COLS)],
        out_shape=[jax.ShapeDtypeStruct((b, S, RWKV_DIM), F32)] * 8 + [jax.ShapeDtypeStruct((b, 1, RWKV_COLS), F32)],
        scratch_shapes=[pltpu.VMEM((1, RWKV_COLS), F32)],
        compiler_params=_cparams("parallel", "arbitrary"),
        name="rwkv_prep",
    )(x, shift_prev, p["w_rwkv"], p["mu"], p["w0"], p["a0"], p["k_k"], p["k_a"], p["r_k"], p["w_w2"], p["w_a2"], p["w_g2"],
      p["head_sum"])
    r, lw, kh, v, kk, a, g, bonus, shift_new = outs

    L = min(CHUNK, S)
    tb = min(4 * L, S)
    npair = RWKV_HEADS // 2
    zt = jnp.swapaxes(s0.astype(F32), 2, 3).reshape(b, npair, 2, RWKV_HEAD, RWKV_HEAD)
    z0 = jnp.einsum("bpikv,ij->bpikjv", zt, jnp.eye(2, dtype=F32)).reshape(b, npair, LANES, LANES)
    tokb = lambda: pl.BlockSpec((1, tb, RWKV_DIM), lambda bi, si: (bi, si, 0))
    zspec = pl.BlockSpec((1, npair, LANES, LANES), lambda bi, si: (bi, 0, 0, 0))
    y, zf = pl.pallas_call(
        functools.partial(_rwkv_scan_kernel, L=L, n_chunks=tb // L),
        grid=(b, S // tb),
        in_specs=[tokb() for _ in range(8)] + [_full((1, RWKV_DIM)), _full((1, RWKV_DIM)), zspec],
        out_specs=[tokb(), zspec],
        out_shape=[jax.ShapeDtypeStruct((b, S, RWKV_DIM), BF16), jax.ShapeDtypeStruct((b, npair, LANES, LANES), F32)],
        scratch_shapes=[pltpu.VMEM((npair, LANES, LANES), F32)],
        compiler_params=_cparams("parallel", "arbitrary"),
        name="rwkv_scan",
    )(r, lw, kh, v, kk, a, g, bonus, p["gn_g"], p["gn_b"], z0)
    zd = jnp.einsum("bpikiv->bpikv", zf.reshape(b, npair, 2, RWKV_HEAD, 2, RWKV_HEAD))
    s_new = jnp.swapaxes(zd.reshape(b, RWKV_HEADS, RWKV_HEAD, RWKV_HEAD), 2, 3)
    return y, s_new, shift_new


def _ssd_kernel(x_ref, w_ref, s0_ref, cp_ref, cw_ref, cb_ref, dtb_ref, a_ref, dsk_ref, ng_ref,
                y_ref, sf_ref, ct_ref, st_sc, tail_sc, *, L):
    si = pl.program_id(1)

    @pl.when(si == 0)
    def _():
        st_sc[...] = s0_ref[0]
        tail_sc[...] = cp_ref[0]

    u = _dot(x_ref[0], w_ref[...])
    z = u[:, :SSM_DIM]
    xbc = u[:, SSM_DIM:SSM_DIM + CONV_DIM]
    dtr = u[:, SSM_DIM + CONV_DIM:]
    tail = tail_sc[...]
    row = lax.broadcasted_iota(jnp.int32, xbc.shape, 0)
    sh1 = jnp.where(row == 0, tail[7:8], pltpu.roll(xbc, 1, axis=0))
    sh2 = jnp.where(row == 0, tail[6:7], pltpu.roll(sh1, 1, axis=0))
    sh3 = jnp.where(row == 0, tail[5:6], pltpu.roll(sh2, 1, axis=0))
    tail_sc[...] = xbc[L - 8:L]
    ct_ref[0] = xbc[L - 8:L]
    conv = cb_ref[...] + cw_ref[3:4] * xbc + cw_ref[2:3] * sh1 + cw_ref[1:2] * sh2 + cw_ref[0:1] * sh3
    act = conv * _sigmoid(conv)
    xs = act[:, :SSM_DIM]
    gw = SSM_STATE
    dt = _softplus(dtr + dtb_ref[...])
    a = dt * a_ref[...]
    rl = lax.broadcasted_iota(jnp.int32, (L, L), 0)
    cl = lax.broadcasted_iota(jnp.int32, (L, L), 1)
    causal = cl <= rl
    cum = _dot_exact_lhs(causal.astype(BF16), a)
    sel = (lax.broadcasted_iota(jnp.int32, (16, LANES), 0) == lax.broadcasted_iota(jnp.int32, (16, LANES), 1)).astype(BF16)
    cum_t = _dot_nt_exact_lhs(sel, cum)
    dt_t = _dot_nt_exact_lhs(sel, dt)
    ys = []
    hpg = SSM_HEADS // SSM_GROUPS
    for gi in range(SSM_GROUPS):
        bg = act[:, SSM_DIM + gi * gw:SSM_DIM + (gi + 1) * gw]
        cg = act[:, SSM_DIM + SSM_GROUPS * gw + gi * gw:SSM_DIM + SSM_GROUPS * gw + (gi + 1) * gw]
        cb = _dot_nt(cg, bg)
        for h in range(gi * hpg, (gi + 1) * hpg):
            xh = xs[:, h * SSM_HEADDIM:(h + 1) * SSM_HEADDIM]
            cc = cum[:, h:h + 1]
            seg = cc - cum_t[h:h + 1, :]
            dec = jnp.where(causal, jnp.exp(jnp.minimum(seg, 0.0)), 0.0)
            sc = cb * dec * dt_t[h:h + 1, :]
            st = st_sc[h]
            yh = _dot(sc, xh) + _dot_nt(cg, st) * jnp.exp(cc)
            clast = cum[L - 1:L, h:h + 1]
            wcol = jnp.exp(clast - cc) * dt[:, h:h + 1]
            st_sc[h] = st * jnp.exp(clast) + _dot_tn_split(xh * wcol, bg)
            ys.append(yh)
    y = jnp.concatenate(ys, axis=1) + dsk_ref[...] * xs
    y = y * (z * _sigmoid(z))
    gdim = SSM_DIM // SSM_GROUPS
    outs = []
    for gi in range(SSM_GROUPS):
        yg = y[:, gi * gdim:(gi + 1) * gdim]
        outs.append(yg * lax.rsqrt(jnp.mean(yg * yg, axis=-1, keepdims=True) + RMS_EPS))
    y_ref[0] = (jnp.concatenate(outs, axis=1) * ng_ref[...]).astype(y_ref.dtype)

    @pl.when(si == pl.num_programs(1) - 1)
    def _():
        sf_ref[0] = st_sc[...]


def _ssm_branch(x, s0, conv_prev, p):
    b, S, _ = x.shape
    L = min(256, S)
    cp = jnp.concatenate([jnp.zeros((b, 8 - (CONV_W - 1), CONV_DIM), F32), conv_prev.astype(F32)], axis=1)
    sspec = pl.BlockSpec((1, SSM_HEADS, SSM_HEADDIM, SSM_STATE), lambda bi, si: (bi, 0, 0, 0))
    cspec = pl.BlockSpec((1, 8, CONV_DIM), lambda bi, si: (bi, 0, 0))
    y, s_new, ctail = pl.pallas_call(
        functools.partial(_ssd_kernel, L=L),
        grid=(b, S // L),
        in_specs=[
            pl.BlockSpec((1, L, D_MODEL), lambda bi, si: (bi, si, 0)),
            _full((D_MODEL, SSM_PROJ_COLS)), sspec, cspec,
            _full((CONV_W, CONV_DIM)), _full((1, CONV_DIM)), _full((1, LANES)), _full((1, LANES)),
            _full((1, SSM_DIM)), _full((1, SSM_DIM)),
        ],
        out_specs=[pl.BlockSpec((1, L, SSM_DIM), lambda bi, si: (bi, si, 0)), sspec, cspec],
        out_shape=[
            jax.ShapeDtypeStruct((b, S, SSM_DIM), BF16),
            jax.ShapeDtypeStruct((b, SSM_HEADS, SSM_HEADDIM, SSM_STATE), F32),
            jax.ShapeDtypeStruct((b, 8, CONV_DIM), F32),
        ],
        scratch_shapes=[pltpu.VMEM((SSM_HEADS, SSM_HEADDIM, SSM_STATE), F32), pltpu.VMEM((8, CONV_DIM), F32)],
        compiler_params=_cparams("parallel", "arbitrary"),
        name="ssd",
    )(x, p["w_ssm"], s0.astype(F32), cp, p["conv_w"], p["conv_b"], p["dt_bias"], p["a_neg"], p["d_skip"], p["ssm_norm_g"])
    return y, s_new, ctail[:, 8 - (CONV_W - 1):]


def _merge_kernel(x_ref, ym_ref, yr_ref, ys_ref, wg_ref, wb_ref, wo_ref, g_ref, b_ref, rw_ref, rb_ref,
                  h_ref, hb_ref, ti_ref, tg_ref):
    x = x_ref[...]
    gates = _sigmoid(_dot(x, wg_ref[...]))
    mix = gates[:, :D_MODEL] * _dot(ym_ref[...], wb_ref[0])
    mix = mix + gates[:, D_MODEL:2 * D_MODEL] * _dot(yr_ref[...], wb_ref[1])
    mix = mix + gates[:, 2 * D_MODEL:] * _dot(ys_ref[...], wb_ref[2])
    h = _layer_norm(DEEPNORM_ALPHA * x + _dot(mix, wo_ref[...]), g_ref[...], b_ref[...])
    h_ref[...] = h
    hb_ref[...] = h.astype(BF16)
    logits = _dot(h, rw_ref[...]) + rb_ref[...]
    lane = lax.broadcasted_iota(jnp.int32, logits.shape, 1)
    idx_out = jnp.zeros(logits.shape, jnp.int32)
    val_out = jnp.zeros(logits.shape, F32)
    top = None
    den = None
    for kth in range(TOP_K):
        mval = jnp.max(logits, axis=-1, keepdims=True)
        midx = jnp.min(jnp.where(logits == mval, lane, LANES), axis=-1, keepdims=True)
        if kth == 0:
            top = mval
            e = jnp.ones_like(mval)
            den = e
        else:
            e = jnp.exp(mval - top)
            den = den + e
        idx_out = jnp.where(lane == kth, midx, idx_out)
        val_out = jnp.where(lane == kth, e, val_out)
        logits = jnp.where(lane == midx, NEG_BIG * 2.0, logits)
    ti_ref[...] = idx_out
    tg_ref[...] = val_out / den


def _merge(x2, ym, yr, ys, p):
    T = x2.shape[0]
    tm = min(512, T)
    tok = lambda w: pl.BlockSpec((tm, w), lambda i: (i, 0))
    return pl.pallas_call(
        _merge_kernel,
        grid=(T // tm,),
        in_specs=[tok(D_MODEL), tok(BRANCH_DIM), tok(BRANCH_DIM), tok(BRANCH_DIM),
                  _full((D_MODEL, N_BRANCH * D_MODEL)), _full((N_BRANCH, BRANCH_DIM, D_MODEL)), _full((D_MODEL, D_MODEL)),
                  _full((1, D_MODEL)), _full((1, D_MODEL)), _full((D_MODEL, LANES)), _full((1, LANES))],
        out_specs=[tok(D_MODEL), tok(D_MODEL), tok(LANES), tok(LANES)],
        out_shape=[jax.ShapeDtypeStruct((T, D_MODEL), F32), jax.ShapeDtypeStruct((T, D_MODEL), BF16),
                   jax.ShapeDtypeStruct((T, LANES), jnp.int32), jax.ShapeDtypeStruct((T, LANES), F32)],
        compiler_params=_cparams("parallel"),
        name="merge_router",
    )(x2, ym, yr, ys, p["w_gate"], p["w_branch"], p["w_out"], p["ln1_g"], p["ln1_b"], p["router_w"], p["router_b"])


def _expert_kernel(be_ref, nu_ref, x_ref, wgu_ref, bgu_ref, wd_ref, bd_ref, o_ref, wgu_sc, wd_sc):
    i = pl.program_id(0)
    used = i < nu_ref[0]
    new_expert = jnp.logical_or(i == 0, be_ref[i] != be_ref[jnp.maximum(i - 1, 0)])

    @pl.when(jnp.logical_and(used, new_expert))
    def _():
        wgu_sc[...] = wgu_ref[0, 0].astype(BF16)
        wd_sc[...] = wd_ref[0, 0].astype(BF16)

    @pl.when(used)
    def _():
        hgu = jnp.dot(x_ref[...], wgu_sc[...], preferred_element_type=F32) + bgu_ref[0]
        gate = jnp.minimum(hgu[:, :D_FF], SWIGLU_LIMIT)
        up = jnp.clip(hgu[:, D_FF:], -SWIGLU_LIMIT, SWIGLU_LIMIT)
        hid = gate * _sigmoid(SWIGLU_ALPHA * gate) * (up + 1.0)
        o_ref[...] = (jnp.dot(hid.astype(BF16), wd_sc[...], preferred_element_type=F32) + bd_ref[0]).astype(o_ref.dtype)

    @pl.when(jnp.logical_not(used))
    def _():
        o_ref[...] = jnp.zeros(o_ref.shape, o_ref.dtype)


def _combine_kernel(y0_ref, y1_ref, y2_ref, y3_ref, tg_ref, h_ref, g_ref, b_ref, o_ref):
    tg = tg_ref[...]
    f = tg[:, 0:1] * y0_ref[...].astype(F32)
    for kth, y_ref in ((1, y1_ref), (2, y2_ref), (3, y3_ref)):
        f = f + tg[:, kth:kth + 1] * y_ref[...].astype(F32)
    o_ref[...] = _layer_norm(DEEPNORM_ALPHA * h_ref[...] + f, g_ref[...], b_ref[...])


def _moe_experts(h_bf, top_i, p, layer):
    T = h_bf.shape[0]
    n_assign = T * TOP_K
    flat_e = top_i[:, :TOP_K].reshape(-1)
    ids = jnp.arange(n_assign, dtype=jnp.int32)
    skey = jnp.sort(flat_e * n_assign + ids)
    sorted_e = skey // n_assign
    order = skey - sorted_e * n_assign
    experts = jnp.arange(N_EXPERTS, dtype=jnp.int32)
    counts = jnp.sum((flat_e[:, None] == experts[None, :]).astype(jnp.int32), axis=0)
    padded = (counts + MOE_ROWS - 1) // MOE_ROWS * MOE_ROWS
    pad_end = jnp.cumsum(padded)
    pad_start = pad_end - padded
    start = jnp.cumsum(counts) - counts
    n_blocks = -(-(n_assign + N_EXPERTS * (MOE_ROWS - 1)) // MOE_ROWS)
    n_rows = n_blocks * MOE_ROWS
    blk_row0 = jnp.arange(n_blocks, dtype=jnp.int32) * MOE_ROWS
    block_e = jnp.minimum(jnp.sum((pad_end[None, :] <= blk_row0[:, None]).astype(jnp.int32), axis=1), N_EXPERTS - 1)
    n_used = (pad_end[-1:] // MOE_ROWS).astype(jnp.int32)
    within = (blk_row0 - pad_start[block_e])[:, None] + jnp.arange(MOE_ROWS, dtype=jnp.int32)[None, :]
    valid = within < counts[block_e][:, None]
    src = jnp.clip(start[block_e][:, None] + within, 0, n_assign - 1)
    own_row = (blk_row0[:, None] + jnp.arange(MOE_ROWS, dtype=jnp.int32)[None, :]) % T
    row_tok = jnp.where(valid, order[src.reshape(-1)].reshape(n_blocks, MOE_ROWS) // TOP_K, own_row).reshape(-1)
    xb = h_bf[row_tok]

    yb = pl.pallas_call(
        _expert_kernel,
        grid_spec=pltpu.PrefetchScalarGridSpec(
            num_scalar_prefetch=2,
            grid=(n_blocks,),
            in_specs=[
                pl.BlockSpec((MOE_ROWS, D_MODEL), lambda i, be, nu: (i, 0)),
                pl.BlockSpec((1, 1, D_MODEL, 2 * D_FF), lambda i, be, nu: (layer, be[i], 0, 0)),
                pl.BlockSpec((1, 1, 2 * D_FF), lambda i, be, nu: (be[i], 0, 0)),
                pl.BlockSpec((1, 1, D_FF, D_MODEL), lambda i, be, nu: (layer, be[i], 0, 0)),
                pl.BlockSpec((1, 1, D_MODEL), lambda i, be, nu: (be[i], 0, 0)),
            ],
            out_specs=pl.BlockSpec((MOE_ROWS, D_MODEL), lambda i, be, nu: (i, 0)),
            scratch_shapes=[pltpu.VMEM((D_MODEL, 2 * D_FF), BF16), pltpu.VMEM((D_FF, D_MODEL), BF16)],
        ),
        out_shape=jax.ShapeDtypeStruct((n_rows, D_MODEL), BF16),
        compiler_params=_cparams("arbitrary"),
        name="moe_experts",
    )(block_e, n_used, xb, p["w_gu"], p["b_gu"], p["w_down"], p["b_down"])

    dest_sorted = pad_start[sorted_e] + (ids - start[sorted_e])
    _, dest = lax.sort((order, dest_sorted), num_keys=1)
    return yb, dest.reshape(T, TOP_K)


def _moe_combine(yb, dest, top_g, h, p):
    T = h.shape[0]
    ygs = [yb[dest[:, kth]] for kth in range(TOP_K)]
    tm = min(512, T)
    tok = lambda w: pl.BlockSpec((tm, w), lambda i: (i, 0))
    return pl.pallas_call(
        _combine_kernel,
        grid=(T // tm,),
        in_specs=[tok(D_MODEL)] * TOP_K + [tok(LANES), tok(D_MODEL), _full((1, D_MODEL)), _full((1, D_MODEL))],
        out_specs=tok(D_MODEL),
        out_shape=jax.ShapeDtypeStruct((T, D_MODEL), F32),
        compiler_params=_cparams("parallel"),
        name="moe_combine_ln",
    )(*ygs, top_g, h, p["ln2_g"], p["ln2_b"])


def _prep_layer(l, w):
    w_in = w["w_in"][l]
    kr_cols = w_in[:, MLA_COLS - QK_ROPE:MLA_COLS]
    w_mla = jnp.concatenate([w_in[:, :MLA_COLS], _rot_half(kr_cols),
                             jnp.zeros((D_MODEL, MLA_PROJ_COLS - MLA_COLS - QK_ROPE), F32)], axis=1)
    wq = w["mla_w_uq"][l].reshape(Q_LORA, MLA_HEADS, QK_DIM)
    wq_rope = wq[:, :, QK_NOPE:]
    w_q = jnp.concatenate([wq[:, :, :QK_NOPE].reshape(Q_LORA, -1), wq_rope.reshape(Q_LORA, -1),
                           _rot_half(wq_rope).reshape(Q_LORA, -1)], axis=1)
    wkv = w["mla_w_ukv"][l].reshape(KV_LORA, MLA_HEADS, QK_NOPE + V_DIM)
    o_r = MLA_COLS
    o_s = o_r + RWKV_COLS
    o_g = o_s + SSM_COLS
    w_ssm = jnp.concatenate([w_in[:, o_s:o_s + SSM_DIM + CONV_DIM], w_in[:, o_s + SSM_DIM + CONV_DIM:o_g],
                             jnp.zeros((D_MODEL, LANES - SSM_HEADS), F32)], axis=1)
    pad8 = lambda v: jnp.concatenate([v.astype(F32), jnp.zeros((LANES - SSM_HEADS,), F32)])[None]
    row = lambda v: v.astype(F32)[None]
    hid = jnp.arange(RWKV_DIM) // RWKV_HEAD
    rw = jnp.concatenate([w["router_w"][l], jnp.zeros((D_MODEL, LANES - N_EXPERTS), F32)], axis=1)
    return dict(
        w_mla=w_mla.astype(BF16), q_norm=row(w["mla_q_norm"][l]), kv_norm=row(w["mla_kv_norm"][l]), w_q=w_q.astype(BF16),
        w_uk=wkv[:, :, :QK_NOPE].reshape(KV_LORA, -1).astype(BF16),
        w_uvt=wkv[:, :, QK_NOPE:].reshape(KV_LORA, -1).T.astype(BF16),
        w_rwkv=w_in[:, o_r:o_s].astype(BF16), mu=row(w["rwkv_mu"][l]), w0=row(w["rwkv_w0"][l]), a0=row(w["rwkv_a0"][l]),
        k_k=row(w["rwkv_k_k"][l]), k_a=row(w["rwkv_k_a"][l]), r_k=row(w["rwkv_r_k"][l]),
        w_w2=w["rwkv_w_w2"][l].astype(BF16), w_a2=w["rwkv_w_a2"][l].astype(BF16), w_g2=w["rwkv_w_g2"][l].astype(BF16),
        head_sum=(hid[:, None] == hid[None, :]).astype(BF16),
        gn_g=row(w["rwkv_gn_g"][l]), gn_b=row(w["rwkv_gn_b"][l]),
        w_ssm=w_ssm.astype(BF16), conv_w=w["ssm_conv_w"][l].astype(F32), conv_b=row(w["ssm_conv_b"][l]),
        dt_bias=pad8(w["ssm_dt_bias"][l]), a_neg=pad8(-jnp.exp(w["ssm_a_log"][l].astype(F32))),
        d_skip=row(jnp.repeat(w["ssm_d"][l], SSM_HEADDIM)), ssm_norm_g=row(w["ssm_norm_g"][l]),
        w_gate=w_in[:, o_g:].astype(BF16), w_branch=w["w_branch"][l].astype(BF16), w_out=w["w_out"][l].astype(BF16),
        ln1_g=row(w["ln1_g"][l]), ln1_b=row(w["ln1_b"][l]),
        router_w=rw.astype(BF16),
        router_b=jnp.concatenate([w["router_b"][l].astype(F32), jnp.full((LANES - N_EXPERTS,), NEG_BIG, F32)])[None],
        w_gu=w["expert_w_gu"], b_gu=w["expert_b_gu"][l].astype(F32)[:, None, :],
        w_down=w["expert_w_down"], b_down=w["expert_b_down"][l].astype(F32)[:, None, :], layer=l,
        ln2_g=row(w["ln2_g"][l]), ln2_b=row(w["ln2_b"][l]),
    )


def _mixers_and_router(x, pos0, past_lat, past_kr, rwkv_s0, shift_prev, ssm_s0, conv_prev, p):
    b, S, _ = x.shape
    y_mla, lat, kr = _mla_branch(x, pos0, past_lat, past_kr, p)
    y_rwkv, rwkv_s, shift_new = _rwkv_branch(x, rwkv_s0, shift_prev, p)
    y_ssm, ssm_s, conv_new = _ssm_branch(x, ssm_s0, conv_prev, p)
    T = b * S
    flat = lambda t: t.reshape(T, t.shape[-1])
    routed = _merge(flat(x), flat(y_mla), flat(y_rwkv), flat(y_ssm), p)
    return routed, (lat, kr, rwkv_s, shift_new, ssm_s, conv_new)


def _trunk_layer(x, pos0, past_lat, past_kr, rwkv_s0, shift_prev, ssm_s0, conv_prev, p):
    (h, h_bf, top_i, top_g), states = _mixers_and_router(x, pos0, past_lat, past_kr, rwkv_s0, shift_prev, ssm_s0, conv_prev, p)
    yb, dest = _moe_experts(h_bf, top_i, p, p["layer"])
    return _moe_combine(yb, dest, top_g, h, p).reshape(x.shape), states


def kernel(x_prompt, x_sample, cache_mla_latent, cache_mla_krope, state_rwkv, state_rwkv_shift, state_ssm, state_ssm_conv, w_in, mla_q_norm, mla_kv_norm, mla_w_uq, mla_w_ukv, rwkv_mu, rwkv_w0, rwkv_w_w2, rwkv_a0, rwkv_w_a2, rwkv_w_g2, rwkv_k_k, rwkv_k_a, rwkv_r_k, rwkv_gn_g, rwkv_gn_b, ssm_conv_w, ssm_conv_b, ssm_dt_bias, ssm_a_log, ssm_d, ssm_norm_g, w_branch, w_out, ln1_g, ln1_b, router_w, router_b, expert_w_gu, expert_b_gu, expert_w_down, expert_b_down, ln2_g, ln2_b):
    w = dict(w_in=w_in, mla_q_norm=mla_q_norm, mla_kv_norm=mla_kv_norm, mla_w_uq=mla_w_uq, mla_w_ukv=mla_w_ukv,
             rwkv_mu=rwkv_mu, rwkv_w0=rwkv_w0, rwkv_w_w2=rwkv_w_w2, rwkv_a0=rwkv_a0, rwkv_w_a2=rwkv_w_a2, rwkv_w_g2=rwkv_w_g2,
             rwkv_k_k=rwkv_k_k, rwkv_k_a=rwkv_k_a, rwkv_r_k=rwkv_r_k, rwkv_gn_g=rwkv_gn_g, rwkv_gn_b=rwkv_gn_b,
             ssm_conv_w=ssm_conv_w, ssm_conv_b=ssm_conv_b, ssm_dt_bias=ssm_dt_bias, ssm_a_log=ssm_a_log, ssm_d=ssm_d,
             ssm_norm_g=ssm_norm_g, w_branch=w_branch, w_out=w_out, ln1_g=ln1_g, ln1_b=ln1_b, router_w=router_w,
             router_b=router_b, expert_w_gu=expert_w_gu, expert_b_gu=expert_b_gu, expert_w_down=expert_w_down,
             expert_b_down=expert_b_down, ln2_g=ln2_g, ln2_b=ln2_b)
    bp = x_prompt.shape[0]
    past_len = cache_mla_latent.shape[2]
    zero_rwkv = jnp.zeros((bp, RWKV_HEADS, RWKV_HEAD, RWKV_HEAD), F32)
    zero_shift = jnp.zeros((bp, 1, RWKV_COLS), F32)
    zero_ssm = jnp.zeros((bp, SSM_HEADS, SSM_HEADDIM, SSM_STATE), F32)
    zero_conv = jnp.zeros((bp, CONV_W - 1, CONV_DIM), F32)
    yp, ys = x_prompt, x_sample
    st_p = [[] for _ in range(6)]
    st_s = [[] for _ in range(6)]
    for l in range(DEPTH):
        p = _prep_layer(l, w)
        (h_p, hb_p, ti_p, tg_p), new_p = _mixers_and_router(yp, 0, None, None, zero_rwkv, zero_shift, zero_ssm, zero_conv, p)
        (h_s, hb_s, ti_s, tg_s), new_s = _mixers_and_router(ys, past_len, cache_mla_latent[l], cache_mla_krope[l], state_rwkv[l],
                                                          state_rwkv_shift[l], state_ssm[l], state_ssm_conv[l], p)
        t_p = h_p.shape[0]
        yb, dest = _moe_experts(jnp.concatenate([hb_p, hb_s], axis=0), jnp.concatenate([ti_p, ti_s], axis=0), p, l)
        yp = _moe_combine(yb, dest[:t_p], tg_p, h_p, p).reshape(yp.shape)
        ys = _moe_combine(yb, dest[t_p:], tg_s, h_s, p).reshape(ys.shape)
        for i in range(6):
            st_p[i].append(new_p[i])
            st_s[i].append(new_s[i])
    outs_p = [jnp.stack(t, axis=0) for t in st_p]
    outs_s = [jnp.stack(t, axis=0) for t in st_s]
    return (yp, ys, *outs_p, *outs_s)
```

```python
import functools
import math

import jax
import jax.numpy as jnp
from jax import lax
from jax.experimental import pallas as pl
from jax.experimental.pallas import tpu as pltpu

F32 = jnp.float32
BF16 = jnp.bfloat16

D_MODEL = 1024
DEPTH = 4
CHUNK = 64
MLA_HEADS = 8
QK_NOPE = 64
QK_ROPE = 32
QK_DIM = QK_NOPE + QK_ROPE
V_DIM = 64
VT_ROWS = V_DIM + 16
Q_LORA = 384
KV_LORA = 256
ROPE_THETA = 10000.0
RWKV_HEADS = 8
RWKV_HEAD = 64
RWKV_DIM = RWKV_HEADS * RWKV_HEAD
W_LORA = 64
A_LORA = 64
G_LORA = 128
RWKV_GN_EPS = 64e-5
SSM_HEADS = 8
SSM_HEADDIM = 64
SSM_DIM = SSM_HEADS * SSM_HEADDIM
SSM_STATE = 64
SSM_GROUPS = 2
CONV_W = 4
CONV_DIM = SSM_DIM + 2 * SSM_GROUPS * SSM_STATE
N_BRANCH = 3
BRANCH_DIM = 512
N_EXPERTS = 32
TOP_K = 4
D_FF = 1024
SWIGLU_LIMIT = 7.0
SWIGLU_ALPHA = 1.702
DEEPNORM_ALPHA = (2.0 * DEPTH) ** 0.25
LN_EPS = 1e-5
RMS_EPS = 1e-6
MLA_COLS = Q_LORA + KV_LORA + QK_ROPE
RWKV_COLS = 3 * RWKV_DIM + W_LORA + A_LORA + G_LORA
SSM_COLS = SSM_DIM + CONV_DIM + SSM_HEADS

LANES = 128
MLA_PROJ_COLS = 768
SSM_PROJ_COLS = SSM_DIM + CONV_DIM + LANES
MOE_ROWS = 512
NEG_BIG = -1e30
EXP2_SCALE = QK_DIM ** -0.5 * math.log2(math.e)
FAST_MARGIN = 100.0
VMEM_LIMIT = 56 * 1024 * 1024


def _cparams(*sem):
    return pltpu.CompilerParams(dimension_semantics=sem, vmem_limit_bytes=VMEM_LIMIT)


def _dot(a, b):
    return jnp.dot(a.astype(BF16), b.astype(BF16), preferred_element_type=F32)


def _dot_nt(a, b):
    return lax.dot_general(a.astype(BF16), b.astype(BF16), (((1,), (1,)), ((), ())), preferred_element_type=F32)


def _dot_tn(a, b):
    return lax.dot_general(a.astype(BF16), b.astype(BF16), (((0,), (0,)), ((), ())), preferred_element_type=F32)


def _dot_split_lhs(a, m):
    a_hi = a.astype(BF16)
    a_lo = (a - a_hi.astype(F32)).astype(BF16)
    return jnp.dot(a_hi, m, preferred_element_type=F32) + jnp.dot(a_lo, m, preferred_element_type=F32)


def _dot_tn_split(a, b):
    a_hi = a.astype(BF16)
    a_lo = (a - a_hi.astype(F32)).astype(BF16)
    b_hi = b.astype(BF16)
    b_lo = (b - b_hi.astype(F32)).astype(BF16)
    dn = (((0,), (0,)), ((), ()))
    return (lax.dot_general(a_hi, b_hi, dn, preferred_element_type=F32) + lax.dot_general(a_lo, b_hi, dn, preferred_element_type=F32)
            + lax.dot_general(a_hi, b_lo, dn, preferred_element_type=F32))


def _split3(x):
    hi = x.astype(BF16)
    r1 = x - hi.astype(F32)
    mid = r1.astype(BF16)
    lo = (r1 - mid.astype(F32)).astype(BF16)
    return hi, mid, lo


def _dot_exact_lhs(m, x):
    hi, mid, lo = _split3(x)
    return (jnp.dot(m, hi, preferred_element_type=F32) + jnp.dot(m, mid, preferred_element_type=F32)
            + jnp.dot(m, lo, preferred_element_type=F32))


def _dot_nt_exact_lhs(m, x):
    dn = (((1,), (1,)), ((), ()))
    hi, mid, lo = _split3(x)
    return (lax.dot_general(m, hi, dn, preferred_element_type=F32) + lax.dot_general(m, mid, dn, preferred_element_type=F32)
            + lax.dot_general(m, lo, dn, preferred_element_type=F32))


def _sigmoid(x):
    return 1.0 / (1.0 + jnp.exp(-x))


def _softplus(x):
    return jnp.maximum(x, 0.0) + jnp.log(1.0 + jnp.exp(-jnp.abs(x)))


def _rms(x, g):
    return x * lax.rsqrt(jnp.mean(x * x, axis=-1, keepdims=True) + RMS_EPS) * g


def _layer_norm(x, g, b):
    mu = jnp.mean(x, axis=-1, keepdims=True)
    xc = x - mu
    var = jnp.mean(xc * xc, axis=-1, keepdims=True)
    return xc * lax.rsqrt(var + LN_EPS) * g + b


def _full(shape):
    return pl.BlockSpec(shape, lambda *_: (0,) * len(shape))


def _mla_prep_kernel(x_ref, w_ref, qn_ref, kvn_ref, wq_ref, cq_ref, sq_ref, ck_ref, sk_ref, q_ref, lat_ref, kr_ref):
    u = _dot(x_ref[0], w_ref[...])
    c_q = u[:, :Q_LORA]
    c_kv = u[:, Q_LORA:Q_LORA + KV_LORA]
    kr = u[:, MLA_COLS - QK_ROPE:MLA_COLS]
    kr_rot = u[:, MLA_COLS:MLA_COLS + QK_ROPE]
    qall = _dot(_rms(c_q, qn_ref[...]), wq_ref[...])
    nope_w = MLA_HEADS * QK_NOPE
    rope_w = MLA_HEADS * QK_ROPE
    q_rope = qall[:, nope_w:nope_w + rope_w] * cq_ref[0] + qall[:, nope_w + rope_w:] * sq_ref[0]
    for h in range(MLA_HEADS):
        qh = jnp.concatenate([qall[:, h * QK_NOPE:(h + 1) * QK_NOPE], q_rope[:, h * QK_ROPE:(h + 1) * QK_ROPE]], axis=1)
        q_ref[0, h] = qh.astype(BF16)
    lat_ref[0] = _rms(c_kv, kvn_ref[...])
    kr_ref[0] = kr * ck_ref[0] + kr_rot * sk_ref[0]


def _kv_up_kernel(lat_ref, kr_ref, wk_ref, wvt_ref, k_ref, vt_ref, kn_ref):
    lat = lat_ref[0].astype(BF16)
    kn = jnp.dot(lat, wk_ref[...], preferred_element_type=F32)
    vt = _dot_nt(wvt_ref[...], lat)
    kr = kr_ref[0]
    ts = kn.shape[0]
    ones_rows = (lax.broadcasted_iota(jnp.int32, (VT_ROWS - V_DIM, ts), 0) == 0).astype(F32)
    ones = jnp.ones((8, QK_DIM), BF16)
    norms = []
    for h in range(MLA_HEADS):
        kh = jnp.concatenate([kn[:, h * QK_NOPE:(h + 1) * QK_NOPE], kr], axis=1).astype(BF16)
        k_ref[0, h] = kh
        khf = kh.astype(F32)
        sq = 1.02 * _dot_nt(ones, khf * khf)[0:1, :]
        norms.append(jnp.broadcast_to(jnp.max(sq, axis=1, keepdims=True), (1, LANES)))
        vt_ref[0, h] = jnp.concatenate([vt[h * V_DIM:(h + 1) * V_DIM], ones_rows], axis=0).astype(BF16)
    kn_ref[0, 0] = jnp.concatenate(norms, axis=0)


def _last_kv_block(qi, tq, tk, q_off, nk):
    q_hi = q_off + qi * tq + tq - 1
    return min(nk - 1, ((q_hi // CHUNK) * CHUNK + CHUNK - 1) // tk)


def _flash_kernel(qi_ref, kj_ref, last_ref, kmax_ref, q_ref, k_ref, vt_ref, o_ref, m_sc, mrun_sc, qn_sc, acc_sc, ok_sc,
                  *, tq, tk, q_off, sk_valid, nk):
    bi = pl.program_id(0)
    t = pl.program_id(1)
    qi = qi_ref[t]
    kj = kj_ref[t]

    @pl.when(kj == 0)
    def _():
        m_sc[...] = jnp.full(m_sc.shape, NEG_BIG, F32)
        mrun_sc[...] = jnp.full(mrun_sc.shape, NEG_BIG, F32)
        acc_sc[...] = jnp.zeros(acc_sc.shape, F32)
        ones = jnp.ones((8, QK_DIM), BF16)
        for h in range(MLA_HEADS):
            qf = q_ref[0, h].astype(F32)
            qn_sc[h] = jnp.sqrt(1.02 * _dot_nt(ones, qf * qf)[0:1, :])

    q_lo = q_off + qi * tq
    k_lo = kj * tk
    k_hi = k_lo + tk - 1
    full = jnp.logical_and((k_hi // CHUNK) <= (q_lo // CHUNK), k_hi < sk_valid)
    fast = jnp.logical_and(kj > 0, ok_sc[0] == 1)
    kj_next = jnp.minimum(kj + 1, nk - 1)

    def body(masked, lagged):
        if masked:
            key = k_lo + lax.broadcasted_iota(jnp.int32, (tk, tq), 0)
            qry = q_lo + lax.broadcasted_iota(jnp.int32, (tk, tq), 1)
            mask = jnp.logical_and((key // CHUNK) <= (qry // CHUNK), key < sk_valid)
        excess = None
        s_next = _dot_nt(k_ref[0, 0], q_ref[0, 0])
        for h in range(MLA_HEADS):
            s = s_next
            if h + 1 < MLA_HEADS:
                s_next = _dot_nt(k_ref[0, h + 1], q_ref[0, h + 1])
            if masked:
                s = jnp.where(mask, s, NEG_BIG)
            m_ref_old = m_sc[h]
            m_run = mrun_sc[h]
            m_blk = jnp.max(s, axis=0, keepdims=True)
            m_run_new = jnp.maximum(m_run, m_blk)
            m_ref_new = m_run if lagged else m_run_new
            p = jnp.exp2((s - m_ref_new) * EXP2_SCALE).astype(BF16)
            if masked:
                p = jnp.where(mask, p, jnp.zeros_like(p))
            alpha = jnp.exp2((m_ref_old - m_ref_new) * EXP2_SCALE)
            acc_sc[h] = alpha * acc_sc[h] + jnp.dot(vt_ref[0, h], p, preferred_element_type=F32)
            m_sc[h] = m_ref_new
            mrun_sc[h] = m_run_new
            ex = (qn_sc[h] * kmax_ref[(bi * MLA_HEADS + h) * nk + kj_next] - m_run_new) * EXP2_SCALE
            excess = ex if excess is None else jnp.maximum(excess, ex)
        ok_sc[0] = (jnp.max(excess) <= FAST_MARGIN).astype(jnp.int32)

    for masked in (False, True):
        for lagged in (False, True):
            cond = jnp.logical_and(full != masked, fast == lagged)
            pl.when(cond)(functools.partial(body, masked, lagged))

    @pl.when(last_ref[t] == 1)
    def _():
        o_ref[0] = jnp.concatenate([(acc_sc[h, :V_DIM] / acc_sc[h, V_DIM:V_DIM + 1]).T for h in range(MLA_HEADS)],
                                   axis=1).astype(o_ref.dtype)


def _rot_half(w):
    half = w.shape[-1] // 2
    return jnp.concatenate([-w[..., half:], w[..., :half]], axis=-1)


def _rope_tables(pos0, S):
    half = QK_ROPE // 2
    inv_freq = ROPE_THETA ** (-jnp.arange(half, dtype=F32) / half)
    ang = (pos0 + jnp.arange(S, dtype=jnp.int32)).astype(F32)[:, None] * inv_freq[None, :]
    cos = jnp.concatenate([jnp.cos(ang), jnp.cos(ang)], axis=-1)[None]
    sin = jnp.concatenate([jnp.sin(ang), jnp.sin(ang)], axis=-1)[None]
    return cos, sin, jnp.tile(cos, (1, 1, MLA_HEADS)), jnp.tile(sin, (1, 1, MLA_HEADS))


def _mla_branch(x, pos0, past_lat, past_kr, p):
    b, S, _ = x.shape
    ts = min(512, S)
    cos_k, sin_k, cos_q, sin_q = _rope_tables(pos0, S)
    q, lat, kr = pl.pallas_call(
        _mla_prep_kernel,
        grid=(b, S // ts),
        in_specs=[
            pl.BlockSpec((1, ts, D_MODEL), lambda bi, si: (bi, si, 0)),
            _full((D_MODEL, MLA_PROJ_COLS)), _full((1, Q_LORA)), _full((1, KV_LORA)),
            _full((Q_LORA, MLA_HEADS * (QK_NOPE + 2 * QK_ROPE))),
            pl.BlockSpec((1, ts, MLA_HEADS * QK_ROPE), lambda bi, si: (0, si, 0)),
            pl.BlockSpec((1, ts, MLA_HEADS * QK_ROPE), lambda bi, si: (0, si, 0)),
            pl.BlockSpec((1, ts, QK_ROPE), lambda bi, si: (0, si, 0)),
            pl.BlockSpec((1, ts, QK_ROPE), lambda bi, si: (0, si, 0)),
        ],
        out_specs=[
            pl.BlockSpec((1, MLA_HEADS, ts, QK_DIM), lambda bi, si: (bi, 0, si, 0)),
            pl.BlockSpec((1, ts, KV_LORA), lambda bi, si: (bi, si, 0)),
            pl.BlockSpec((1, ts, QK_ROPE), lambda bi, si: (bi, si, 0)),
        ],
        out_shape=[
            jax.ShapeDtypeStruct((b, MLA_HEADS, S, QK_DIM), BF16),
            jax.ShapeDtypeStruct((b, S, KV_LORA), F32),
            jax.ShapeDtypeStruct((b, S, QK_ROPE), F32),
        ],
        compiler_params=_cparams("parallel", "parallel"),
        name="mla_prep",
    )(x, p["w_mla"], p["q_norm"], p["kv_norm"], p["w_q"], cos_q, sin_q, cos_k, sin_k)

    if past_lat is None:
        lat_all, kr_all, q_off, sk_valid = lat, kr, 0, S
        tk = min(512, S)
    else:
        past_len = past_lat.shape[1]
        sk_valid = past_len + S
        tk = -(-sk_valid // LANES) * LANES
        pad = tk - sk_valid
        lat_all = jnp.concatenate([past_lat, lat, jnp.zeros((b, pad, KV_LORA), F32)], axis=1)
        kr_all = jnp.concatenate([past_kr, kr, jnp.zeros((b, pad, QK_ROPE), F32)], axis=1)
        q_off = past_len
    Sk = lat_all.shape[1]
    tku = tk
    k, vt, ksq = pl.pallas_call(
        _kv_up_kernel,
        grid=(b, Sk // tku),
        in_specs=[
            pl.BlockSpec((1, tku, KV_LORA), lambda bi, si: (bi, si, 0)),
            pl.BlockSpec((1, tku, QK_ROPE), lambda bi, si: (bi, si, 0)),
            _full((KV_LORA, MLA_HEADS * QK_NOPE)), _full((MLA_HEADS * V_DIM, KV_LORA)),
        ],
        out_specs=[
            pl.BlockSpec((1, MLA_HEADS, tku, QK_DIM), lambda bi, si: (bi, 0, si, 0)),
            pl.BlockSpec((1, MLA_HEADS, VT_ROWS, tku), lambda bi, si: (bi, 0, 0, si)),
            pl.BlockSpec((1, 1, MLA_HEADS, LANES), lambda bi, si: (bi, si, 0, 0)),
        ],
        out_shape=[
            jax.ShapeDtypeStruct((b, MLA_HEADS, Sk, QK_DIM), BF16),
            jax.ShapeDtypeStruct((b, MLA_HEADS, VT_ROWS, Sk), BF16),
            jax.ShapeDtypeStruct((b, Sk // tku, MLA_HEADS, LANES), F32),
        ],
        compiler_params=_cparams("parallel", "parallel"),
        name="mla_kv_up",
    )(lat_all, kr_all, p["w_uk"], p["w_uvt"])

    tq = min(512, S)
    nq, nk = S // tq, Sk // tk
    pairs = [(qi, kj) for qi in range(nq) for kj in range(_last_kv_block(qi, tq, tk, q_off, nk) + 1)]
    qi_tab = jnp.asarray([pq for pq, _ in pairs], jnp.int32)
    kj_tab = jnp.asarray([pk for _, pk in pairs], jnp.int32)
    last_tab = jnp.asarray([int(pk == _last_kv_block(pq, tq, tk, q_off, nk)) for pq, pk in pairs], jnp.int32)
    kmax = jnp.sqrt(jnp.swapaxes(ksq[:, :, :, 0], 1, 2)).reshape(-1)
    y = pl.pallas_call(
        functools.partial(_flash_kernel, tq=tq, tk=tk, q_off=q_off, sk_valid=sk_valid, nk=nk),
        grid_spec=pltpu.PrefetchScalarGridSpec(
            num_scalar_prefetch=4,
            grid=(b, len(pairs)),
            in_specs=[
                pl.BlockSpec((1, MLA_HEADS, tq, QK_DIM), lambda bi, t, qt, kt, lt, km: (bi, 0, qt[t], 0)),
                pl.BlockSpec((1, MLA_HEADS, tk, QK_DIM), lambda bi, t, qt, kt, lt, km: (bi, 0, kt[t], 0)),
                pl.BlockSpec((1, MLA_HEADS, VT_ROWS, tk), lambda bi, t, qt, kt, lt, km: (bi, 0, 0, kt[t])),
            ],
            out_specs=pl.BlockSpec((1, tq, MLA_HEADS * V_DIM), lambda bi, t, qt, kt, lt, km: (bi, qt[t], 0)),
            scratch_shapes=[
                pltpu.VMEM((MLA_HEADS, 1, tq), F32),
                pltpu.VMEM((MLA_HEADS, 1, tq), F32),
                pltpu.VMEM((MLA_HEADS, 1, tq), F32),
                pltpu.VMEM((MLA_HEADS, VT_ROWS, tq), F32),
                pltpu.SMEM((1,), jnp.int32),
            ],
        ),
        out_shape=jax.ShapeDtypeStruct((b, S, MLA_HEADS * V_DIM), BF16),
        compiler_params=_cparams("parallel", "arbitrary"),
        name="mla_flash",
    )(qi_tab, kj_tab, last_tab, kmax, q, k, vt)
    return y, lat, kr


def _rwkv_prep_kernel(x_ref, sp_ref, w_ref, mu_ref, w0_ref, a0_ref, kk_ref, ka_ref, rk_ref, ww2_ref, wa2_ref, wg2_ref,
                      hsum_ref, r_out, lw_out, k_out, v_out, kk_out, a_out, g_out, bon_out, sh_out, prev_sc):
    si = pl.program_id(1)

    @pl.when(si == 0)
    def _():
        prev_sc[...] = sp_ref[0]

    u = _dot(x_ref[0], w_ref[...])
    ts = u.shape[0]
    row = lax.broadcasted_iota(jnp.int32, u.shape, 0)
    shifted = jnp.where(row == 0, prev_sc[...], pltpu.roll(u, 1, axis=0))
    prev_sc[...] = u[ts - 1:ts, :]
    sh_out[0] = u[ts - 1:ts, :]
    m = u + (shifted - u) * mu_ref[...]
    r = m[:, :RWKV_DIM]
    k = m[:, RWKV_DIM:2 * RWKV_DIM]
    v = m[:, 2 * RWKV_DIM:3 * RWKV_DIM]
    o = 3 * RWKV_DIM
    wl = m[:, o:o + W_LORA]
    al = m[:, o + W_LORA:o + W_LORA + A_LORA]
    gl = m[:, o + W_LORA + A_LORA:]
    d = w0_ref[...] + _dot(jnp.tanh(wl), ww2_ref[...])
    lw_out[0] = -jnp.exp(-_softplus(-d) - 0.5)
    a = _sigmoid(a0_ref[...] + _dot(al, wa2_ref[...]))
    g_out[0] = _dot(_sigmoid(gl), wg2_ref[...])
    hsum = hsum_ref[...]
    kk = k * kk_ref[...]
    kk_out[0] = kk * lax.rsqrt(_dot_split_lhs(kk * kk, hsum) + 1e-12)
    kh = k * (1.0 + (a - 1.0) * ka_ref[...])
    bon_out[0] = _dot_split_lhs(r * kh * rk_ref[...], hsum) * v
    r_out[0] = r
    k_out[0] = kh
    v_out[0] = v
    a_out[0] = a


def _rwkv_chunk_prepare(ins, c):
    r, lw, k, v, kk, a = (list(t) for t in zip(*ins))
    n = len(ins)
    L = r[0].shape[0]
    L2 = 2 * L
    lo = c["lane_lo"]

    def stack(xv):
        return jnp.concatenate([jnp.where(lo, xv, 0.0), jnp.where(lo, 0.0, xv)], axis=0)

    g = [_dot_exact_lhs(c["tri"], x) for x in lw]
    gl = [x[L - 1:L, :] for x in g]
    e_neg = [jnp.exp(-x) for x in g]
    at_s = [stack(-kk[i] * jnp.exp(g[i] - lw[i])) for i in range(n)]
    rt_s = [stack(r[i] * jnp.exp(g[i])) for i in range(n)]
    beta = [kk[i] * a[i] for i in range(n)]
    mm = []
    for i in range(n):
        bt = beta[i] * e_neg[i]
        kt = k[i] * e_neg[i]
        mm.append(_dot_nt(jnp.concatenate([at_s[i], rt_s[i]], axis=0), jnp.concatenate([bt, bt, kt, kt], axis=0)))
    nmat = [jnp.where(c["strict"], x[:L2, :L2], 0.0) for x in mm]
    mak = [jnp.where(c["strict"], x[:L2, L2:], 0.0) for x in mm]
    mrb = [jnp.where(c["incl"], x[L2:, :L2], 0.0) for x in mm]
    mrk = [jnp.where(c["incl"], x[L2:, L2:], 0.0) for x in mm]
    xinv = [c["eye2"] + x for x in nmat]
    pw = nmat
    for _ in range(int(math.log2(L)) - 1):
        pw = [_dot(x, x) for x in pw]
        xinv = [xinv[i] + _dot(xinv[i], pw[i]) for i in range(n)]
    vs = [stack(x) for x in v]
    w1 = [_dot(mak[i], vs[i]) for i in range(n)]
    au = [_dot(xinv[i], jnp.concatenate([at_s[i], w1[i]], axis=1)) for i in range(n)]
    ry = [_dot(mrb[i], au[i]) for i in range(n)]
    mv = [_dot(mrk[i], vs[i]) for i in range(n)]
    rh = [rt_s[i] + ry[i][:, :LANES] for i in range(n)]
    yh = [ry[i][:, LANES:] + mv[i] for i in range(n)]
    bs = [stack(beta[i] * jnp.exp(gl[i] - g[i])) for i in range(n)]
    ks = [stack(k[i] * jnp.exp(gl[i] - g[i])) for i in range(n)]
    pt_lr = [_dot_tn(bs[i], au[i][:, :LANES]) for i in range(n)]
    qt = [_dot_tn(jnp.concatenate([bs[i], ks[i]], axis=0), jnp.concatenate([au[i][:, LANES:], vs[i]], axis=0)) for i in range(n)]
    return [(rh[i], yh[i], pt_lr[i], qt[i], gl[i]) for i in range(n)]


def _rwkv_scan_kernel(r_ref, lw_ref, k_ref, v_ref, kk_ref, a_ref, g_ref, bon_ref, gng_ref, gnb_ref, z0_ref,
                      y_ref, zf_ref, z_sc, *, L, n_chunks):
    si = pl.program_id(1)

    @pl.when(si == 0)
    def _():
        z_sc[...] = z0_ref[0]

    L2 = 2 * L
    ri = lax.broadcasted_iota(jnp.int32, (L2, L2), 0)
    ci = lax.broadcasted_iota(jnp.int32, (L2, L2), 1)
    same = (ri // L) == (ci // L)
    rl = lax.broadcasted_iota(jnp.int32, (L, L), 0)
    cl = lax.broadcasted_iota(jnp.int32, (L, L), 1)
    r128 = lax.broadcasted_iota(jnp.int32, (LANES, LANES), 0)
    c128 = lax.broadcasted_iota(jnp.int32, (LANES, LANES), 1)
    consts = dict(
        tri=(cl <= rl).astype(BF16),
        strict=jnp.logical_and(same, (ci % L) < (ri % L)),
        incl=jnp.logical_and(same, (ci % L) <= (ri % L)),
        eye2=(ri == ci).astype(F32),
        lane_lo=lax.broadcasted_iota(jnp.int32, (L, LANES), 1) < RWKV_HEAD,
    )
    diag = r128 == c128
    gmean = jnp.where((r128 // RWKV_HEAD) == (c128 // RWKV_HEAD), 1.0 / RWKV_HEAD, 0.0).astype(BF16)
    npair = RWKV_HEADS // 2

    def head_mean(xv):
        hi = xv.astype(BF16)
        lo = (xv - hi.astype(F32)).astype(BF16)
        return jnp.dot(hi, gmean, preferred_element_type=F32) + jnp.dot(lo, gmean, preferred_element_type=F32)

    cpi = 4 if n_chunks % 4 == 0 else (2 if n_chunks % 2 == 0 else 1)
    lanes = [slice(p * LANES, (p + 1) * LANES) for p in range(npair)]

    def chunk(ci_, carry):
        rows = [pl.ds(pl.multiple_of((ci_ * cpi + j) * L, L), L) for j in range(cpi)]
        ins = [(r_ref[0, rw, ln], lw_ref[0, rw, ln], k_ref[0, rw, ln], v_ref[0, rw, ln], kk_ref[0, rw, ln], a_ref[0, rw, ln])
               for rw in rows for ln in lanes]
        prep = _rwkv_chunk_prepare(ins, consts)
        z = [z_sc[p] for p in range(npair)]
        for j in range(cpi):
            pj = prep[j * npair:(j + 1) * npair]
            ys = [_dot(pj[p][0], z[p]) + pj[p][1] for p in range(npair)]
            zlr = [_dot(pj[p][2], z[p]) for p in range(npair)]
            y = [x[:L] + x[L:] for x in ys]
            mean = [head_mean(x) for x in y]
            yc = [y[p] - mean[p] for p in range(npair)]
            var = [head_mean(x * x) for x in yc]
            for p in range(npair):
                ln = lanes[p]
                gcol = jnp.sum(jnp.where(diag, jnp.broadcast_to(jnp.exp(pj[p][4]), (LANES, LANES)), 0.0), axis=1, keepdims=True)
                z[p] = gcol * z[p] + zlr[p] + pj[p][3]
                o = yc[p] * lax.rsqrt(var[p] + RWKV_GN_EPS) * gng_ref[:, ln] + gnb_ref[:, ln]
                y_ref[0, rows[j], ln] = ((o + bon_ref[0, rows[j], ln]) * g_ref[0, rows[j], ln]).astype(y_ref.dtype)
        for p in range(npair):
            z_sc[p] = z[p]
        return carry

    lax.fori_loop(0, n_chunks // cpi, chunk, 0)

    @pl.when(si == pl.num_programs(1) - 1)
    def _():
        zf_ref[0] = z_sc[...]


def _rwkv_branch(x, s0, shift_prev, p):
    b, S, _ = x.shape
    ts = min(512, S)
    tok = lambda w: pl.BlockSpec((1, ts, w), lambda bi, si: (bi, si, 0))
    row = lambda w: pl.BlockSpec((1, 1, w), lambda bi, si: (bi, 0, 0))
    outs = pl.pallas_call(
        _rwkv_prep_kernel,
        grid=(b, S // ts),
        in_specs=[tok(D_MODEL), row(RWKV_COLS), _full((D_MODEL, RWKV_COLS)), _full((1, RWKV_COLS))]
        + [_full((1, RWKV_DIM))] * 5
        + [_full((W_LORA, RWKV_DIM)), _full((A_LORA, RWKV_DIM)), _full((G_LORA, RWKV_DIM)), _full((RWKV_DIM, RWKV_DIM))],
        out_specs=[tok(RWKV_DIM)] * 8 + [row(RWKV_COLS)],
        out_shape=[jax.ShapeDtypeStruct((b, S, RWKV_DIM), F32)] * 8 + [jax.ShapeDtypeStruct((b, 1, RWKV_COLS), F32)],
        scratch_shapes=[pltpu.VMEM((1, RWKV_COLS), F32)],
        compiler_params=_cparams("parallel", "arbitrary"),
        name="rwkv_prep",
    )(x, shift_prev, p["w_rwkv"], p["mu"], p["w0"], p["a0"], p["k_k"], p["k_a"], p["r_k"], p["w_w2"], p["w_a2"], p["w_g2"],
      p["head_sum"])
    r, lw, kh, v, kk, a, g, bonus, shift_new = outs

    L = min(CHUNK, S)
    tb = min(4 * L, S)
    npair = RWKV_HEADS // 2
    zt = jnp.swapaxes(s0.astype(F32), 2, 3).reshape(b, npair, 2, RWKV_HEAD, RWKV_HEAD)
    z0 = jnp.einsum("bpikv,ij->bpikjv", zt, jnp.eye(2, dtype=F32)).reshape(b, npair, LANES, LANES)
    tokb = lambda: pl.BlockSpec((1, tb, RWKV_DIM), lambda bi, si: (bi, si, 0))
    zspec = pl.BlockSpec((1, npair, LANES, LANES), lambda bi, si: (bi, 0, 0, 0))
    y, zf = pl.pallas_call(
        functools.partial(_rwkv_scan_kernel, L=L, n_chunks=tb // L),
        grid=(b, S // tb),
        in_specs=[tokb() for _ in range(8)] + [_full((1, RWKV_DIM)), _full((1, RWKV_DIM)), zspec],
        out_specs=[tokb(), zspec],
        out_shape=[jax.ShapeDtypeStruct((b, S, RWKV_DIM), BF16), jax.ShapeDtypeStruct((b, npair, LANES, LANES), F32)],
        scratch_shapes=[pltpu.VMEM((npair, LANES, LANES), F32)],
        compiler_params=_cparams("parallel", "arbitrary"),
        name="rwkv_scan",
    )(r, lw, kh, v, kk, a, g, bonus, p["gn_g"], p["gn_b"], z0)
    zd = jnp.einsum("bpikiv->bpikv", zf.reshape(b, npair, 2, RWKV_HEAD, 2, RWKV_HEAD))
    s_new = jnp.swapaxes(zd.reshape(b, RWKV_HEADS, RWKV_HEAD, RWKV_HEAD), 2, 3)
    return y, s_new, shift_new


def _ssd_kernel(x_ref, w_ref, s0_ref, cp_ref, cw_ref, cb_ref, dtb_ref, a_ref, dsk_ref, ng_ref,
                y_ref, sf_ref, ct_ref, st_sc, tail_sc, *, L):
    si = pl.program_id(1)

    @pl.when(si == 0)
    def _():
        st_sc[...] = s0_ref[0]
        tail_sc[...] = cp_ref[0]

    u = _dot(x_ref[0], w_ref[...])
    z = u[:, :SSM_DIM]
    xbc = u[:, SSM_DIM:SSM_DIM + CONV_DIM]
    dtr = u[:, SSM_DIM + CONV_DIM:]
    tail = tail_sc[...]
    row = lax.broadcasted_iota(jnp.int32, xbc.shape, 0)
    sh1 = jnp.where(row == 0, tail[7:8], pltpu.roll(xbc, 1, axis=0))
    sh2 = jnp.where(row == 0, tail[6:7], pltpu.roll(sh1, 1, axis=0))
    sh3 = jnp.where(row == 0, tail[5:6], pltpu.roll(sh2, 1, axis=0))
    tail_sc[...] = xbc[L - 8:L]
    ct_ref[0] = xbc[L - 8:L]
    conv = cb_ref[...] + cw_ref[3:4] * xbc + cw_ref[2:3] * sh1 + cw_ref[1:2] * sh2 + cw_ref[0:1] * sh3
    act = conv * _sigmoid(conv)
    xs = act[:, :SSM_DIM]
    gw = SSM_STATE
    dt = _softplus(dtr + dtb_ref[...])
    a = dt * a_ref[...]
    rl = lax.broadcasted_iota(jnp.int32, (L, L), 0)
    cl = lax.broadcasted_iota(jnp.int32, (L, L), 1)
    causal = cl <= rl
    cum = _dot_exact_lhs(causal.astype(BF16), a)
    sel = (lax.broadcasted_iota(jnp.int32, (16, LANES), 0) == lax.broadcasted_iota(jnp.int32, (16, LANES), 1)).astype(BF16)
    cum_t = _dot_nt_exact_lhs(sel, cum)
    dt_t = _dot_nt_exact_lhs(sel, dt)
    ys = []
    hpg = SSM_HEADS // SSM_GROUPS
    for gi in range(SSM_GROUPS):
        bg = act[:, SSM_DIM + gi * gw:SSM_DIM + (gi + 1) * gw]
        cg = act[:, SSM_DIM + SSM_GROUPS * gw + gi * gw:SSM_DIM + SSM_GROUPS * gw + (gi + 1) * gw]
        cb = _dot_nt(cg, bg)
        for h in range(gi * hpg, (gi + 1) * hpg):
            xh = xs[:, h * SSM_HEADDIM:(h + 1) * SSM_HEADDIM]
            cc = cum[:, h:h + 1]
            seg = cc - cum_t[h:h + 1, :]
            dec = jnp.where(causal, jnp.exp(jnp.minimum(seg, 0.0)), 0.0)
            sc = cb * dec * dt_t[h:h + 1, :]
            st = st_sc[h]
            yh = _dot(sc, xh) + _dot_nt(cg, st) * jnp.exp(cc)
            clast = cum[L - 1:L, h:h + 1]
            wcol = jnp.exp(clast - cc) * dt[:, h:h + 1]
            st_sc[h] = st * jnp.exp(clast) + _dot_tn_split(xh * wcol, bg)
            ys.append(yh)
    y = jnp.concatenate(ys, axis=1) + dsk_ref[...] * xs
    y = y * (z * _sigmoid(z))
    gdim = SSM_DIM // SSM_GROUPS
    outs = []
    for gi in range(SSM_GROUPS):
        yg = y[:, gi * gdim:(gi + 1) * gdim]
        outs.append(yg * lax.rsqrt(jnp.mean(yg * yg, axis=-1, keepdims=True) + RMS_EPS))
    y_ref[0] = (jnp.concatenate(outs, axis=1) * ng_ref[...]).astype(y_ref.dtype)

    @pl.when(si == pl.num_programs(1) - 1)
    def _():
        sf_ref[0] = st_sc[...]


def _ssm_branch(x, s0, conv_prev, p):
    b, S, _ = x.shape
    L = min(256, S)
    cp = jnp.concatenate([jnp.zeros((b, 8 - (CONV_W - 1), CONV_DIM), F32), conv_prev.astype(F32)], axis=1)
    sspec = pl.BlockSpec((1, SSM_HEADS, SSM_HEADDIM, SSM_STATE), lambda bi, si: (bi, 0, 0, 0))
    cspec = pl.BlockSpec((1, 8, CONV_DIM), lambda bi, si: (bi, 0, 0))
    y, s_new, ctail = pl.pallas_call(
        functools.partial(_ssd_kernel, L=L),
        grid=(b, S // L),
        in_specs=[
            pl.BlockSpec((1, L, D_MODEL), lambda bi, si: (bi, si, 0)),
            _full((D_MODEL, SSM_PROJ_COLS)), sspec, cspec,
            _full((CONV_W, CONV_DIM)), _full((1, CONV_DIM)), _full((1, LANES)), _full((1, LANES)),
            _full((1, SSM_DIM)), _full((1, SSM_DIM)),
        ],
        out_specs=[pl.BlockSpec((1, L, SSM_DIM), lambda bi, si: (bi, si, 0)), sspec, cspec],
        out_shape=[
            jax.ShapeDtypeStruct((b, S, SSM_DIM), BF16),
            jax.ShapeDtypeStruct((b, SSM_HEADS, SSM_HEADDIM, SSM_STATE), F32),
            jax.ShapeDtypeStruct((b, 8, CONV_DIM), F32),
        ],
        scratch_shapes=[pltpu.VMEM((SSM_HEADS, SSM_HEADDIM, SSM_STATE), F32), pltpu.VMEM((8, CONV_DIM), F32)],
        compiler_params=_cparams("parallel", "arbitrary"),
        name="ssd",
    )(x, p["w_ssm"], s0.astype(F32), cp, p["conv_w"], p["conv_b"], p["dt_bias"], p["a_neg"], p["d_skip"], p["ssm_norm_g"])
    return y, s_new, ctail[:, 8 - (CONV_W - 1):]


def _merge_kernel(x_ref, ym_ref, yr_ref, ys_ref, wg_ref, wb_ref, wo_ref, g_ref, b_ref, rw_ref, rb_ref,
                  h_ref, hb_ref, ti_ref, tg_ref):
    x = x_ref[...]
    gates = _sigmoid(_dot(x, wg_ref[...]))
    mix = gates[:, :D_MODEL] * _dot(ym_ref[...], wb_ref[0])
    mix = mix + gates[:, D_MODEL:2 * D_MODEL] * _dot(yr_ref[...], wb_ref[1])
    mix = mix + gates[:, 2 * D_MODEL:] * _dot(ys_ref[...], wb_ref[2])
    h = _layer_norm(DEEPNORM_ALPHA * x + _dot(mix, wo_ref[...]), g_ref[...], b_ref[...])
    h_ref[...] = h
    hb_ref[...] = h.astype(BF16)
    logits = _dot(h, rw_ref[...]) + rb_ref[...]
    lane = lax.broadcasted_iota(jnp.int32, logits.shape, 1)
    idx_out = jnp.zeros(logits.shape, jnp.int32)
    val_out = jnp.zeros(logits.shape, F32)
    top = None
    den = None
    for kth in range(TOP_K):
        mval = jnp.max(logits, axis=-1, keepdims=True)
        midx = jnp.min(jnp.where(logits == mval, lane, LANES), axis=-1, keepdims=True)
        if kth == 0:
            top = mval
            e = jnp.ones_like(mval)
            den = e
        else:
            e = jnp.exp(mval - top)
            den = den + e
        idx_out = jnp.where(lane == kth, midx, idx_out)
        val_out = jnp.where(lane == kth, e, val_out)
        logits = jnp.where(lane == midx, NEG_BIG * 2.0, logits)
    ti_ref[...] = idx_out
    tg_ref[...] = val_out / den


def _merge(x2, ym, yr, ys, p):
    T = x2.shape[0]
    tm = min(512, T)
    tok = lambda w: pl.BlockSpec((tm, w), lambda i: (i, 0))
    return pl.pallas_call(
        _merge_kernel,
        grid=(T // tm,),
        in_specs=[tok(D_MODEL), tok(BRANCH_DIM), tok(BRANCH_DIM), tok(BRANCH_DIM),
                  _full((D_MODEL, N_BRANCH * D_MODEL)), _full((N_BRANCH, BRANCH_DIM, D_MODEL)), _full((D_MODEL, D_MODEL)),
                  _full((1, D_MODEL)), _full((1, D_MODEL)), _full((D_MODEL, LANES)), _full((1, LANES))],
        out_specs=[tok(D_MODEL), tok(D_MODEL), tok(LANES), tok(LANES)],
        out_shape=[jax.ShapeDtypeStruct((T, D_MODEL), F32), jax.ShapeDtypeStruct((T, D_MODEL), BF16),
                   jax.ShapeDtypeStruct((T, LANES), jnp.int32), jax.ShapeDtypeStruct((T, LANES), F32)],
        compiler_params=_cparams("parallel"),
        name="merge_router",
    )(x2, ym, yr, ys, p["w_gate"], p["w_branch"], p["w_out"], p["ln1_g"], p["ln1_b"], p["router_w"], p["router_b"])


def _expert_kernel(be_ref, nu_ref, x_ref, wgu_ref, bgu_ref, wd_ref, bd_ref, o_ref, wgu_sc, wd_sc):
    i = pl.program_id(0)
    used = i < nu_ref[0]
    new_expert = jnp.logical_or(i == 0, be_ref[i] != be_ref[jnp.maximum(i - 1, 0)])

    @pl.when(jnp.logical_and(used, new_expert))
    def _():
        wgu_sc[...] = wgu_ref[0, 0].astype(BF16)
        wd_sc[...] = wd_ref[0, 0].astype(BF16)

    @pl.when(used)
    def _():
        hgu = jnp.dot(x_ref[...], wgu_sc[...], preferred_element_type=F32) + bgu_ref[0]
        gate = jnp.minimum(hgu[:, :D_FF], SWIGLU_LIMIT)
        up = jnp.clip(hgu[:, D_FF:], -SWIGLU_LIMIT, SWIGLU_LIMIT)
        hid = gate * _sigmoid(SWIGLU_ALPHA * gate) * (up + 1.0)
        o_ref[...] = (jnp.dot(hid.astype(BF16), wd_sc[...], preferred_element_type=F32) + bd_ref[0]).astype(o_ref.dtype)

    @pl.when(jnp.logical_not(used))
    def _():
        o_ref[...] = jnp.zeros(o_ref.shape, o_ref.dtype)


def _combine_kernel(y0_ref, y1_ref, y2_ref, y3_ref, tg_ref, h_ref, g_ref, b_ref, o_ref):
    tg = tg_ref[...]
    f = tg[:, 0:1] * y0_ref[...].astype(F32)
    for kth, y_ref in ((1, y1_ref), (2, y2_ref), (3, y3_ref)):
        f = f + tg[:, kth:kth + 1] * y_ref[...].astype(F32)
    o_ref[...] = _layer_norm(DEEPNORM_ALPHA * h_ref[...] + f, g_ref[...], b_ref[...])


def _moe_experts(h_bf_parts, top_i_parts, p, layer):
    experts = jnp.arange(N_EXPERTS, dtype=jnp.int32)
    parts = []
    for top_i in top_i_parts:
        n = top_i.shape[0] * TOP_K
        flat_e = top_i[:, :TOP_K].reshape(-1)
        ids = jnp.arange(n, dtype=jnp.int32)
        skey = jnp.sort(flat_e * n + ids)
        sorted_e = skey // n
        cnt = jnp.sum((flat_e[:, None] == experts[None, :]).astype(jnp.int32), axis=0)
        parts.append(dict(n=n, ids=ids, sorted_e=sorted_e, order=skey - sorted_e * n, counts=cnt, start=jnp.cumsum(cnt) - cnt))
    counts = sum(pt["counts"] for pt in parts)
    n_assign = sum(pt["n"] for pt in parts)
    T = n_assign // TOP_K
    padded = (counts + MOE_ROWS - 1) // MOE_ROWS * MOE_ROWS
    pad_end = jnp.cumsum(padded)
    pad_start = pad_end - padded
    n_blocks = -(-(n_assign + N_EXPERTS * (MOE_ROWS - 1)) // MOE_ROWS)
    n_rows = n_blocks * MOE_ROWS
    blk_row0 = jnp.arange(n_blocks, dtype=jnp.int32) * MOE_ROWS
    block_e = jnp.minimum(jnp.sum((pad_end[None, :] <= blk_row0[:, None]).astype(jnp.int32), axis=1), N_EXPERTS - 1)
    n_used = (pad_end[-1:] // MOE_ROWS).astype(jnp.int32)
    within = (blk_row0 - pad_start[block_e])[:, None] + jnp.arange(MOE_ROWS, dtype=jnp.int32)[None, :]
    row_tok = (blk_row0[:, None] + jnp.arange(MOE_ROWS, dtype=jnp.int32)[None, :]) % T
    seg_off = jnp.zeros((N_EXPERTS,), jnp.int32)
    tok_off = 0
    for pt in parts:
        pt["seg_off"] = seg_off
        local = within - seg_off[block_e][:, None]
        mine = jnp.logical_and(local >= 0, local < pt["counts"][block_e][:, None])
        src = jnp.clip(pt["start"][block_e][:, None] + local, 0, pt["n"] - 1)
        tok = pt["order"][src.reshape(-1)].reshape(n_blocks, MOE_ROWS) // TOP_K + tok_off
        row_tok = jnp.where(mine, tok, row_tok)
        seg_off = seg_off + pt["counts"]
        tok_off += pt["n"] // TOP_K
    h_bf = h_bf_parts[0] if len(h_bf_parts) == 1 else jnp.concatenate(h_bf_parts, axis=0)
    xb = h_bf[row_tok.reshape(-1)]

    yb = pl.pallas_call(
        _expert_kernel,
        grid_spec=pltpu.PrefetchScalarGridSpec(
            num_scalar_prefetch=2,
            grid=(n_blocks,),
            in_specs=[
                pl.BlockSpec((MOE_ROWS, D_MODEL), lambda i, be, nu: (i, 0)),
                pl.BlockSpec((1, 1, D_MODEL, 2 * D_FF), lambda i, be, nu: (layer, be[i], 0, 0)),
                pl.BlockSpec((1, 1, 2 * D_FF), lambda i, be, nu: (be[i], 0, 0)),
                pl.BlockSpec((1, 1, D_FF, D_MODEL), lambda i, be, nu: (layer, be[i], 0, 0)),
                pl.BlockSpec((1, 1, D_MODEL), lambda i, be, nu: (be[i], 0, 0)),
            ],
            out_specs=pl.BlockSpec((MOE_ROWS, D_MODEL), lambda i, be, nu: (i, 0)),
            scratch_shapes=[pltpu.VMEM((D_MODEL, 2 * D_FF), BF16), pltpu.VMEM((D_FF, D_MODEL), BF16)],
        ),
        out_shape=jax.ShapeDtypeStruct((n_rows, D_MODEL), BF16),
        compiler_params=_cparams("arbitrary"),
        name="moe_experts",
    )(block_e, n_used, xb, p["w_gu"], p["b_gu"], p["w_down"], p["b_down"])

    dests = []
    for pt in parts:
        se = pt["sorted_e"]
        dest_sorted = pad_start[se] + pt["seg_off"][se] + (pt["ids"] - pt["start"][se])
        _, dest = lax.sort((pt["order"], dest_sorted), num_keys=1)
        dests.append(dest.reshape(pt["n"] // TOP_K, TOP_K))
    return yb, dests


def _moe_combine(yb, dest, top_g, h, p):
    T = h.shape[0]
    ygs = [yb[dest[:, kth]] for kth in range(TOP_K)]
    tm = min(512, T)
    tok = lambda w: pl.BlockSpec((tm, w), lambda i: (i, 0))
    return pl.pallas_call(
        _combine_kernel,
        grid=(T // tm,),
        in_specs=[tok(D_MODEL)] * TOP_K + [tok(LANES), tok(D_MODEL), _full((1, D_MODEL)), _full((1, D_MODEL))],
        out_specs=tok(D_MODEL),
        out_shape=jax.ShapeDtypeStruct((T, D_MODEL), F32),
        compiler_params=_cparams("parallel"),
        name="moe_combine_ln",
    )(*ygs, top_g, h, p["ln2_g"], p["ln2_b"])


def _prep_layer(l, w):
    w_in = w["w_in"][l]
    kr_cols = w_in[:, MLA_COLS - QK_ROPE:MLA_COLS]
    w_mla = jnp.concatenate([w_in[:, :MLA_COLS], _rot_half(kr_cols),
                             jnp.zeros((D_MODEL, MLA_PROJ_COLS - MLA_COLS - QK_ROPE), F32)], axis=1)
    wq = w["mla_w_uq"][l].reshape(Q_LORA, MLA_HEADS, QK_DIM)
    wq_rope = wq[:, :, QK_NOPE:]
    w_q = jnp.concatenate([wq[:, :, :QK_NOPE].reshape(Q_LORA, -1), wq_rope.reshape(Q_LORA, -1),
                           _rot_half(wq_rope).reshape(Q_LORA, -1)], axis=1)
    wkv = w["mla_w_ukv"][l].reshape(KV_LORA, MLA_HEADS, QK_NOPE + V_DIM)
    o_r = MLA_COLS
    o_s = o_r + RWKV_COLS
    o_g = o_s + SSM_COLS
    w_ssm = jnp.concatenate([w_in[:, o_s:o_s + SSM_DIM + CONV_DIM], w_in[:, o_s + SSM_DIM + CONV_DIM:o_g],
                             jnp.zeros((D_MODEL, LANES - SSM_HEADS), F32)], axis=1)
    pad8 = lambda v: jnp.concatenate([v.astype(F32), jnp.zeros((LANES - SSM_HEADS,), F32)])[None]
    row = lambda v: v.astype(F32)[None]
    hid = jnp.arange(RWKV_DIM) // RWKV_HEAD
    rw = jnp.concatenate([w["router_w"][l], jnp.zeros((D_MODEL, LANES - N_EXPERTS), F32)], axis=1)
    return dict(
        w_mla=w_mla.astype(BF16), q_norm=row(w["mla_q_norm"][l]), kv_norm=row(w["mla_kv_norm"][l]), w_q=w_q.astype(BF16),
        w_uk=wkv[:, :, :QK_NOPE].reshape(KV_LORA, -1).astype(BF16),
        w_uvt=wkv[:, :, QK_NOPE:].reshape(KV_LORA, -1).T.astype(BF16),
        w_rwkv=w_in[:, o_r:o_s].astype(BF16), mu=row(w["rwkv_mu"][l]), w0=row(w["rwkv_w0"][l]), a0=row(w["rwkv_a0"][l]),
        k_k=row(w["rwkv_k_k"][l]), k_a=row(w["rwkv_k_a"][l]), r_k=row(w["rwkv_r_k"][l]),
        w_w2=w["rwkv_w_w2"][l].astype(BF16), w_a2=w["rwkv_w_a2"][l].astype(BF16), w_g2=w["rwkv_w_g2"][l].astype(BF16),
        head_sum=(hid[:, None] == hid[None, :]).astype(BF16),
        gn_g=row(w["rwkv_gn_g"][l]), gn_b=row(w["rwkv_gn_b"][l]),
        w_ssm=w_ssm.astype(BF16), conv_w=w["ssm_conv_w"][l].astype(F32), conv_b=row(w["ssm_conv_b"][l]),
        dt_bias=pad8(w["ssm_dt_bias"][l]), a_neg=pad8(-jnp.exp(w["ssm_a_log"][l].astype(F32))),
        d_skip=row(jnp.repeat(w["ssm_d"][l], SSM_HEADDIM)), ssm_norm_g=row(w["ssm_norm_g"][l]),
        w_gate=w_in[:, o_g:].astype(BF16), w_branch=w["w_branch"][l].astype(BF16), w_out=w["w_out"][l].astype(BF16),
        ln1_g=row(w["ln1_g"][l]), ln1_b=row(w["ln1_b"][l]),
        router_w=rw.astype(BF16),
        router_b=jnp.concatenate([w["router_b"][l].astype(F32), jnp.full((LANES - N_EXPERTS,), NEG_BIG, F32)])[None],
        w_gu=w["expert_w_gu"], b_gu=w["expert_b_gu"][l].astype(F32)[:, None, :],
        w_down=w["expert_w_down"], b_down=w["expert_b_down"][l].astype(F32)[:, None, :], layer=l,
        ln2_g=row(w["ln2_g"][l]), ln2_b=row(w["ln2_b"][l]),
    )


def _mixers_and_router(x, pos0, past_lat, past_kr, rwkv_s0, shift_prev, ssm_s0, conv_prev, p):
    b, S, _ = x.shape
    y_mla, lat, kr = _mla_branch(x, pos0, past_lat, past_kr, p)
    y_rwkv, rwkv_s, shift_new = _rwkv_branch(x, rwkv_s0, shift_prev, p)
    y_ssm, ssm_s, conv_new = _ssm_branch(x, ssm_s0, conv_prev, p)
    T = b * S
    flat = lambda t: t.reshape(T, t.shape[-1])
    routed = _merge(flat(x), flat(y_mla), flat(y_rwkv), flat(y_ssm), p)
    return routed, (lat, kr, rwkv_s, shift_new, ssm_s, conv_new)


def _trunk_layer(x, pos0, past_lat, past_kr, rwkv_s0, shift_prev, ssm_s0, conv_prev, p):
    (h, h_bf, top_i, top_g), states = _mixers_and_router(x, pos0, past_lat, past_kr, rwkv_s0, shift_prev, ssm_s0, conv_prev, p)
    yb, (dest,) = _moe_experts([h_bf], [top_i], p, p["layer"])
    return _moe_combine(yb, dest, top_g, h, p).reshape(x.shape), states


def kernel(x_prompt, x_sample, cache_mla_latent, cache_mla_krope, state_rwkv, state_rwkv_shift, state_ssm, state_ssm_conv, w_in, mla_q_norm, mla_kv_norm, mla_w_uq, mla_w_ukv, rwkv_mu, rwkv_w0, rwkv_w_w2, rwkv_a0, rwkv_w_a2, rwkv_w_g2, rwkv_k_k, rwkv_k_a, rwkv_r_k, rwkv_gn_g, rwkv_gn_b, ssm_conv_w, ssm_conv_b, ssm_dt_bias, ssm_a_log, ssm_d, ssm_norm_g, w_branch, w_out, ln1_g, ln1_b, router_w, router_b, expert_w_gu, expert_b_gu, expert_w_down, expert_b_down, ln2_g, ln2_b):
    w = dict(w_in=w_in, mla_q_norm=mla_q_norm, mla_kv_norm=mla_kv_norm, mla_w_uq=mla_w_uq, mla_w_ukv=mla_w_ukv,
             rwkv_mu=rwkv_mu, rwkv_w0=rwkv_w0, rwkv_w_w2=rwkv_w_w2, rwkv_a0=rwkv_a0, rwkv_w_a2=rwkv_w_a2, rwkv_w_g2=rwkv_w_g2,
             rwkv_k_k=rwkv_k_k, rwkv_k_a=rwkv_k_a, rwkv_r_k=rwkv_r_k, rwkv_gn_g=rwkv_gn_g, rwkv_gn_b=rwkv_gn_b,
             ssm_conv_w=ssm_conv_w, ssm_conv_b=ssm_conv_b, ssm_dt_bias=ssm_dt_bias, ssm_a_log=ssm_a_log, ssm_d=ssm_d,
             ssm_norm_g=ssm_norm_g, w_branch=w_branch, w_out=w_out, ln1_g=ln1_g, ln1_b=ln1_b, router_w=router_w,
             router_b=router_b, expert_w_gu=expert_w_gu, expert_b_gu=expert_b_gu, expert_w_down=expert_w_down,
             expert_b_down=expert_b_down, ln2_g=ln2_g, ln2_b=ln2_b)
    bp = x_prompt.shape[0]
    past_len = cache_mla_latent.shape[2]
    zero_rwkv = jnp.zeros((bp, RWKV_HEADS, RWKV_HEAD, RWKV_HEAD), F32)
    zero_shift = jnp.zeros((bp, 1, RWKV_COLS), F32)
    zero_ssm = jnp.zeros((bp, SSM_HEADS, SSM_HEADDIM, SSM_STATE), F32)
    zero_conv = jnp.zeros((bp, CONV_W - 1, CONV_DIM), F32)
    yp, ys = x_prompt, x_sample
    st_p = [[] for _ in range(6)]
    st_s = [[] for _ in range(6)]
    for l in range(DEPTH):
        p = _prep_layer(l, w)
        (h_p, hb_p, ti_p, tg_p), new_p = _mixers_and_router(yp, 0, None, None, zero_rwkv, zero_shift, zero_ssm, zero_conv, p)
        (h_s, hb_s, ti_s, tg_s), new_s = _mixers_and_router(ys, past_len, cache_mla_latent[l], cache_mla_krope[l], state_rwkv[l],
                                                          state_rwkv_shift[l], state_ssm[l], state_ssm_conv[l], p)
        yb, (dest_p, dest_s) = _moe_experts([hb_p, hb_s], [ti_p, ti_s], p, l)
        yp = _moe_combine(yb, dest_p, tg_p, h_p, p).reshape(yp.shape)
        ys = _moe_combine(yb, dest_s, tg_s, h_s, p).reshape(ys.shape)
        for i in range(6):
            st_p[i].append(new_p[i])
            st_s[i].append(new_s[i])
    outs_p = [jnp.stack(t, axis=0) for t in st_p]
    outs_s = [jnp.stack(t, axis=0) for t in st_s]
    return (yp, ys, *outs_p, *outs_s)
```
